```python
import math
import jax, jax.numpy as jnp
from jax import lax
import numpy as np

D_MODEL = 1024
BATCH = 8
SEQ = 4096
DEPTH = 1

N_MEM = 256
RMS_EPS = 1e-6
NEG_INF = -1e30

HG_EXPAND = 128
HG_HEADS = D_MODEL // HG_EXPAND
HG_DK = HG_EXPAND
HG_DV = D_MODEL // HG_HEADS
HG_WIDTH = HG_HEADS * HG_DV
HG_CHUNK = 64
HG_SCALE = HG_DK ** -0.5

DA_CONFIGS = ((128, 1), (512, 4), (2048, 16))
DA_HEADS_PER_GROUP = 4
DA_HEADS = DA_HEADS_PER_GROUP * len(DA_CONFIGS)
DA_HEAD_DIM = D_MODEL // 8
DA_QKV_WIDTH = DA_HEADS * DA_HEAD_DIM
DA_WIDTH = DA_HEADS_PER_GROUP * DA_HEAD_DIM
DA_SCALE = DA_HEAD_DIM ** -0.5

MEM_HEADS = 4
MEM_HEAD_DIM = D_MODEL // 8
MEM_WIDTH = MEM_HEADS * MEM_HEAD_DIM
MEM_SCALE = MEM_HEAD_DIM ** -0.5

D_FF = ((8 * D_MODEL // 3 + 255) // 256) * 256

IN_SPLITS = (HG_WIDTH,) * 5 + (DA_QKV_WIDTH,) * 3 + (MEM_WIDTH,) + (D_MODEL,) * 3
IN_COLS = sum(IN_SPLITS)
IN_SPLIT_POINTS = tuple(int(p) for p in np.cumsum(IN_SPLITS)[:-1])

kernel_name = "hybrid_hgrn2_dilated_memory_block"


def rmsnorm(x, gain):
    xf = x.astype(jnp.float32)
    y = xf * lax.rsqrt(jnp.mean(xf * xf, axis=-1, keepdims=True) + RMS_EPS)
    return (y * gain.astype(jnp.float32)).astype(x.dtype)


def alibi_slopes(n):
    return (2.0 ** (-8.0 * np.arange(1, n + 1) / n)).astype(np.float32)


def gla_chunkwise(q, k, v, log_f):
    B, H, L, dk = q.shape
    dv = v.shape[-1]
    C = HG_CHUNK
    n = L // C
    q, k, log_f = [t.reshape(B, H, n, C, dk) for t in (q, k, log_f)]
    v = v.reshape(B, H, n, C, dv)
    b = jnp.cumsum(log_f.astype(jnp.float32), axis=3)
    b_last = b[:, :, :, -1:, :]
    q_in = q * jnp.exp(b)
    k_in = k * jnp.exp(-b)
    k_st = k * jnp.exp(b_last - b)
    causal = jnp.tril(jnp.ones((C, C), dtype=bool))
    a = jnp.where(causal, jnp.einsum('bhnti,bhnsi->bhnts', q_in, k_in), 0.0)
    o_intra = jnp.einsum('bhnts,bhnsj->bhntj', a, v)
    chunk_state = jnp.einsum('bhnsi,bhnsj->bhnij', k_st, v)
    decay = jnp.exp(b_last[:, :, :, 0, :])

    def step(S, inp):
        d, s = inp
        return d[..., None] * S + s, S

    S0 = jnp.zeros((B, H, dk, dv), jnp.float32)
    _, S_in = lax.scan(step, S0, (jnp.moveaxis(decay, 2, 0), jnp.moveaxis(chunk_state, 2, 0)))
    S_in = jnp.moveaxis(S_in, 0, 2)
    o_inter = jnp.einsum('bhnti,bhnij->bhntj', q_in, S_in)
    return (o_intra + o_inter).reshape(B, H, L, dv)


def hgrn2_mixer(q, f_fw, f_bw, inp, gate, lb_fw, lb_bw, norm_gain):
    B, L, _ = q.shape

    def to_heads(t):
        return t.reshape(B, L, HG_HEADS, -1).transpose(0, 2, 1, 3)

    qh = to_heads(jax.nn.silu(q)) * HG_SCALE
    vh = to_heads(inp)

    def direction(f_logit, lb, flip):
        forget = lb + (1.0 - lb) * jax.nn.sigmoid(f_logit.astype(jnp.float32))
        kh, logf = to_heads(1.0 - forget), to_heads(jnp.log(forget))
        if flip:
            rev = lambda t: jnp.flip(t, axis=2)
            return rev(gla_chunkwise(rev(qh), rev(kh), rev(vh), rev(logf)))
        return gla_chunkwise(qh, kh, vh, logf)

    o = direction(f_fw, lb_fw, False) + direction(f_bw, lb_bw, True)
    o = o.transpose(0, 2, 1, 3)
    o = rmsnorm(o, norm_gain) * jax.nn.silu(gate.reshape(B, L, HG_HEADS, HG_DV).astype(jnp.float32))
    return o.reshape(B, L, HG_WIDTH).astype(q.dtype)


def dilated_group(q, k, v, dilation, radius, slopes):
    B, L, Hg, dh = q.shape
    d, P = dilation, radius
    Ld = L // d
    nb = -(-Ld // P)
    Lp = nb * P

    def residues(t):
        return t.reshape(B, Ld, d, Hg, dh).transpose(0, 3, 2, 1, 4)

    qr, kr, vr = residues(q), residues(k), residues(v)
    qb = jnp.pad(qr, ((0, 0),) * 3 + ((0, Lp - Ld), (0, 0))).reshape(B, Hg, d, nb, P, dh)
    kv_pad = ((0, 0),) * 3 + ((P, Lp - Ld + P), (0, 0))

    def key_blocks(t):
        tb = jnp.pad(t, kv_pad).reshape(B, Hg, d, nb + 2, P, dh)
        return jnp.concatenate([tb[:, :, :, :-2], tb[:, :, :, 1:-1], tb[:, :, :, 2:]], axis=4)

    kb, vb = key_blocks(kr), key_blocks(vr)
    qi = jnp.arange(P)[:, None]
    kj = jnp.arange(3 * P)[None, :]
    rel = kj - P - qi
    s_pos = jnp.arange(nb)[:, None, None] * P + kj[None] - P
    valid = (jnp.abs(rel) <= radius)[None] & (s_pos >= 0) & (s_pos < Ld)
    dist = (d * jnp.abs(rel)).astype(jnp.float32)
    slopes = jnp.asarray(slopes, jnp.float32)

    scores = jnp.einsum('bhrnid,bhrnjd->bhrnij', qb, kb).astype(jnp.float32) * DA_SCALE
    scores = scores - slopes[:, None, None, None, None] * dist
    scores = jnp.where(valid, scores, NEG_INF)
    lse = jax.nn.logsumexp(scores, axis=-1)
    p = jnp.exp(scores - lse[..., None])
    o = jnp.einsum('bhrnij,bhrnjd->bhrnid', p.astype(vb.dtype), vb)
    o = o.reshape(B, Hg, d, Lp, dh)[:, :, :, :Ld]
    o = o.transpose(0, 3, 2, 1, 4).reshape(B, L, Hg, dh)
    lse = lse.reshape(B, Hg, d, Lp)[:, :, :, :Ld].transpose(0, 3, 2, 1).reshape(B, L, Hg)
    return o, lse


def dilated_mixer(q, k, v, q_gain, k_gain):
    B, L, _ = q.shape
    q = rmsnorm(q.reshape(B, L, DA_HEADS, DA_HEAD_DIM), q_gain)
    k = rmsnorm(k.reshape(B, L, DA_HEADS, DA_HEAD_DIM), k_gain)
    v = v.reshape(B, L, DA_HEADS, DA_HEAD_DIM)
    slopes = alibi_slopes(DA_HEADS)
    outs, lses = [], []
    for g, (window, dilation) in enumerate(DA_CONFIGS):
        lo, hi = g * DA_HEADS_PER_GROUP, (g + 1) * DA_HEADS_PER_GROUP
        o, lse = dilated_group(q[:, :, lo:hi], k[:, :, lo:hi], v[:, :, lo:hi],
                               dilation, window // (2 * dilation), slopes[lo:hi])
        outs.append(o)
        lses.append(lse)
    w = jax.nn.softmax(jnp.stack(lses, axis=0), axis=0)
    o = jnp.sum(w[..., None] * jnp.stack(outs, axis=0).astype(jnp.float32), axis=0)
    return o.reshape(B, L, DA_WIDTH).astype(q.dtype)


def memory_mixer(q, mem_n, w_kv, q_gain, k_gain):
    B, L, _ = q.shape
    M = mem_n.shape[1]
    k, v = jnp.split(mem_n @ w_kv, 2, axis=-1)
    qh = rmsnorm(q.reshape(B, L, MEM_HEADS, MEM_HEAD_DIM), q_gain)
    kh = rmsnorm(k.reshape(B, M, MEM_HEADS, MEM_HEAD_DIM), k_gain)
    vh = v.reshape(B, M, MEM_HEADS, MEM_HEAD_DIM)
    s = jnp.einsum('blhd,bmhd->bhlm', qh, kh).astype(jnp.float32) * MEM_SCALE
    p = jax.nn.softmax(s, axis=-1)
    o = jnp.einsum('bhlm,bmhd->blhd', p.astype(vh.dtype), vh)
    return o.reshape(B, L, MEM_WIDTH)


def setup_inputs(seed: int = 0) -> dict:
    key = jax.random.key(seed)
    ks = jax.random.split(key, 22)

    def w(k, shape, fan_in):
        return jax.random.normal(k, shape, jnp.float32) * fan_in ** -0.5

    def gain(k, shape):
        return 1.0 + 0.02 * jax.random.normal(k, shape, jnp.float32)

    return {
        "x": jax.random.normal(ks[0], (BATCH, SEQ, D_MODEL), jnp.float32),
        "mem": jax.random.normal(ks[1], (BATCH, N_MEM, D_MODEL), jnp.float32),
        "norm_mix_gain": gain(ks[2], (DEPTH, D_MODEL)),
        "norm_mem_gain": gain(ks[3], (DEPTH, D_MODEL)),
        "w_in": w(ks[4], (DEPTH, D_MODEL, IN_COLS), D_MODEL),
        "lb_logits_fw": 0.1 * jax.random.normal(ks[5], (DEPTH + 1, HG_WIDTH), jnp.float32),
        "lb_logits_bw": 0.1 * jax.random.normal(ks[6], (DEPTH + 1, HG_WIDTH), jnp.float32),
        "hg_norm_gain": gain(ks[7], (DEPTH, HG_DV)),
        "da_q_gain": gain(ks[8], (DEPTH, DA_HEAD_DIM)),
        "da_k_gain": gain(ks[9], (DEPTH, DA_HEAD_DIM)),
        "w_mem_kv": w(ks[10], (DEPTH, D_MODEL, 2 * MEM_WIDTH), D_MODEL),
        "mem_q_gain": gain(ks[11], (DEPTH, MEM_HEAD_DIM)),
        "mem_k_gain": gain(ks[12], (DEPTH, MEM_HEAD_DIM)),
        "w_proj_hg": w(ks[13], (DEPTH, HG_WIDTH, D_MODEL), HG_WIDTH),
        "w_proj_da": w(ks[14], (DEPTH, DA_WIDTH, D_MODEL), DA_WIDTH),
        "w_proj_mem": w(ks[15], (DEPTH, MEM_WIDTH, D_MODEL), MEM_WIDTH),
        "w_out": w(ks[16], (DEPTH, D_MODEL, D_MODEL), D_MODEL),
        "norm_ffn_gain": gain(ks[17], (DEPTH, D_MODEL)),
        "w_ffn_in": w(ks[18], (DEPTH, D_MODEL, 2 * D_FF), D_MODEL),
        "w_ffn_out": w(ks[19], (DEPTH, D_FF, D_MODEL), D_FF),
    }


def reference(x, mem, norm_mix_gain, norm_mem_gain, w_in, lb_logits_fw, lb_logits_bw,
              hg_norm_gain, da_q_gain, da_k_gain, w_mem_kv, mem_q_gain, mem_k_gain,
              w_proj_hg, w_proj_da, w_proj_mem, w_out, norm_ffn_gain, w_ffn_in, w_ffn_out):
    lb_fw_table = jnp.cumsum(jax.nn.softmax(lb_logits_fw.astype(jnp.float32), axis=0), axis=0)
    lb_bw_table = jnp.cumsum(jax.nn.softmax(lb_logits_bw.astype(jnp.float32), axis=0), axis=0)
    for l in range(DEPTH):
        h = rmsnorm(x, norm_mix_gain[l])
        proj = h @ w_in[l]
        (hg_q, hg_f_fw, hg_f_bw, hg_i, hg_g, da_q, da_k, da_v, mem_q,
         gate_hg, gate_da, gate_mem) = jnp.split(proj, IN_SPLIT_POINTS, axis=-1)

        o_hg = hgrn2_mixer(hg_q, hg_f_fw, hg_f_bw, hg_i, hg_g,
                           lb_fw_table[l], lb_bw_table[l], hg_norm_gain[l])
        o_da = dilated_mixer(da_q, da_k, da_v, da_q_gain[l], da_k_gain[l])
        mem_n = rmsnorm(mem, norm_mem_gain[l])
        o_mem = memory_mixer(mem_q, mem_n, w_mem_kv[l], mem_q_gain[l], mem_k_gain[l])

        merged = (jax.nn.sigmoid(gate_hg) * (o_hg @ w_proj_hg[l])
                  + jax.nn.sigmoid(gate_da) * (o_da @ w_proj_da[l])
                  + jax.nn.sigmoid(gate_mem) * (o_mem @ w_proj_mem[l]))
        x = x + merged @ w_out[l]

        h = rmsnorm(x, norm_ffn_gain[l])
        a, b = jnp.split(h @ w_ffn_in[l], 2, axis=-1)
        x = x + (jax.nn.silu(a) * b) @ w_ffn_out[l]
    return x
```

```python
import functools
import math

import jax
import jax.numpy as jnp
import numpy as np
from jax import lax
from jax.experimental import pallas as pl
from jax.experimental.pallas import tpu as pltpu

RMS_EPS = 1e-6
NEG_INF = -1e30
HEAD_DIM = 128
HG_CHUNK = 64
DA_CONFIGS = ((128, 1), (512, 4), (2048, 16))
DA_RADIUS = 64
DA_HEADS_PER_GROUP = 4
DA_HEADS = DA_HEADS_PER_GROUP * len(DA_CONFIGS)
MEM_HEADS = 4
GROUP_WIDTH = DA_HEADS_PER_GROUP * HEAD_DIM
ATTN_SCALE = HEAD_DIM ** -0.5

BF16 = jnp.bfloat16
F32 = jnp.float32

VMEM_LIMIT_BYTES = 56 * 1024 * 1024

IN_TILE_N = 512


def _alibi_slopes(n):
    return [float(2.0 ** (-8.0 * (i + 1) / n)) for i in range(n)]


def _sigmoid(x):
    return 1.0 / (1.0 + jnp.exp(-x))


def _silu(x):
    return x * _sigmoid(x)


def _rms_rows(x, gain):
    ms = jnp.mean(x * x, axis=-1, keepdims=True)
    return x * lax.rsqrt(ms + RMS_EPS) * gain


def _head_norm(x, gain, scale):
    parts = []
    for h in range(x.shape[-1] // HEAD_DIM):
        parts.append(_rms_rows(x[:, h * HEAD_DIM:(h + 1) * HEAD_DIM], gain) * scale)
    return jnp.concatenate(parts, axis=-1)


def _dot(a, b):
    return jnp.dot(a, b, preferred_element_type=F32)


def _dot_nt(a, b):
    return lax.dot_general(a, b, (((1,), (1,)), ((), ())), preferred_element_type=F32)


def _dot_tn(a, b):
    return lax.dot_general(a, b, (((0,), (0,)), ((), ())), preferred_element_type=F32)


def _mem_kv_kernel(mem_ref, gain_ref, w_ref, kgain_ref, k_ref, v_ref):
    mem_n = _rms_rows(mem_ref[0], gain_ref[...]).astype(BF16)
    kv = _dot(mem_n, w_ref[...])
    width = k_ref.shape[-1]
    k_ref[0] = _head_norm(kv[:, :width], kgain_ref[...], 1.0).astype(BF16)
    v_ref[0] = kv[:, width:].astype(BF16)


def _mem_kv(mem, gain, w_kv, k_gain):
    B, M, D = mem.shape
    width = w_kv.shape[1] // 2
    return pl.pallas_call(
        _mem_kv_kernel,
        grid=(B,),
        in_specs=[
            pl.BlockSpec((1, M, D), lambda b: (b, 0, 0)),
            pl.BlockSpec((1, D), lambda b: (0, 0)),
            pl.BlockSpec((D, 2 * width), lambda b: (0, 0)),
            pl.BlockSpec((1, HEAD_DIM), lambda b: (0, 0)),
        ],
        out_specs=[
            pl.BlockSpec((1, M, width), lambda b: (b, 0, 0)),
            pl.BlockSpec((1, M, width), lambda b: (b, 0, 0)),
        ],
        out_shape=[jax.ShapeDtypeStruct((B, M, width), BF16)] * 2,
        compiler_params=pltpu.CompilerParams(vmem_limit_bytes=VMEM_LIMIT_BYTES),
        name="mem_kv",
    )(mem, gain, w_kv, k_gain)


_SEC_HQ, _SEC_FFW, _SEC_FBW, _SEC_HV, _SEC_HG = (0, 2), (2, 2), (4, 2), (6, 2), (8, 2)
_SEC_DAQ, _SEC_DAK, _SEC_DAV = (10, 3), (13, 3), (16, 3)
_SEC_MEMQ, _SEC_GATES = (19, 1), (20, 6)
_IN_TILES = 26


def _in_sec(j, sec):
    return (j >= sec[0]) & (j < sec[0] + sec[1])


def _lower_bound(logits):
    e = jnp.exp(logits - jnp.max(logits, axis=0, keepdims=True))
    return e[0:1] / jnp.sum(e, axis=0, keepdims=True)


def _in_proj_kernel(x_ref, gain_ref, w_ref, lbfw_ref, lbbw_ref, qg_ref, kg_ref, mg_ref,
                    hq_ref, lffw_ref, kfw_ref, lfbw_ref, kbw_ref, hv_ref, hg_ref,
                    q0_ref, q1_ref, q2_ref, k0_ref, k1_ref, k2_ref, v0_ref, v1_ref, v2_ref,
                    mq_ref, gates_ref, h_ref, acc_ref):
    j = pl.program_id(1)

    @pl.when(j == 0)
    def _():
        h_ref[...] = _rms_rows(x_ref[...], gain_ref[...]).astype(BF16)

    acc = _dot(h_ref[...], w_ref[...])
    nh = acc_ref.shape[0]
    for h in range(nh):
        acc_ref[h] = acc[:, h * HEAD_DIM:(h + 1) * HEAD_DIM]

    def per_head(out_ref, fn):
        for h in range(nh):
            out_ref[:, h * HEAD_DIM:(h + 1) * HEAD_DIM] = fn(acc_ref[h]).astype(out_ref.dtype)

    @pl.when(_in_sec(j, _SEC_HQ))
    def _():
        per_head(hq_ref, lambda t: _silu(t) * ATTN_SCALE)

    def forget(lb_ref, lf_ref, k_ref):
        lb = _lower_bound(lb_ref[...])
        for h in range(nh):
            sl = slice(h * HEAD_DIM, (h + 1) * HEAD_DIM)
            f = lb[:, sl] + (1.0 - lb[:, sl]) * _sigmoid(acc_ref[h])
            lf_ref[:, sl] = jnp.log(f)
            k_ref[:, sl] = (1.0 - f).astype(BF16)

    @pl.when(_in_sec(j, _SEC_FFW))
    def _():
        forget(lbfw_ref, lffw_ref, kfw_ref)

    @pl.when(_in_sec(j, _SEC_FBW))
    def _():
        forget(lbbw_ref, lfbw_ref, kbw_ref)

    @pl.when(_in_sec(j, _SEC_HV))
    def _():
        per_head(hv_ref, lambda t: t)

    @pl.when(_in_sec(j, _SEC_HG))
    def _():
        per_head(hg_ref, _silu)

    def class_major(out_ref, fn):
        d = out_ref.shape[1]
        rows = out_ref.shape[2]
        for r in range(d):
            for h in range(nh):
                vals = acc_ref[h, pl.ds(r, rows, stride=d), :] if d > 1 else acc_ref[h]
                out_ref[0, r, :, h * HEAD_DIM:(h + 1) * HEAD_DIM] = fn(vals).astype(BF16)

    for g, (qr, kr, vr) in enumerate(((q0_ref, k0_ref, v0_ref), (q1_ref, k1_ref, v1_ref),
                                      (q2_ref, k2_ref, v2_ref))):
        @pl.when(j == _SEC_DAQ[0] + g)
        def _(qr=qr):
            class_major(qr, lambda t: _rms_rows(t, qg_ref[...]) * ATTN_SCALE)

        @pl.when(j == _SEC_DAK[0] + g)
        def _(kr=kr):
            class_major(kr, lambda t: _rms_rows(t, kg_ref[...]))

        @pl.when(j == _SEC_DAV[0] + g)
        def _(vr=vr):
            class_major(vr, lambda t: t)

    @pl.when(_in_sec(j, _SEC_MEMQ))
    def _():
        per_head(mq_ref, lambda t: _rms_rows(t, mg_ref[...]) * ATTN_SCALE)

    @pl.when(_in_sec(j, _SEC_GATES))
    def _():
        per_head(gates_ref, _sigmoid)


def _in_proj(x2, gain, w_in, lb_fw, lb_bw, q_gain, k_gain, mq_gain, B, L):
    T, D = x2.shape
    tm = 512
    nt = L // tm
    tn = IN_TILE_N

    def sec_map(sec):
        return lambda i, j: (i, jnp.clip(j - sec[0], 0, sec[1] - 1))

    def lb_map(sec):
        return lambda i, j: (0, jnp.clip(j - sec[0], 0, sec[1] - 1))

    def tok_spec(sec):
        return pl.BlockSpec((tm, tn), sec_map(sec))

    def cls_spec(d):
        return pl.BlockSpec((1, d, tm // d, tn), lambda i, j: (i // nt, 0, i % nt, 0))

    def tok_shape(sec, dtype):
        return jax.ShapeDtypeStruct((T, sec[1] * tn), dtype)

    def cls_shape(d):
        return jax.ShapeDtypeStruct((B, d, L // d, tn), BF16)

    dils = [d for _, d in DA_CONFIGS]
    out_specs = ([tok_spec(_SEC_HQ), tok_spec(_SEC_FFW), tok_spec(_SEC_FFW), tok_spec(_SEC_FBW),
                  tok_spec(_SEC_FBW), tok_spec(_SEC_HV), tok_spec(_SEC_HG)]
                 + [cls_spec(d) for d in dils] * 3
                 + [tok_spec(_SEC_MEMQ), tok_spec(_SEC_GATES)])
    out_shape = ([tok_shape(_SEC_HQ, BF16), tok_shape(_SEC_FFW, F32), tok_shape(_SEC_FFW, BF16),
                  tok_shape(_SEC_FBW, F32), tok_shape(_SEC_FBW, BF16), tok_shape(_SEC_HV, BF16),
                  tok_shape(_SEC_HG, BF16)]
                 + [cls_shape(d) for d in dils] * 3
                 + [tok_shape(_SEC_MEMQ, BF16), tok_shape(_SEC_GATES, BF16)])
    return pl.pallas_call(
        _in_proj_kernel,
        grid=(T // tm, _IN_TILES),
        in_specs=[
            pl.BlockSpec((tm, D), lambda i, j: (i, 0)),
            pl.BlockSpec((1, D), lambda i, j: (0, 0)),
            pl.BlockSpec((D, tn), lambda i, j: (0, j)),
            pl.BlockSpec((lb_fw.shape[0], tn), lb_map(_SEC_FFW)),
            pl.BlockSpec((lb_bw.shape[0], tn), lb_map(_SEC_FBW)),
            pl.BlockSpec((1, HEAD_DIM), lambda i, j: (0, 0)),
            pl.BlockSpec((1, HEAD_DIM), lambda i, j: (0, 0)),
            pl.BlockSpec((1, HEAD_DIM), lambda i, j: (0, 0)),
        ],
        out_specs=out_specs,
        out_shape=out_shape,
        scratch_shapes=[pltpu.VMEM((tm, D), BF16),
                        pltpu.VMEM((tn // HEAD_DIM, tm, HEAD_DIM), F32)],
        compiler_params=pltpu.CompilerParams(
            dimension_semantics=("arbitrary", "arbitrary"),
            vmem_limit_bytes=VMEM_LIMIT_BYTES),
        name="in_proj",
    )(x2, gain, w_in, lb_fw, lb_bw, q_gain, k_gain, mq_gain)


def _hgrn_kernel(q_ref, lffw_ref, kfw_ref, lfbw_ref, kbw_ref, v_ref, g_ref, gain_ref,
                 o_ref, ofw_ref, obw_ref, sfw_ref, sbw_ref):
    L = q_ref.shape[1]
    C = HG_CHUNK
    n_chunks = L // C
    row = lax.broadcasted_iota(jnp.int32, (C, C), 0)
    col = lax.broadcasted_iota(jnp.int32, (C, C), 1)
    lower = row >= col
    tri_fw = jnp.where(lower, 1.0, 0.0).astype(BF16)
    tri_bw = jnp.where(row <= col, 1.0, 0.0).astype(BF16)

    sfw_ref[...] = jnp.zeros_like(sfw_ref)
    sbw_ref[...] = jnp.zeros_like(sbw_ref)

    def chunk(c, lf_ref, k_ref, s_ref, out_ref, tri, causal, end_row):
        rows = pl.ds(pl.multiple_of(c * C, C), C)
        lf = lf_ref[0, rows, :]
        lf_hi = lf.astype(BF16)
        lf_lo = (lf - lf_hi.astype(F32)).astype(BF16)
        b = _dot(tri, lf_hi) + _dot(tri, lf_lo)
        b_end = b[end_row:end_row + 1, :]
        e_neg = jnp.exp(-b)
        k = k_ref[0, rows, :].astype(F32)
        q_in = (q_ref[0, rows, :].astype(F32) * jnp.exp(b)).astype(BF16)
        k_in = k * e_neg
        d_end = jnp.exp(b_end)
        k_st = (k_in * d_end).astype(BF16)
        v = v_ref[0, rows, :]
        a = _dot_nt(q_in, k_in.astype(BF16))
        a = jnp.where(causal, a, 0.0).astype(BF16)
        s_t = s_ref[...]
        out_ref[rows, :] = _dot(a, v) + _dot_nt(q_in, s_t.astype(BF16))
        s_ref[...] = s_t * d_end + _dot_tn(v, k_st)

    def body(n, carry):
        chunk(n, lffw_ref, kfw_ref, sfw_ref, ofw_ref, tri_fw, lower, C - 1)
        chunk(n_chunks - 1 - n, lfbw_ref, kbw_ref, sbw_ref, obw_ref, tri_bw, row <= col, 0)
        return carry

    lax.fori_loop(0, n_chunks, body, 0)

    o = ofw_ref[...] + obw_ref[...]
    o_ref[0] = (_rms_rows(o, gain_ref[...]) * g_ref[0].astype(F32)).astype(BF16)


def _hgrn(hq, lf_fw, k_fw, lf_bw, k_bw, hv, hg, gain):
    B, L, W = hq.shape
    H = W // HEAD_DIM
    spec = pl.BlockSpec((1, L, HEAD_DIM), lambda b, h: (b, 0, h))
    return pl.pallas_call(
        _hgrn_kernel,
        grid=(B, H),
        in_specs=[spec] * 7 + [pl.BlockSpec((1, HEAD_DIM), lambda b, h: (0, 0))],
        out_specs=spec,
        out_shape=jax.ShapeDtypeStruct((B, L, W), BF16),
        scratch_shapes=[pltpu.VMEM((L, HEAD_DIM), F32), pltpu.VMEM((L, HEAD_DIM), F32),
                        pltpu.VMEM((HEAD_DIM, HEAD_DIM), F32), pltpu.VMEM((HEAD_DIM, HEAD_DIM), F32)],
        compiler_params=pltpu.CompilerParams(
            dimension_semantics=("arbitrary", "arbitrary"),
            vmem_limit_bytes=VMEM_LIMIT_BYTES),
        name="hgrn2",
    )(hq, lf_fw, k_fw, lf_bw, k_bw, hv, hg, gain)


ATTN_BLOCK_Q = 128
ATTN_BLOCK_K = ATTN_BLOCK_Q + 2 * DA_RADIUS


def _dilated_kernel(q_ref, k_ref, v_ref, o_ref, lse_ref, *, dilation, slopes):
    Ld = q_ref.shape[2]
    h = pl.program_id(2)
    slope = jnp.float32(slopes[0])
    for i in range(1, len(slopes)):
        slope = jnp.where(h == i, jnp.float32(slopes[i]), slope)
    slope = slope * float(dilation)
    bq, bk = ATTN_BLOCK_Q, ATTN_BLOCK_K
    row = lax.broadcasted_iota(jnp.int32, (bq, bk), 0)
    col = lax.broadcasted_iota(jnp.int32, (bq, bk), 1)
    lane = lax.broadcasted_iota(jnp.int32, (bq, HEAD_DIM), 1)

    def body(i, carry):
        q0 = pl.multiple_of(i * bq, bq)
        k0 = pl.multiple_of(jnp.clip(q0 - DA_RADIUS, 0, Ld - bk), DA_RADIUS)
        q = q_ref[0, 0, pl.ds(q0, bq), :]
        k = k_ref[0, 0, pl.ds(k0, bk), :]
        v = v_ref[0, 0, pl.ds(k0, bk), :]
        dist = jnp.abs(col - row + (k0 - q0))
        s = _dot_nt(q, k) - slope * dist.astype(F32)
        s = jnp.where(dist <= DA_RADIUS, s, NEG_INF)
        m = jnp.max(s, axis=-1, keepdims=True)
        p = jnp.exp(s - m)
        l = jnp.sum(p, axis=-1, keepdims=True)
        o = _dot(p.astype(BF16), v) / l
        o_ref[0, pl.ds(q0, bq), :] = o.astype(BF16)
        lse = m + jnp.log(l)

        @pl.when(h == 0)
        def _():
            lse_ref[0, pl.ds(q0, bq), :] = jnp.where(lane == 0, lse, 0.0)

        @pl.when(h != 0)
        def _():
            cur = lse_ref[0, pl.ds(q0, bq), :]
            lse_ref[0, pl.ds(q0, bq), :] = jnp.where(lane == h, lse, cur)

        return carry

    lax.fori_loop(0, Ld // bq, body, 0)


def _dilated_group(q, k, v, dilation, slopes):
    B, d, Ld, W = q.shape
    nh = W // HEAD_DIM
    L = d * Ld
    in_spec = pl.BlockSpec((1, 1, Ld, HEAD_DIM), lambda b, r, h: (b, r, 0, h))
    o, lse = pl.pallas_call(
        functools.partial(_dilated_kernel, dilation=dilation, slopes=tuple(slopes)),
        grid=(B, d, nh),
        in_specs=[in_spec] * 3,
        out_specs=[
            pl.BlockSpec((1, Ld, HEAD_DIM), lambda b, r, h: (b, 0, r * nh + h)),
            pl.BlockSpec((1, Ld, HEAD_DIM), lambda b, r, h: (b, 0, r)),
        ],
        out_shape=[jax.ShapeDtypeStruct((B, Ld, d * W), BF16),
                   jax.ShapeDtypeStruct((B, Ld, d * HEAD_DIM), F32)],
        compiler_params=pltpu.CompilerParams(
            dimension_semantics=("arbitrary", "arbitrary", "arbitrary"),
            vmem_limit_bytes=VMEM_LIMIT_BYTES),
        name=f"dilated_d{dilation}",
    )(q, k, v)
    return o.reshape(B, L, W), lse.reshape(B, L, HEAD_DIM)


def _mix_kernel(x_ref, ohg_ref, o0_ref, o1_ref, o2_ref, l0_ref, l1_ref, l2_ref, mq_ref,
                mk_ref, mv_ref, gates_ref, whg_ref, wda_ref, wmem_ref, wout_ref, out_ref):
    D = x_ref.shape[-1]
    mem_parts = []
    for h in range(MEM_HEADS):
        sl = slice(h * HEAD_DIM, (h + 1) * HEAD_DIM)
        s = _dot_nt(mq_ref[:, sl], mk_ref[0, :, sl])
        p = jnp.exp(s - jnp.max(s, axis=-1, keepdims=True))
        l = jnp.sum(p, axis=-1, keepdims=True)
        mem_parts.append((_dot(p.astype(BF16), mv_ref[0, :, sl]) / l).astype(BF16))
    o_mem = jnp.concatenate(mem_parts, axis=-1)

    da_parts = []
    for h in range(DA_HEADS_PER_GROUP):
        sl = slice(h * HEAD_DIM, (h + 1) * HEAD_DIM)
        ls = [r[:, h:h + 1] for r in (l0_ref, l1_ref, l2_ref)]
        m = jnp.maximum(jnp.maximum(ls[0], ls[1]), ls[2])
        es = [jnp.exp(t - m) for t in ls]
        inv = 1.0 / (es[0] + es[1] + es[2])
        acc = None
        for e, o_ref in zip(es, (o0_ref, o1_ref, o2_ref)):
            term = (e * inv) * o_ref[:, sl].astype(F32)
            acc = term if acc is None else acc + term
        da_parts.append(acc.astype(BF16))
    o_da = jnp.concatenate(da_parts, axis=-1)

    merged = (gates_ref[:, 0:D].astype(F32) * _dot(ohg_ref[...], whg_ref[...])
              + gates_ref[:, D:2 * D].astype(F32) * _dot(o_da, wda_ref[...])
              + gates_ref[:, 2 * D:3 * D].astype(F32) * _dot(o_mem, wmem_ref[...]))
    out_ref[...] = x_ref[...] + _dot(merged.astype(BF16), wout_ref[...])


def _mix(x2, o_hg, o_das, lses, mem_q, mem_k, mem_v, gates, w_hg, w_da, w_mem, w_out, L):
    T, D = x2.shape
    tm = 512
    nt = L // tm
    M = mem_k.shape[1]

    def tok(width):
        return pl.BlockSpec((tm, width), lambda i: (i, 0))

    def full(w):
        return pl.BlockSpec(w.shape, lambda i: (0, 0))

    mem_spec = pl.BlockSpec((1, M, mem_k.shape[2]), lambda i: (i // nt, 0, 0))
    return pl.pallas_call(
        _mix_kernel,
        grid=(T // tm,),
        in_specs=[tok(D), tok(o_hg.shape[1])] + [tok(GROUP_WIDTH)] * 3 + [tok(HEAD_DIM)] * 3
                 + [tok(mem_q.shape[1]), mem_spec, mem_spec, tok(gates.shape[1]),
                    full(w_hg), full(w_da), full(w_mem), full(w_out)],
        out_specs=tok(D),
        out_shape=jax.ShapeDtypeStruct((T, D), F32),
        compiler_params=pltpu.CompilerParams(
            dimension_semantics=("arbitrary",),
            vmem_limit_bytes=VMEM_LIMIT_BYTES),
        name="mix",
    )(x2, o_hg, *o_das, *lses, mem_q, mem_k, mem_v, gates, w_hg, w_da, w_mem, w_out)


def _ffn_kernel(x_ref, gain_ref, wa_ref, wb_ref, wo_ref, out_ref, h_ref, acc_ref):
    j = pl.program_id(1)

    @pl.when(j == 0)
    def _():
        h_ref[...] = _rms_rows(x_ref[...], gain_ref[...]).astype(BF16)
        acc_ref[...] = jnp.zeros_like(acc_ref)

    h = h_ref[...]
    a = _dot(h, wa_ref[...])
    b = _dot(h, wb_ref[...])
    acc_ref[...] += _dot((_silu(a) * b).astype(BF16), wo_ref[...])

    @pl.when(j == pl.num_programs(1) - 1)
    def _():
        out_ref[...] = x_ref[...] + acc_ref[...]


def _ffn(x2, gain, w_in, w_out):
    T, D = x2.shape
    d_ff = w_out.shape[0]
    tm = 1024
    tf = 256
    nf = d_ff // tf
    return pl.pallas_call(
        _ffn_kernel,
        grid=(T // tm, nf),
        in_specs=[
            pl.BlockSpec((tm, D), lambda i, j: (i, 0)),
            pl.BlockSpec((1, D), lambda i, j: (0, 0)),
            pl.BlockSpec((D, tf), lambda i, j: (0, j)),
            pl.BlockSpec((D, tf), lambda i, j: (0, nf + j)),
            pl.BlockSpec((tf, D), lambda i, j: (j, 0)),
        ],
        out_specs=pl.BlockSpec((tm, D), lambda i, j: (i, 0)),
        out_shape=jax.ShapeDtypeStruct((T, D), F32),
        scratch_shapes=[pltpu.VMEM((tm, D), BF16), pltpu.VMEM((tm, D), F32)],
        compiler_params=pltpu.CompilerParams(
            dimension_semantics=("arbitrary", "arbitrary"),
            vmem_limit_bytes=VMEM_LIMIT_BYTES),
        name="ffn",
    )(x2, gain, w_in, w_in, w_out)


def kernel(x, mem, norm_mix_gain, norm_mem_gain, w_in, lb_logits_fw, lb_logits_bw, hg_norm_gain,
           da_q_gain, da_k_gain, w_mem_kv, mem_q_gain, mem_k_gain, w_proj_hg, w_proj_da,
           w_proj_mem, w_out, norm_ffn_gain, w_ffn_in, w_ffn_out):
    B, L, D = x.shape
    depth = w_in.shape[0]
    assert depth == 1, "lower-bound tables are evaluated for a single layer"
    T = B * L
    slopes = _alibi_slopes(DA_HEADS)
    x2 = x.reshape(T, D)
    for l in range(depth):
        mem_k, mem_v = _mem_kv(mem, norm_mem_gain[l][None], w_mem_kv[l].astype(BF16),
                               mem_k_gain[l][None])
        (hq, lf_fw, k_fw, lf_bw, k_bw, hv, hg, q0, q1, q2, k0, k1, k2, v0, v1, v2,
         mem_q, gates) = _in_proj(
            x2, norm_mix_gain[l][None], w_in[l].astype(BF16), lb_logits_fw.astype(F32),
            lb_logits_bw.astype(F32), da_q_gain[l][None], da_k_gain[l][None],
            mem_q_gain[l][None], B, L)

        def seq(t):
            return t.reshape(B, L, t.shape[-1])

        o_hg = _hgrn(seq(hq), seq(lf_fw), seq(k_fw), seq(lf_bw), seq(k_bw), seq(hv), seq(hg),
                     hg_norm_gain[l][None])

        o_das, lses = [], []
        for g, ((_, dilation), q, k, v) in enumerate(zip(DA_CONFIGS, (q0, q1, q2), (k0, k1, k2),
                                                         (v0, v1, v2))):
            lo = g * DA_HEADS_PER_GROUP
            o, lse = _dilated_group(q, k, v, dilation, slopes[lo:lo + DA_HEADS_PER_GROUP])
            o_das.append(o.reshape(T, GROUP_WIDTH))
            lses.append(lse.reshape(T, HEAD_DIM))

        x2 = _mix(x2, o_hg.reshape(T, -1), o_das, lses, mem_q, mem_k, mem_v, gates,
                  w_proj_hg[l].astype(BF16), w_proj_da[l].astype(BF16),
                  w_proj_mem[l].astype(BF16), w_out[l].astype(BF16), L)
        x2 = _ffn(x2, norm_ffn_gain[l][None], w_ffn_in[l].astype(BF16), w_ffn_out[l].astype(BF16))
    return x2.reshape(B, L, D)
```

```python
import functools
import math

import jax
import jax.numpy as jnp
import numpy as np
from jax import lax
from jax.experimental import pallas as pl
from jax.experimental.pallas import tpu as pltpu

RMS_EPS = 1e-6
NEG_INF = -1e30
HEAD_DIM = 128
HG_CHUNK = 128
HG_UNROLL = 4
DA_CONFIGS = ((128, 1), (512, 4), (2048, 16))
DA_RADIUS = 64
DA_HEADS_PER_GROUP = 4
DA_HEADS = DA_HEADS_PER_GROUP * len(DA_CONFIGS)
MEM_HEADS = 4
GROUP_WIDTH = DA_HEADS_PER_GROUP * HEAD_DIM
ATTN_SCALE = HEAD_DIM ** -0.5

BF16 = jnp.bfloat16
F32 = jnp.float32

VMEM_LIMIT_BYTES = 56 * 1024 * 1024

IN_TILE_N = 512


def _alibi_slopes(n):
    return [float(2.0 ** (-8.0 * (i + 1) / n)) for i in range(n)]


def _sigmoid(x):
    return 1.0 / (1.0 + jnp.exp(-x))


def _silu(x):
    return x * _sigmoid(x)


def _rms_rows(x, gain):
    ms = jnp.mean(x * x, axis=-1, keepdims=True)
    return x * lax.rsqrt(ms + RMS_EPS) * gain


def _head_norm(x, gain, scale):
    parts = []
    for h in range(x.shape[-1] // HEAD_DIM):
        parts.append(_rms_rows(x[:, h * HEAD_DIM:(h + 1) * HEAD_DIM], gain) * scale)
    return jnp.concatenate(parts, axis=-1)


def _dot(a, b):
    return jnp.dot(a, b, preferred_element_type=F32)


def _dot_nt(a, b):
    return lax.dot_general(a, b, (((1,), (1,)), ((), ())), preferred_element_type=F32)


def _dot_tn(a, b):
    return lax.dot_general(a, b, (((0,), (0,)), ((), ())), preferred_element_type=F32)


def _mem_kv_kernel(mem_ref, gain_ref, w_ref, kgain_ref, k_ref, v_ref):
    mem_n = _rms_rows(mem_ref[0], gain_ref[...]).astype(BF16)
    kv = _dot(mem_n, w_ref[...])
    width = k_ref.shape[-1]
    k_ref[0] = _head_norm(kv[:, :width], kgain_ref[...], 1.0).astype(BF16)
    v_ref[0] = kv[:, width:].astype(BF16)


def _mem_kv(mem, gain, w_kv, k_gain):
    B, M, D = mem.shape
    width = w_kv.shape[1] // 2
    return pl.pallas_call(
        _mem_kv_kernel,
        grid=(B,),
        in_specs=[
            pl.BlockSpec((1, M, D), lambda b: (b, 0, 0)),
            pl.BlockSpec((1, D), lambda b: (0, 0)),
            pl.BlockSpec((D, 2 * width), lambda b: (0, 0)),
            pl.BlockSpec((1, HEAD_DIM), lambda b: (0, 0)),
        ],
        out_specs=[
            pl.BlockSpec((1, M, width), lambda b: (b, 0, 0)),
            pl.BlockSpec((1, M, width), lambda b: (b, 0, 0)),
        ],
        out_shape=[jax.ShapeDtypeStruct((B, M, width), BF16)] * 2,
        compiler_params=pltpu.CompilerParams(vmem_limit_bytes=VMEM_LIMIT_BYTES),
        name="mem_kv",
    )(mem, gain, w_kv, k_gain)


_SEC_HQ, _SEC_FFW, _SEC_FBW, _SEC_HV, _SEC_HG = (0, 2), (2, 2), (4, 2), (6, 2), (8, 2)
_SEC_DAQ, _SEC_DAK, _SEC_DAV = (10, 3), (13, 3), (16, 3)
_SEC_MEMQ, _SEC_GATES = (19, 1), (20, 6)
_IN_TILES = 26


def _in_sec(j, sec):
    return (j >= sec[0]) & (j < sec[0] + sec[1])


def _lower_bound(logits):
    e = jnp.exp(logits - jnp.max(logits, axis=0, keepdims=True))
    return e[0:1] / jnp.sum(e, axis=0, keepdims=True)


def _in_proj_kernel(x_ref, gain_ref, w_ref, lbfw_ref, lbbw_ref, qg_ref, kg_ref, mg_ref,
                    hq_ref, lffw_ref, kfw_ref, lfbw_ref, kbw_ref, hv_ref, hg_ref,
                    q0_ref, q1_ref, q2_ref, k0_ref, k1_ref, k2_ref, v0_ref, v1_ref, v2_ref,
                    mq_ref, gates_ref, h_ref, acc_ref):
    j = pl.program_id(1)

    @pl.when(j == 0)
    def _():
        h_ref[...] = _rms_rows(x_ref[...], gain_ref[...]).astype(BF16)

    acc = _dot(h_ref[...], w_ref[...])
    nh = acc_ref.shape[0]
    for h in range(nh):
        acc_ref[h] = acc[:, h * HEAD_DIM:(h + 1) * HEAD_DIM]

    def per_head(out_ref, fn):
        for h in range(nh):
            out_ref[:, h * HEAD_DIM:(h + 1) * HEAD_DIM] = fn(acc_ref[h]).astype(out_ref.dtype)

    @pl.when(_in_sec(j, _SEC_HQ))
    def _():
        per_head(hq_ref, lambda t: _silu(t) * ATTN_SCALE)

    def forget(lb_ref, lf_ref, k_ref):
        lb = _lower_bound(lb_ref[...])
        for h in range(nh):
            sl = slice(h * HEAD_DIM, (h + 1) * HEAD_DIM)
            f = lb[:, sl] + (1.0 - lb[:, sl]) * _sigmoid(acc_ref[h])
            lf_ref[:, sl] = jnp.log(f)
            k_ref[:, sl] = (1.0 - f).astype(BF16)

    @pl.when(_in_sec(j, _SEC_FFW))
    def _():
        forget(lbfw_ref, lffw_ref, kfw_ref)

    @pl.when(_in_sec(j, _SEC_FBW))
    def _():
        forget(lbbw_ref, lfbw_ref, kbw_ref)

    @pl.when(_in_sec(j, _SEC_HV))
    def _():
        per_head(hv_ref, lambda t: t)

    @pl.when(_in_sec(j, _SEC_HG))
    def _():
        per_head(hg_ref, _silu)

    def class_major(out_ref, fn):
        d = out_ref.shape[1]
        rows = out_ref.shape[2]
        for r in range(d):
            for h in range(nh):
                vals = acc_ref[h, pl.ds(r, rows, stride=d), :] if d > 1 else acc_ref[h]
                out_ref[0, r, :, h * HEAD_DIM:(h + 1) * HEAD_DIM] = fn(vals).astype(BF16)

    for g, (qr, kr, vr) in enumerate(((q0_ref, k0_ref, v0_ref), (q1_ref, k1_ref, v1_ref),
                                      (q2_ref, k2_ref, v2_ref))):
        @pl.when(j == _SEC_DAQ[0] + g)
        def _(qr=qr):
            class_major(qr, lambda t: _rms_rows(t, qg_ref[...]) * ATTN_SCALE)

        @pl.when(j == _SEC_DAK[0] + g)
        def _(kr=kr):
            class_major(kr, lambda t: _rms_rows(t, kg_ref[...]))

        @pl.when(j == _SEC_DAV[0] + g)
        def _(vr=vr):
            class_major(vr, lambda t: t)

    @pl.when(_in_sec(j, _SEC_MEMQ))
    def _():
        per_head(mq_ref, lambda t: _rms_rows(t, mg_ref[...]) * ATTN_SCALE)

    @pl.when(_in_sec(j, _SEC_GATES))
    def _():
        per_head(gates_ref, _sigmoid)


def _in_proj(x2, gain, w_in, lb_fw, lb_bw, q_gain, k_gain, mq_gain, B, L):
    T, D = x2.shape
    tm = 512
    nt = L // tm
    tn = IN_TILE_N

    def sec_map(sec):
        return lambda i, j: (i, jnp.clip(j - sec[0], 0, sec[1] - 1))

    def lb_map(sec):
        return lambda i, j: (0, jnp.clip(j - sec[0], 0, sec[1] - 1))

    def tok_spec(sec):
        return pl.BlockSpec((tm, tn), sec_map(sec))

    def cls_spec(d):
        return pl.BlockSpec((1, d, tm // d, tn), lambda i, j: (i // nt, 0, i % nt, 0))

    def tok_shape(sec, dtype):
        return jax.ShapeDtypeStruct((T, sec[1] * tn), dtype)

    def cls_shape(d):
        return jax.ShapeDtypeStruct((B, d, L // d, tn), BF16)

    dils = [d for _, d in DA_CONFIGS]
    out_specs = ([tok_spec(_SEC_HQ), tok_spec(_SEC_FFW), tok_spec(_SEC_FFW), tok_spec(_SEC_FBW),
                  tok_spec(_SEC_FBW), tok_spec(_SEC_HV), tok_spec(_SEC_HG)]
                 + [cls_spec(d) for d in dils] * 3
                 + [tok_spec(_SEC_MEMQ), tok_spec(_SEC_GATES)])
    out_shape = ([tok_shape(_SEC_HQ, BF16), tok_shape(_SEC_FFW, F32), tok_shape(_SEC_FFW, BF16),
                  tok_shape(_SEC_FBW, F32), tok_shape(_SEC_FBW, BF16), tok_shape(_SEC_HV, BF16),
                  tok_shape(_SEC_HG, BF16)]
                 + [cls_shape(d) for d in dils] * 3
                 + [tok_shape(_SEC_MEMQ, BF16), tok_shape(_SEC_GATES, BF16)])
    return pl.pallas_call(
        _in_proj_kernel,
        grid=(T // tm, _IN_TILES),
        in_specs=[
            pl.BlockSpec((tm, D), lambda i, j: (i, 0)),
            pl.BlockSpec((1, D), lambda i, j: (0, 0)),
            pl.BlockSpec((D, tn), lambda i, j: (0, j)),
            pl.BlockSpec((lb_fw.shape[0], tn), lb_map(_SEC_FFW)),
            pl.BlockSpec((lb_bw.shape[0], tn), lb_map(_SEC_FBW)),
            pl.BlockSpec((1, HEAD_DIM), lambda i, j: (0, 0)),
            pl.BlockSpec((1, HEAD_DIM), lambda i, j: (0, 0)),
            pl.BlockSpec((1, HEAD_DIM), lambda i, j: (0, 0)),
        ],
        out_specs=out_specs,
        out_shape=out_shape,
        scratch_shapes=[pltpu.VMEM((tm, D), BF16),
                        pltpu.VMEM((tn // HEAD_DIM, tm, HEAD_DIM), F32)],
        compiler_params=pltpu.CompilerParams(
            dimension_semantics=("arbitrary", "arbitrary"),
            vmem_limit_bytes=VMEM_LIMIT_BYTES),
        name="in_proj",
    )(x2, gain, w_in, lb_fw, lb_bw, q_gain, k_gain, mq_gain)


def _hgrn_kernel(q_ref, lffw_ref, kfw_ref, lfbw_ref, kbw_ref, v_ref, g_ref, gain_ref,
                 o_ref, ofw_ref, obw_ref, sfw_ref, sbw_ref):
    L = q_ref.shape[1]
    C = HG_CHUNK
    n_chunks = L // C
    row = lax.broadcasted_iota(jnp.int32, (C, C), 0)
    col = lax.broadcasted_iota(jnp.int32, (C, C), 1)
    lower = row >= col
    upper = row <= col
    tri_fw = jnp.where(lower, 1.0, 0.0).astype(BF16)
    tri_bw = jnp.where(upper, 1.0, 0.0).astype(BF16)

    sfw_ref[...] = jnp.zeros_like(sfw_ref)
    sbw_ref[...] = jnp.zeros_like(sbw_ref)

    def chunk(c, lf_ref, k_ref, s, out_ref, tri, causal, mid_row, end_row):
        rows = pl.ds(pl.multiple_of(c * C, C), C)
        lf = lf_ref[0, rows, :]
        lf_hi = lf.astype(BF16)
        lf_lo = (lf - lf_hi.astype(F32)).astype(BF16)
        bb = _dot(tri, jnp.concatenate([lf_hi, lf_lo], axis=1))
        b = bb[:, :HEAD_DIM] + bb[:, HEAD_DIM:]
        b_mid = b[mid_row:mid_row + 1, :]
        b_end = b[end_row:end_row + 1, :]
        q_in = q_ref[0, rows, :].astype(F32) * jnp.exp(b - b_mid)
        k_in = k_ref[0, rows, :].astype(F32) * jnp.exp(b_mid - b)
        q_st = (q_in * jnp.exp(b_mid)).astype(BF16)
        k_st = (k_in * jnp.exp(b_end - b_mid)).astype(BF16)
        v = v_ref[0, rows, :]
        a = _dot_nt(q_in.astype(BF16), k_in.astype(BF16))
        a = jnp.where(causal, a, 0.0).astype(BF16)
        out_ref[rows, :] = _dot(jnp.concatenate([q_st, a], axis=1),
                                jnp.concatenate([s.astype(BF16), v], axis=0))
        d_col = jnp.broadcast_to(jnp.exp(b_end), (HEAD_DIM, HEAD_DIM)).T
        return s * d_col + _dot_tn(k_st, v)

    def body(n, carry):
        s_fw = sfw_ref[...]
        s_bw = sbw_ref[...]
        for u in range(HG_UNROLL):
            c = n * HG_UNROLL + u
            s_fw = chunk(c, lffw_ref, kfw_ref, s_fw, ofw_ref, tri_fw, lower, C // 2 - 1, C - 1)
            s_bw = chunk(n_chunks - 1 - c, lfbw_ref, kbw_ref, s_bw, obw_ref, tri_bw, upper,
                         C // 2, 0)
        sfw_ref[...] = s_fw
        sbw_ref[...] = s_bw
        return carry

    lax.fori_loop(0, n_chunks // HG_UNROLL, body, 0)

    o = ofw_ref[...] + obw_ref[...]
    o_ref[0] = (_rms_rows(o, gain_ref[...]) * g_ref[0].astype(F32)).astype(BF16)


def _hgrn(hq, lf_fw, k_fw, lf_bw, k_bw, hv, hg, gain):
    B, L, W = hq.shape
    H = W // HEAD_DIM
    spec = pl.BlockSpec((1, L, HEAD_DIM), lambda b, h: (b, 0, h))
    return pl.pallas_call(
        _hgrn_kernel,
        grid=(B, H),
        in_specs=[spec] * 7 + [pl.BlockSpec((1, HEAD_DIM), lambda b, h: (0, 0))],
        out_specs=spec,
        out_shape=jax.ShapeDtypeStruct((B, L, W), BF16),
        scratch_shapes=[pltpu.VMEM((L, HEAD_DIM), F32), pltpu.VMEM((L, HEAD_DIM), F32),
                        pltpu.VMEM((HEAD_DIM, HEAD_DIM), F32), pltpu.VMEM((HEAD_DIM, HEAD_DIM), F32)],
        compiler_params=pltpu.CompilerParams(
            dimension_semantics=("arbitrary", "arbitrary"),
            vmem_limit_bytes=VMEM_LIMIT_BYTES),
        name="hgrn2",
    )(hq, lf_fw, k_fw, lf_bw, k_bw, hv, hg, gain)


ATTN_BLOCK_Q = 128
ATTN_BLOCK_K = ATTN_BLOCK_Q + 2 * DA_RADIUS


def _dilated_kernel(q_ref, k_ref, v_ref, o_ref, lse_ref, *, dilation, slopes):
    Ld = q_ref.shape[2]
    h = pl.program_id(2)
    slope = jnp.float32(slopes[0])
    for i in range(1, len(slopes)):
        slope = jnp.where(h == i, jnp.float32(slopes[i]), slope)
    slope = slope * float(dilation)
    bq, bk = ATTN_BLOCK_Q, ATTN_BLOCK_K
    row = lax.broadcasted_iota(jnp.int32, (bq, bk), 0)
    col = lax.broadcasted_iota(jnp.int32, (bq, bk), 1)
    lane = lax.broadcasted_iota(jnp.int32, (bq, HEAD_DIM), 1)

    def body(i, carry):
        q0 = pl.multiple_of(i * bq, bq)
        k0 = pl.multiple_of(jnp.clip(q0 - DA_RADIUS, 0, Ld - bk), DA_RADIUS)
        q = q_ref[0, 0, pl.ds(q0, bq), :]
        k = k_ref[0, 0, pl.ds(k0, bk), :]
        v = v_ref[0, 0, pl.ds(k0, bk), :]
        dist = jnp.abs(col - row + (k0 - q0))
        s = _dot_nt(q, k) - slope * dist.astype(F32)
        s = jnp.where(dist <= DA_RADIUS, s, NEG_INF)
        m = jnp.max(s, axis=-1, keepdims=True)
        p = jnp.exp(s - m)
        l = jnp.sum(p, axis=-1, keepdims=True)
        o = _dot(p.astype(BF16), v) / l
        o_ref[0, pl.ds(q0, bq), :] = o.astype(BF16)
        lse = m + jnp.log(l)

        @pl.when(h == 0)
        def _():
            lse_ref[0, pl.ds(q0, bq), :] = jnp.where(lane == 0, lse, 0.0)

        @pl.when(h != 0)
        def _():
            cur = lse_ref[0, pl.ds(q0, bq), :]
            lse_ref[0, pl.ds(q0, bq), :] = jnp.where(lane == h, lse, cur)

        return carry

    lax.fori_loop(0, Ld // bq, body, 0)


def _dilated_group(q, k, v, dilation, slopes):
    B, d, Ld, W = q.shape
    nh = W // HEAD_DIM
    L = d * Ld
    in_spec = pl.BlockSpec((1, 1, Ld, HEAD_DIM), lambda b, r, h: (b, r, 0, h))
    o, lse = pl.pallas_call(
        functools.partial(_dilated_kernel, dilation=dilation, slopes=tuple(slopes)),
        grid=(B, d, nh),
        in_specs=[in_spec] * 3,
        out_specs=[
            pl.BlockSpec((1, Ld, HEAD_DIM), lambda b, r, h: (b, 0, r * nh + h)),
            pl.BlockSpec((1, Ld, HEAD_DIM), lambda b, r, h: (b, 0, r)),
        ],
        out_shape=[jax.ShapeDtypeStruct((B, Ld, d * W), BF16),
                   jax.ShapeDtypeStruct((B, Ld, d * HEAD_DIM), F32)],
        compiler_params=pltpu.CompilerParams(
            dimension_semantics=("arbitrary", "arbitrary", "arbitrary"),
            vmem_limit_bytes=VMEM_LIMIT_BYTES),
        name=f"dilated_d{dilation}",
    )(q, k, v)
    return o.reshape(B, L, W), lse.reshape(B, L, HEAD_DIM)


def _mix_kernel(x_ref, ohg_ref, o0_ref, o1_ref, o2_ref, l0_ref, l1_ref, l2_ref, mq_ref,
                mk_ref, mv_ref, gates_ref, whg_ref, wda_ref, wmem_ref, wout_ref, out_ref):
    D = x_ref.shape[-1]
    mem_parts = []
    for h in range(MEM_HEADS):
        sl = slice(h * HEAD_DIM, (h + 1) * HEAD_DIM)
        s = _dot_nt(mq_ref[:, sl], mk_ref[0, :, sl])
        p = jnp.exp(s - jnp.max(s, axis=-1, keepdims=True))
        l = jnp.sum(p, axis=-1, keepdims=True)
        mem_parts.append((_dot(p.astype(BF16), mv_ref[0, :, sl]) / l).astype(BF16))
    o_mem = jnp.concatenate(mem_parts, axis=-1)

    da_parts = []
    for h in range(DA_HEADS_PER_GROUP):
        sl = slice(h * HEAD_DIM, (h + 1) * HEAD_DIM)
        ls = [r[:, h:h + 1] for r in (l0_ref, l1_ref, l2_ref)]
        m = jnp.maximum(jnp.maximum(ls[0], ls[1]), ls[2])
        es = [jnp.exp(t - m) for t in ls]
        inv = 1.0 / (es[0] + es[1] + es[2])
        acc = None
        for e, o_ref in zip(es, (o0_ref, o1_ref, o2_ref)):
            term = (e * inv) * o_ref[:, sl].astype(F32)
            acc = term if acc is None else acc + term
        da_parts.append(acc.astype(BF16))
    o_da = jnp.concatenate(da_parts, axis=-1)

    merged = (gates_ref[:, 0:D].astype(F32) * _dot(ohg_ref[...], whg_ref[...])
              + gates_ref[:, D:2 * D].astype(F32) * _dot(o_da, wda_ref[...])
              + gates_ref[:, 2 * D:3 * D].astype(F32) * _dot(o_mem, wmem_ref[...]))
    out_ref[...] = x_ref[...] + _dot(merged.astype(BF16), wout_ref[...])


def _mix(x2, o_hg, o_das, lses, mem_q, mem_k, mem_v, gates, w_hg, w_da, w_mem, w_out, L):
    T, D = x2.shape
    tm = 512
    nt = L // tm
    M = mem_k.shape[1]

    def tok(width):
        return pl.BlockSpec((tm, width), lambda i: (i, 0))

    def full(w):
        return pl.BlockSpec(w.shape, lambda i: (0, 0))

    mem_spec = pl.BlockSpec((1, M, mem_k.shape[2]), lambda i: (i // nt, 0, 0))
    return pl.pallas_call(
        _mix_kernel,
        grid=(T // tm,),
        in_specs=[tok(D), tok(o_hg.shape[1])] + [tok(GROUP_WIDTH)] * 3 + [tok(HEAD_DIM)] * 3
                 + [tok(mem_q.shape[1]), mem_spec, mem_spec, tok(gates.shape[1]),
                    full(w_hg), full(w_da), full(w_mem), full(w_out)],
        out_specs=tok(D),
        out_shape=jax.ShapeDtypeStruct((T, D), F32),
        compiler_params=pltpu.CompilerParams(
            dimension_semantics=("arbitrary",),
            vmem_limit_bytes=VMEM_LIMIT_BYTES),
        name="mix",
    )(x2, o_hg, *o_das, *lses, mem_q, mem_k, mem_v, gates, w_hg, w_da, w_mem, w_out)


def _ffn_kernel(x_ref, gain_ref, wa_ref, wb_ref, wo_ref, out_ref, h_ref, acc_ref):
    j = pl.program_id(1)

    @pl.when(j == 0)
    def _():
        h_ref[...] = _rms_rows(x_ref[...], gain_ref[...]).astype(BF16)
        acc_ref[...] = jnp.zeros_like(acc_ref)

    h = h_ref[...]
    a = _dot(h, wa_ref[...])
    b = _dot(h, wb_ref[...])
    acc_ref[...] += _dot((_silu(a) * b).astype(BF16), wo_ref[...])

    @pl.when(j == pl.num_programs(1) - 1)
    def _():
        out_ref[...] = x_ref[...] + acc_ref[...]


def _ffn(x2, gain, w_in, w_out):
    T, D = x2.shape
    d_ff = w_out.shape[0]
    tm = 1024
    tf = 256
    nf = d_ff // tf
    return pl.pallas_call(
        _ffn_kernel,
        grid=(T // tm, nf),
        in_specs=[
            pl.BlockSpec((tm, D), lambda i, j: (i, 0)),
            pl.BlockSpec((1, D), lambda i, j: (0, 0)),
            pl.BlockSpec((D, tf), lambda i, j: (0, j)),
            pl.BlockSpec((D, tf), lambda i, j: (0, nf + j)),
            pl.BlockSpec((tf, D), lambda i, j: (j, 0)),
        ],
        out_specs=pl.BlockSpec((tm, D), lambda i, j: (i, 0)),
        out_shape=jax.ShapeDtypeStruct((T, D), F32),
        scratch_shapes=[pltpu.VMEM((tm, D), BF16), pltpu.VMEM((tm, D), F32)],
        compiler_params=pltpu.CompilerParams(
            dimension_semantics=("arbitrary", "arbitrary"),
            vmem_limit_bytes=VMEM_LIMIT_BYTES),
        name="ffn",
    )(x2, gain, w_in, w_in, w_out)


def kernel(x, mem, norm_mix_gain, norm_mem_gain, w_in, lb_logits_fw, lb_logits_bw, hg_norm_gain,
           da_q_gain, da_k_gain, w_mem_kv, mem_q_gain, mem_k_gain, w_proj_hg, w_proj_da,
           w_proj_mem, w_out, norm_ffn_gain, w_ffn_in, w_ffn_out):
    B, L, D = x.shape
    depth = w_in.shape[0]
    assert depth == 1, "lower-bound tables are evaluated for a single layer"
    T = B * L
    slopes = _alibi_slopes(DA_HEADS)
    x2 = x.reshape(T, D)
    for l in range(depth):
        mem_k, mem_v = _mem_kv(mem, norm_mem_gain[l][None], w_mem_kv[l].astype(BF16),
                               mem_k_gain[l][None])
        (hq, lf_fw, k_fw, lf_bw, k_bw, hv, hg, q0, q1, q2, k0, k1, k2, v0, v1, v2,
         mem_q, gates) = _in_proj(
            x2, norm_mix_gain[l][None], w_in[l].astype(BF16), lb_logits_fw.astype(F32),
            lb_logits_bw.astype(F32), da_q_gain[l][None], da_k_gain[l][None],
            mem_q_gain[l][None], B, L)

        def seq(t):
            return t.reshape(B, L, t.shape[-1])

        o_hg = _hgrn(seq(hq), seq(lf_fw), seq(k_fw), seq(lf_bw), seq(k_bw), seq(hv), seq(hg),
                     hg_norm_gain[l][None])

        o_das, lses = [], []
        for g, ((_, dilation), q, k, v) in enumerate(zip(DA_CONFIGS, (q0, q1, q2), (k0, k1, k2),
                                                         (v0, v1, v2))):
            lo = g * DA_HEADS_PER_GROUP
            o, lse = _dilated_group(q, k, v, dilation, slopes[lo:lo + DA_HEADS_PER_GROUP])
            o_das.append(o.reshape(T, GROUP_WIDTH))
            lses.append(lse.reshape(T, HEAD_DIM))

        x2 = _mix(x2, o_hg.reshape(T, -1), o_das, lses, mem_q, mem_k, mem_v, gates,
                  w_proj_hg[l].astype(BF16), w_proj_da[l].astype(BF16),
                  w_proj_mem[l].astype(BF16), w_out[l].astype(BF16), L)
        x2 = _ffn(x2, norm_ffn_gain[l][None], w_ffn_in[l].astype(BF16), w_ffn_out[l].astype(BF16))
    return x2.reshape(B, L, D)
```

```python
import functools

import jax
import jax.numpy as jnp
from jax import lax
from jax.experimental import pallas as pl
from jax.experimental.pallas import tpu as pltpu

RMS_EPS = 1e-6
NEG_INF = -1e30
HEAD_DIM = 128
HG_CHUNK = 128
HG_UNROLL = 4
DA_CONFIGS = ((128, 1), (512, 4), (2048, 16))
DA_DILATIONS = tuple(d for _, d in DA_CONFIGS)
DA_RADIUS = 64
DA_HEADS_PER_GROUP = 4
DA_HEADS = DA_HEADS_PER_GROUP * len(DA_CONFIGS)
MEM_HEADS = 4
GROUP_WIDTH = DA_HEADS_PER_GROUP * HEAD_DIM
ATTN_SCALE = HEAD_DIM ** -0.5

BF16 = jnp.bfloat16
F32 = jnp.float32

VMEM_LIMIT_BYTES = 56 * 1024 * 1024

PROJ_TM = 1024
PROJ_RC = 256
PROJ_TN = 512


def _alibi_slopes(n):
    return [float(2.0 ** (-8.0 * (i + 1) / n)) for i in range(n)]


def _sigmoid(x):
    return 1.0 / (1.0 + jnp.exp(-x))


def _silu(x):
    return x * _sigmoid(x)


def _rms_rows(x, gain):
    ms = jnp.mean(x * x, axis=-1, keepdims=True)
    return x * lax.rsqrt(ms + RMS_EPS) * gain


def _head_norm(x, gain, scale):
    parts = []
    for h in range(x.shape[-1] // HEAD_DIM):
        parts.append(_rms_rows(x[:, h * HEAD_DIM:(h + 1) * HEAD_DIM], gain) * scale)
    return jnp.concatenate(parts, axis=-1)


def _dot(a, b):
    return jnp.dot(a, b, preferred_element_type=F32)


def _dot_nt(a, b):
    return lax.dot_general(a, b, (((1,), (1,)), ((), ())), preferred_element_type=F32)


def _dot_tn(a, b):
    return lax.dot_general(a, b, (((0,), (0,)), ((), ())), preferred_element_type=F32)


def _params(*semantics):
    return pltpu.CompilerParams(dimension_semantics=semantics,
                                vmem_limit_bytes=VMEM_LIMIT_BYTES)


def _mem_kv_kernel(mem_ref, gain_ref, w_ref, kgain_ref, k_ref, v_ref):
    mem_n = _rms_rows(mem_ref[0], gain_ref[...]).astype(BF16)
    kv = _dot(mem_n, w_ref[...])
    width = k_ref.shape[-1]
    k_ref[0] = _head_norm(kv[:, :width], kgain_ref[...], 1.0).astype(BF16)
    v_ref[0] = kv[:, width:].astype(BF16)


def _mem_kv(mem, gain, w_kv, k_gain):
    B, M, D = mem.shape
    width = w_kv.shape[1] // 2
    return pl.pallas_call(
        _mem_kv_kernel,
        grid=(B,),
        in_specs=[
            pl.BlockSpec((1, M, D), lambda b: (b, 0, 0)),
            pl.BlockSpec((1, D), lambda b: (0, 0)),
            pl.BlockSpec((D, 2 * width), lambda b: (0, 0)),
            pl.BlockSpec((1, HEAD_DIM), lambda b: (0, 0)),
        ],
        out_specs=[
            pl.BlockSpec((1, M, width), lambda b: (b, 0, 0)),
            pl.BlockSpec((1, M, width), lambda b: (b, 0, 0)),
        ],
        out_shape=[jax.ShapeDtypeStruct((B, M, width), BF16)] * 2,
        compiler_params=_params("arbitrary"),
        name="mem_kv",
    )(mem, gain, w_kv, k_gain)


def _rmsnorm_kernel(x_ref, gain_ref, h_ref):
    h_ref[...] = _rms_rows(x_ref[...], gain_ref[...]).astype(BF16)


def _rmsnorm(x2, gain):
    T, D = x2.shape
    return pl.pallas_call(
        _rmsnorm_kernel,
        grid=(T // PROJ_TM,),
        in_specs=[pl.BlockSpec((PROJ_TM, D), lambda i: (i, 0)),
                  pl.BlockSpec((1, D), lambda i: (0, 0))],
        out_specs=pl.BlockSpec((PROJ_TM, D), lambda i: (i, 0)),
        out_shape=jax.ShapeDtypeStruct((T, D), BF16),
        compiler_params=_params("arbitrary"),
        name="rmsnorm",
    )(x2, gain)


def _row_chunks(h_ref, w_ref, epilogue):
    for c in range(h_ref.shape[0] // PROJ_RC):
        rows = slice(c * PROJ_RC, (c + 1) * PROJ_RC)
        epilogue(c, rows, _dot(h_ref[rows, :], w_ref[...]))


def _store(out_ref, fn):
    def epilogue(c, rows, acc):
        out_ref[rows, :] = fn(acc).astype(out_ref.dtype)
    return epilogue


def _lower_bound(logits):
    e = jnp.exp(logits - jnp.max(logits, axis=0, keepdims=True))
    return e[0:1] / jnp.sum(e, axis=0, keepdims=True)


_HG_TILES_PER_SECTION = 2


def _proj_hgrn_kernel(h_ref, w_ref, lbfw_ref, lbbw_ref,
                      hq_ref, lffw_ref, kfw_ref, lfbw_ref, kbw_ref, hv_ref, hg_ref):
    section = pl.program_id(1) // _HG_TILES_PER_SECTION

    def forget(lb_ref, lf_ref, k_ref):
        def epilogue(c, rows, acc):
            lb = _lower_bound(lb_ref[...])
            f = lb + (1.0 - lb) * _sigmoid(acc)
            lf_ref[rows, :] = jnp.log(f)
            k_ref[rows, :] = (1.0 - f).astype(BF16)
        return epilogue

    epilogues = (
        _store(hq_ref, lambda t: _silu(t) * ATTN_SCALE),
        forget(lbfw_ref, lffw_ref, kfw_ref),
        forget(lbbw_ref, lfbw_ref, kbw_ref),
        _store(hv_ref, lambda t: t),
        _store(hg_ref, _silu),
    )
    for s, epilogue in enumerate(epilogues):
        @pl.when(section == s)
        def _(epilogue=epilogue):
            _row_chunks(h_ref, w_ref, epilogue)


def _proj_hgrn(h, w_in, lb_fw, lb_bw):
    T, D = h.shape
    tm, tn, tps = PROJ_TM, PROJ_TN, _HG_TILES_PER_SECTION
    n_sections = 5

    def sec_map(s):
        return lambda i, j: (i, jnp.clip(j - s * tps, 0, tps - 1))

    def lb_map(s):
        return lambda i, j: (0, jnp.clip(j - s * tps, 0, tps - 1))

    def out(s, dtype):
        return (pl.BlockSpec((tm, tn), sec_map(s)), jax.ShapeDtypeStruct((T, tps * tn), dtype))

    outs = [out(0, BF16), out(1, F32), out(1, BF16), out(2, F32), out(2, BF16),
            out(3, BF16), out(4, BF16)]
    return pl.pallas_call(
        _proj_hgrn_kernel,
        grid=(T // tm, n_sections * tps),
        in_specs=[
            pl.BlockSpec((tm, D), lambda i, j: (i, 0)),
            pl.BlockSpec((D, tn), lambda i, j: (0, j)),
            pl.BlockSpec((lb_fw.shape[0], tn), lb_map(1)),
            pl.BlockSpec((lb_bw.shape[0], tn), lb_map(2)),
        ],
        out_specs=[o[0] for o in outs],
        out_shape=[o[1] for o in outs],
        compiler_params=_params("arbitrary", "arbitrary"),
        name="proj_hgrn",
    )(h, w_in, lb_fw, lb_bw)


def _proj_attn_kernel(h_ref, w_ref, qg_ref, kg_ref, mg_ref,
                      q0_ref, q1_ref, q2_ref, k0_ref, k1_ref, k2_ref, v0_ref, v1_ref, v2_ref,
                      mq_ref, scr_ref):
    j = pl.program_id(1)
    nh = PROJ_TN // HEAD_DIM

    def class_major(out_ref, fn):
        d = out_ref.shape[1]
        n = PROJ_RC // d

        def epilogue(c, rows, acc):
            for h in range(nh):
                cols = slice(h * HEAD_DIM, (h + 1) * HEAD_DIM)
                vals = fn(acc[:, cols])
                if d == 1:
                    out_ref[0, 0, rows, cols] = vals.astype(BF16)
                    continue
                scr_ref[c, h] = vals
                for r in range(d):
                    out_ref[0, r, c * n:(c + 1) * n, cols] = (
                        scr_ref[c, h, pl.ds(r, n, stride=d), :].astype(BF16))
        return epilogue

    def q_norm(t):
        return _rms_rows(t, qg_ref[...]) * ATTN_SCALE

    def k_norm(t):
        return _rms_rows(t, kg_ref[...])

    def mq_norm(t):
        return _head_norm(t, mg_ref[...], ATTN_SCALE)

    epilogues = ([class_major(r, q_norm) for r in (q0_ref, q1_ref, q2_ref)]
                 + [class_major(r, k_norm) for r in (k0_ref, k1_ref, k2_ref)]
                 + [class_major(r, lambda t: t) for r in (v0_ref, v1_ref, v2_ref)]
                 + [_store(mq_ref, mq_norm)])
    for s, epilogue in enumerate(epilogues):
        @pl.when(j == s)
        def _(epilogue=epilogue):
            _row_chunks(h_ref, w_ref, epilogue)


def _proj_attn(h, w_in, col_tile0, q_gain, k_gain, mq_gain, B, L):
    T, D = h.shape
    tm, tn = PROJ_TM, PROJ_TN
    nt = L // tm

    def cls(d):
        return (pl.BlockSpec((1, d, tm // d, tn), lambda i, j: (i // nt, 0, i % nt, 0)),
                jax.ShapeDtypeStruct((B, d, L // d, tn), BF16))

    outs = [cls(d) for d in DA_DILATIONS] * 3
    outs.append((pl.BlockSpec((tm, tn), lambda i, j: (i, 0)), jax.ShapeDtypeStruct((T, tn), BF16)))
    gain_spec = pl.BlockSpec((1, HEAD_DIM), lambda i, j: (0, 0))
    return pl.pallas_call(
        _proj_attn_kernel,
        grid=(T // tm, len(outs)),
        in_specs=[
            pl.BlockSpec((tm, D), lambda i, j: (i, 0)),
            pl.BlockSpec((D, tn), lambda i, j: (0, col_tile0 + j)),
            gain_spec, gain_spec, gain_spec,
        ],
        out_specs=[o[0] for o in outs],
        out_shape=[o[1] for o in outs],
        scratch_shapes=[pltpu.VMEM((tm // PROJ_RC, tn // HEAD_DIM, PROJ_RC, HEAD_DIM), F32)],
        compiler_params=_params("arbitrary", "arbitrary"),
        name="proj_attn",
    )(h, w_in, q_gain, k_gain, mq_gain)


def _proj_gates_kernel(h_ref, w_ref, gates_ref):
    _row_chunks(h_ref, w_ref, _store(gates_ref, _sigmoid))


def _proj_gates(h, w_in, col0, width):
    T, D = h.shape
    tm, tn = PROJ_TM, 2 * PROJ_TN
    tile0 = col0 // tn
    return pl.pallas_call(
        _proj_gates_kernel,
        grid=(T // tm, width // tn),
        in_specs=[pl.BlockSpec((tm, D), lambda i, j: (i, 0)),
                  pl.BlockSpec((D, tn), lambda i, j: (0, tile0 + j))],
        out_specs=pl.BlockSpec((tm, tn), lambda i, j: (i, j)),
        out_shape=jax.ShapeDtypeStruct((T, width), BF16),
        compiler_params=_params("arbitrary", "arbitrary"),
        name="proj_gates",
    )(h, w_in)


def _hgrn_kernel(q_ref, lffw_ref, kfw_ref, lfbw_ref, kbw_ref, v_ref, g_ref, gain_ref,
                 o_ref, ofw_ref, obw_ref, sfw_ref, sbw_ref):
    L = q_ref.shape[1]
    C = HG_CHUNK
    n_chunks = L // C
    row = lax.broadcasted_iota(jnp.int32, (C, C), 0)
    col = lax.broadcasted_iota(jnp.int32, (C, C), 1)
    lower = row >= col
    upper = row <= col
    tri_fw = jnp.where(lower, 1.0, 0.0).astype(BF16)
    tri_bw = jnp.where(upper, 1.0, 0.0).astype(BF16)

    sfw_ref[...] = jnp.zeros_like(sfw_ref)
    sbw_ref[...] = jnp.zeros_like(sbw_ref)

    def chunk(c, lf_ref, k_ref, s, out_ref, tri, causal, mid_row, end_row):
        rows = pl.ds(pl.multiple_of(c * C, C), C)
        lf = lf_ref[0, rows, :]
        lf_hi = lf.astype(BF16)
        lf_lo = (lf - lf_hi.astype(F32)).astype(BF16)
        bb = _dot(tri, jnp.concatenate([lf_hi, lf_lo], axis=1))
        b = bb[:, :HEAD_DIM] + bb[:, HEAD_DIM:]
        b_mid = b[mid_row:mid_row + 1, :]
        b_end = b[end_row:end_row + 1, :]
        q_in = q_ref[0, rows, :].astype(F32) * jnp.exp(b - b_mid)
        k_in = k_ref[0, rows, :].astype(F32) * jnp.exp(b_mid - b)
        q_st = (q_in * jnp.exp(b_mid)).astype(BF16)
        k_st = (k_in * jnp.exp(b_end - b_mid)).astype(BF16)
        v = v_ref[0, rows, :]
        a = _dot_nt(q_in.astype(BF16), k_in.astype(BF16))
        a = jnp.where(causal, a, 0.0).astype(BF16)
        out_ref[rows, :] = _dot(jnp.concatenate([q_st, a], axis=1),
                                jnp.concatenate([s.astype(BF16), v], axis=0))
        d_col = jnp.broadcast_to(jnp.exp(b_end), (HEAD_DIM, HEAD_DIM)).T
        return s * d_col + _dot_tn(k_st, v)

    def body(n, carry):
        s_fw = sfw_ref[...]
        s_bw = sbw_ref[...]
        for u in range(HG_UNROLL):
            c = n * HG_UNROLL + u
            s_fw = chunk(c, lffw_ref, kfw_ref, s_fw, ofw_ref, tri_fw, lower, C // 2 - 1, C - 1)
            s_bw = chunk(n_chunks - 1 - c, lfbw_ref, kbw_ref, s_bw, obw_ref, tri_bw, upper,
                         C // 2, 0)
        sfw_ref[...] = s_fw
        sbw_ref[...] = s_bw
        return carry

    lax.fori_loop(0, n_chunks // HG_UNROLL, body, 0)

    o = ofw_ref[...] + obw_ref[...]
    o_ref[0] = (_rms_rows(o, gain_ref[...]) * g_ref[0].astype(F32)).astype(BF16)


def _hgrn(hq, lf_fw, k_fw, lf_bw, k_bw, hv, hg, gain):
    B, L, W = hq.shape
    H = W // HEAD_DIM
    spec = pl.BlockSpec((1, L, HEAD_DIM), lambda b, h: (b, 0, h))
    return pl.pallas_call(
        _hgrn_kernel,
        grid=(B, H),
        in_specs=[spec] * 7 + [pl.BlockSpec((1, HEAD_DIM), lambda b, h: (0, 0))],
        out_specs=spec,
        out_shape=jax.ShapeDtypeStruct((B, L, W), BF16),
        scratch_shapes=[pltpu.VMEM((L, HEAD_DIM), F32), pltpu.VMEM((L, HEAD_DIM), F32),
                        pltpu.VMEM((HEAD_DIM, HEAD_DIM), F32), pltpu.VMEM((HEAD_DIM, HEAD_DIM), F32)],
        compiler_params=_params("arbitrary", "arbitrary"),
        name="hgrn2",
    )(hq, lf_fw, k_fw, lf_bw, k_bw, hv, hg, gain)


DA_TQ = 2048
DA_BQ = 128
DA_BK = DA_BQ + 2 * DA_RADIUS
DA_BLOCKS = DA_TQ // DA_BQ
DA_MERGE_ROWS = 256


def _dilated_kernel(q0_ref, q1_ref, q2_ref, k0_ref, k1_ref, k2_ref, v0_ref, v1_ref, v2_ref,
                    o_ref, og_ref, lse_ref, bias_ref, *, slopes):
    h = pl.program_id(1)
    t = pl.program_id(2)
    bq, bk = DA_BQ, DA_BK
    groups = tuple(zip(DA_DILATIONS, (q0_ref, q1_ref, q2_ref), (k0_ref, k1_ref, k2_ref),
                       (v0_ref, v1_ref, v2_ref)))

    row = lax.broadcasted_iota(jnp.int32, (bq, bk), 0)
    col = lax.broadcasted_iota(jnp.int32, (bq, bk), 1)
    for g, d in enumerate(DA_DILATIONS):
        slope = jnp.float32(slopes[g * DA_HEADS_PER_GROUP])
        for i in range(1, DA_HEADS_PER_GROUP):
            slope = jnp.where(h == i, jnp.float32(slopes[g * DA_HEADS_PER_GROUP + i]), slope)
        slope = slope * float(d)
        for variant, offset in enumerate((-DA_RADIUS, 0, -2 * DA_RADIUS)):
            dist = jnp.abs(col - row + offset)
            bias_ref[g, variant] = jnp.where(dist <= DA_RADIUS, -slope * dist.astype(F32), NEG_INF)

    ones = jnp.ones((bk, HEAD_DIM), BF16)

    def block(idx, g, d, q_ref, k_ref, v_ref):
        Ld = k_ref.shape[2]
        per_class = DA_TQ // d // bq
        r = 0 if d == 1 else idx // per_class
        i = idx if d == 1 else idx % per_class
        n0 = t * (DA_TQ // d) + i * bq
        k0 = pl.multiple_of(jnp.clip(n0 - DA_RADIUS, 0, Ld - bk), DA_RADIUS)
        variant = jnp.where(n0 == 0, 1, jnp.where(n0 == Ld - bq, 2, 0))
        q = q_ref[0, r, pl.ds(pl.multiple_of(i * bq, bq), bq), :]
        k = k_ref[0, r, pl.ds(k0, bk), :]
        v = v_ref[0, r, pl.ds(k0, bk), :]
        s = _dot_nt(q, k) + bias_ref[g, variant]
        m = jnp.max(s, axis=-1, keepdims=True)
        p = jnp.exp(s - m).astype(BF16)
        ol = _dot(p, jnp.concatenate([v, ones], axis=1))
        l = ol[:, HEAD_DIM:]
        o = ol[:, :HEAD_DIM] / l
        lse = m + jnp.log(l)
        if d == 1:
            dst = pl.ds(pl.multiple_of(i * bq, bq), bq)
        else:
            dst = pl.ds(i * (bq * d) + r, bq, stride=d)
        og_ref[g, dst, :] = o
        lse_ref[g, dst, :] = lse

    def body(idx, carry):
        for g, (d, q_ref, k_ref, v_ref) in enumerate(groups):
            block(idx, g, d, q_ref, k_ref, v_ref)
        return carry

    lax.fori_loop(0, DA_BLOCKS, body, 0, unroll=4)

    for c in range(DA_TQ // DA_MERGE_ROWS):
        rows = slice(c * DA_MERGE_ROWS, (c + 1) * DA_MERGE_ROWS)
        ls = [lse_ref[g, rows, :] for g in range(len(groups))]
        m = jnp.maximum(jnp.maximum(ls[0], ls[1]), ls[2])
        es = [jnp.exp(x - m) for x in ls]
        inv = 1.0 / (es[0] + es[1] + es[2])
        acc = (es[0] * inv) * og_ref[0, rows, :]
        for g in range(1, len(groups)):
            acc = acc + (es[g] * inv) * og_ref[g, rows, :]
        o_ref[0, rows, :] = acc.astype(BF16)


def _dilated(qs, ks, vs, slopes):
    B, _, L, W = qs[0].shape
    nh = W // HEAD_DIM
    assert L % DA_TQ == 0 and all(L // d >= DA_BK and DA_TQ // d >= DA_BQ for d in DA_DILATIONS)

    def q_spec(d):
        return pl.BlockSpec((1, d, DA_TQ // d, HEAD_DIM), lambda b, h, t: (b, 0, t, h))

    def kv_spec(d):
        return pl.BlockSpec((1, d, L // d, HEAD_DIM), lambda b, h, t: (b, 0, 0, h))

    n_groups = len(DA_DILATIONS)
    return pl.pallas_call(
        functools.partial(_dilated_kernel, slopes=tuple(slopes)),
        grid=(B, nh, L // DA_TQ),
        in_specs=([q_spec(d) for d in DA_DILATIONS] + [kv_spec(d) for d in DA_DILATIONS] * 2),
        out_specs=pl.BlockSpec((1, DA_TQ, HEAD_DIM), lambda b, h, t: (b, t, h)),
        out_shape=jax.ShapeDtypeStruct((B, L, W), BF16),
        scratch_shapes=[pltpu.VMEM((n_groups, DA_TQ, HEAD_DIM), F32),
                        pltpu.VMEM((n_groups, DA_TQ, HEAD_DIM), F32),
                        pltpu.VMEM((n_groups, 3, DA_BQ, DA_BK), F32)],
        compiler_params=_params("arbitrary", "arbitrary", "arbitrary"),
        name="dilated",
    )(*qs, *ks, *vs)


def _mix_kernel(x_ref, ohg_ref, oda_ref, mq_ref, mk_ref, mv_ref, gates_ref,
                whg_ref, wda_ref, wmem_ref, wout_ref, out_ref):
    D = x_ref.shape[-1]
    mem_parts = []
    for h in range(MEM_HEADS):
        sl = slice(h * HEAD_DIM, (h + 1) * HEAD_DIM)
        s = _dot_nt(mq_ref[:, sl], mk_ref[0, :, sl])
        p = jnp.exp(s - jnp.max(s, axis=-1, keepdims=True))
        l = jnp.sum(p, axis=-1, keepdims=True)
        mem_parts.append((_dot(p.astype(BF16), mv_ref[0, :, sl]) / l).astype(BF16))
    o_mem = jnp.concatenate(mem_parts, axis=-1)

    merged = (gates_ref[:, 0:D].astype(F32) * _dot(ohg_ref[...], whg_ref[...])
              + gates_ref[:, D:2 * D].astype(F32) * _dot(oda_ref[...], wda_ref[...])
              + gates_ref[:, 2 * D:3 * D].astype(F32) * _dot(o_mem, wmem_ref[...]))
    out_ref[...] = x_ref[...] + _dot(merged.astype(BF16), wout_ref[...])


def _mix(x2, o_hg, o_da, mem_q, mem_k, mem_v, gates, w_hg, w_da, w_mem, w_out, L):
    T, D = x2.shape
    tm = 512
    nt = L // tm
    M = mem_k.shape[1]

    def tok(width):
        return pl.BlockSpec((tm, width), lambda i: (i, 0))

    def full(w):
        return pl.BlockSpec(w.shape, lambda i: (0, 0))

    mem_spec = pl.BlockSpec((1, M, mem_k.shape[2]), lambda i: (i // nt, 0, 0))
    return pl.pallas_call(
        _mix_kernel,
        grid=(T // tm,),
        in_specs=[tok(D), tok(o_hg.shape[1]), tok(o_da.shape[1]), tok(mem_q.shape[1]),
                  mem_spec, mem_spec, tok(gates.shape[1]),
                  full(w_hg), full(w_da), full(w_mem), full(w_out)],
        out_specs=tok(D),
        out_shape=jax.ShapeDtypeStruct((T, D), F32),
        compiler_params=_params("arbitrary"),
        name="mix",
    )(x2, o_hg, o_da, mem_q, mem_k, mem_v, gates, w_hg, w_da, w_mem, w_out)


def _ffn_kernel(x_ref, gain_ref, wa_ref, wb_ref, wo_ref, out_ref, h_ref, acc_ref):
    j = pl.program_id(1)

    @pl.when(j == 0)
    def _():
        h_ref[...] = _rms_rows(x_ref[...], gain_ref[...]).astype(BF16)
        acc_ref[...] = jnp.zeros_like(acc_ref)

    h = h_ref[...]
    a = _dot(h, wa_ref[...])
    b = _dot(h, wb_ref[...])
    acc_ref[...] += _dot((_silu(a) * b).astype(BF16), wo_ref[...])

    @pl.when(j == pl.num_programs(1) - 1)
    def _():
        out_ref[...] = x_ref[...] + acc_ref[...]


def _ffn(x2, gain, w_in, w_out):
    T, D = x2.shape
    d_ff = w_out.shape[0]
    tm = 1024
    tf = 256
    nf = d_ff // tf
    return pl.pallas_call(
        _ffn_kernel,
        grid=(T // tm, nf),
        in_specs=[
            pl.BlockSpec((tm, D), lambda i, j: (i, 0)),
            pl.BlockSpec((1, D), lambda i, j: (0, 0)),
            pl.BlockSpec((D, tf), lambda i, j: (0, j)),
            pl.BlockSpec((D, tf), lambda i, j: (0, nf + j)),
            pl.BlockSpec((tf, D), lambda i, j: (j, 0)),
        ],
        out_specs=pl.BlockSpec((tm, D), lambda i, j: (i, 0)),
        out_shape=jax.ShapeDtypeStruct((T, D), F32),
        scratch_shapes=[pltpu.VMEM((tm, D), BF16), pltpu.VMEM((tm, D), F32)],
        compiler_params=_params("arbitrary", "arbitrary"),
        name="ffn",
    )(x2, gain, w_in, w_in, w_out)


def kernel(x, mem, norm_mix_gain, norm_mem_gain, w_in, lb_logits_fw, lb_logits_bw, hg_norm_gain,
           da_q_gain, da_k_gain, w_mem_kv, mem_q_gain, mem_k_gain, w_proj_hg, w_proj_da,
           w_proj_mem, w_out, norm_ffn_gain, w_ffn_in, w_ffn_out):
    B, L, D = x.shape
    depth = w_in.shape[0]
    assert depth == 1, "lower-bound tables are evaluated for a single layer"
    T = B * L
    slopes = _alibi_slopes(DA_HEADS)
    hg_width = w_proj_hg.shape[1]
    attn_col0 = 5 * hg_width
    gates_col0 = attn_col0 + (3 * len(DA_CONFIGS) + 1) * GROUP_WIDTH
    x2 = x.reshape(T, D)
    for l in range(depth):
        mem_k, mem_v = _mem_kv(mem, norm_mem_gain[l][None], w_mem_kv[l].astype(BF16),
                               mem_k_gain[l][None])
        w_in_l = w_in[l].astype(BF16)
        h = _rmsnorm(x2, norm_mix_gain[l][None])
        hq, lf_fw, k_fw, lf_bw, k_bw, hv, hg = _proj_hgrn(
            h, w_in_l, lb_logits_fw.astype(F32), lb_logits_bw.astype(F32))
        (q0, q1, q2, k0, k1, k2, v0, v1, v2, mem_q) = _proj_attn(
            h, w_in_l, attn_col0 // PROJ_TN, da_q_gain[l][None], da_k_gain[l][None],
            mem_q_gain[l][None], B, L)
        gates = _proj_gates(h, w_in_l, gates_col0, 3 * D)

        def seq(t):
            return t.reshape(B, L, t.shape[-1])

        o_hg = _hgrn(seq(hq), seq(lf_fw), seq(k_fw), seq(lf_bw), seq(k_bw), seq(hv), seq(hg),
                     hg_norm_gain[l][None])
        o_da = _dilated((q0, q1, q2), (k0, k1, k2), (v0, v1, v2), slopes)

        x2 = _mix(x2, o_hg.reshape(T, -1), o_da.reshape(T, -1), mem_q, mem_k, mem_v, gates,
                  w_proj_hg[l].astype(BF16), w_proj_da[l].astype(BF16),
                  w_proj_mem[l].astype(BF16), w_out[l].astype(BF16), L)
        x2 = _ffn(x2, norm_ffn_gain[l][None], w_ffn_in[l].astype(BF16), w_ffn_out[l].astype(BF16))
    return x2.reshape(B, L, D)
```

```python
import functools

import jax
import jax.numpy as jnp
from jax import lax
from jax.experimental import pallas as pl
from jax.experimental.pallas import tpu as pltpu

RMS_EPS = 1e-6
NEG_INF = -1e30
HEAD_DIM = 128
HG_CHUNK = 128
HG_UNROLL = 4
DA_CONFIGS = ((128, 1), (512, 4), (2048, 16))
DA_DILATIONS = tuple(d for _, d in DA_CONFIGS)
DA_RADIUS = 64
DA_HEADS_PER_GROUP = 4
DA_HEADS = DA_HEADS_PER_GROUP * len(DA_CONFIGS)
MEM_HEADS = 4
GROUP_WIDTH = DA_HEADS_PER_GROUP * HEAD_DIM
ATTN_SCALE = HEAD_DIM ** -0.5

BF16 = jnp.bfloat16
F32 = jnp.float32

VMEM_LIMIT_BYTES = 56 * 1024 * 1024

TOKEN_TILE = 512
ROW_CHUNK = 256


def _alibi_slopes(n):
    return [float(2.0 ** (-8.0 * (i + 1) / n)) for i in range(n)]


def _sigmoid(x):
    return 1.0 / (1.0 + jnp.exp(-x))


def _silu(x):
    return x * _sigmoid(x)


def _rms_rows(x, gain):
    ms = jnp.mean(x * x, axis=-1, keepdims=True)
    return x * lax.rsqrt(ms + RMS_EPS) * gain


def _head_norm(x, gain, scale):
    parts = []
    for h in range(x.shape[-1] // HEAD_DIM):
        parts.append(_rms_rows(x[:, h * HEAD_DIM:(h + 1) * HEAD_DIM], gain) * scale)
    return jnp.concatenate(parts, axis=-1)


def _dot(a, b):
    return jnp.dot(a, b, preferred_element_type=F32)


def _dot_nt(a, b):
    return lax.dot_general(a, b, (((1,), (1,)), ((), ())), preferred_element_type=F32)


def _dot_tn(a, b):
    return lax.dot_general(a, b, (((0,), (0,)), ((), ())), preferred_element_type=F32)


def _params(*semantics):
    return pltpu.CompilerParams(dimension_semantics=semantics,
                                vmem_limit_bytes=VMEM_LIMIT_BYTES)


def _resident(shape, index_map):
    return pl.BlockSpec(shape, index_map, pipeline_mode=pl.Buffered(1))


def _for_row_chunks(n_rows, body):
    def step(c, carry):
        body(c, pl.ds(pl.multiple_of(c * ROW_CHUNK, ROW_CHUNK), ROW_CHUNK))
        return carry
    lax.fori_loop(0, n_rows // ROW_CHUNK, step, 0)


def _mem_kv_kernel(mem_ref, gain_ref, w_ref, kgain_ref, k_ref, v_ref):
    mem_n = _rms_rows(mem_ref[0], gain_ref[...]).astype(BF16)
    kv = _dot(mem_n, w_ref[...])
    width = k_ref.shape[-1]
    k_ref[0] = _head_norm(kv[:, :width], kgain_ref[...], 1.0).astype(BF16)
    v_ref[0] = kv[:, width:].astype(BF16)


def _mem_kv(mem, gain, w_kv, k_gain):
    B, M, D = mem.shape
    width = w_kv.shape[1] // 2
    return pl.pallas_call(
        _mem_kv_kernel,
        grid=(B,),
        in_specs=[
            pl.BlockSpec((1, M, D), lambda b: (b, 0, 0)),
            pl.BlockSpec((1, D), lambda b: (0, 0)),
            pl.BlockSpec((D, 2 * width), lambda b: (0, 0)),
            pl.BlockSpec((1, HEAD_DIM), lambda b: (0, 0)),
        ],
        out_specs=[
            pl.BlockSpec((1, M, width), lambda b: (b, 0, 0)),
            pl.BlockSpec((1, M, width), lambda b: (b, 0, 0)),
        ],
        out_shape=[jax.ShapeDtypeStruct((B, M, width), BF16)] * 2,
        compiler_params=_params("arbitrary"),
        name="mem_kv",
    )(mem, gain, w_kv, k_gain)


def _lower_bound(logits):
    e = jnp.exp(logits - jnp.max(logits, axis=0, keepdims=True))
    return e[0:1] / jnp.sum(e, axis=0, keepdims=True)


def _proj_hgrn_kernel(x_ref, gain_ref, w_ref, lbfw_ref, lbbw_ref,
                      h_ref, hq_ref, lffw_ref, kfw_ref, lfbw_ref, kbw_ref, hv_ref, hg_ref):
    W = hq_ref.shape[1]

    def section(h, s):
        return _dot(h, w_ref[:, s * W:(s + 1) * W])

    def forget(acc, lb_ref, rows, lf_ref, k_ref):
        lb = _lower_bound(lb_ref[...])
        f = lb + (1.0 - lb) * _sigmoid(acc)
        lf_ref[rows, :] = jnp.log(f)
        k_ref[rows, :] = (1.0 - f).astype(BF16)

    def body(c, rows):
        h = _rms_rows(x_ref[rows, :], gain_ref[...]).astype(BF16)
        h_ref[rows, :] = h
        hq_ref[rows, :] = (_silu(section(h, 0)) * ATTN_SCALE).astype(BF16)
        forget(section(h, 1), lbfw_ref, rows, lffw_ref, kfw_ref)
        forget(section(h, 2), lbbw_ref, rows, lfbw_ref, kbw_ref)
        hv_ref[rows, :] = section(h, 3).astype(BF16)
        hg_ref[rows, :] = _silu(section(h, 4)).astype(BF16)

    _for_row_chunks(x_ref.shape[0], body)


def _proj_hgrn(x2, gain, w_hg, lb_fw, lb_bw):
    T, D = x2.shape
    W = w_hg.shape[1] // 5
    tm = TOKEN_TILE
    const = lambda i: (0, 0)
    tok = lambda width: pl.BlockSpec((tm, width), lambda i: (i, 0))
    outs = [(D, BF16), (W, BF16), (W, F32), (W, BF16), (W, F32), (W, BF16), (W, BF16), (W, BF16)]
    return pl.pallas_call(
        _proj_hgrn_kernel,
        grid=(T // tm,),
        in_specs=[tok(D), pl.BlockSpec((1, D), const), _resident(w_hg.shape, const),
                  pl.BlockSpec(lb_fw.shape, const), pl.BlockSpec(lb_bw.shape, const)],
        out_specs=[tok(w) for w, _ in outs],
        out_shape=[jax.ShapeDtypeStruct((T, w), dt) for w, dt in outs],
        compiler_params=_params("arbitrary"),
        name="proj_hgrn",
    )(x2, gain, w_hg, lb_fw, lb_bw)


def _proj_attn_kernel(h_ref, w_ref, qg_ref, kg_ref, mg_ref,
                      q0_ref, q1_ref, q2_ref, k0_ref, k1_ref, k2_ref, v0_ref, v1_ref, v2_ref,
                      mq_ref, gates_ref, scr_ref):
    nh = GROUP_WIDTH // HEAD_DIM
    gates_col0 = w_ref.shape[1] - gates_ref.shape[1]
    D = gates_ref.shape[1] // 3

    def q_norm(t):
        return _rms_rows(t, qg_ref[...]) * ATTN_SCALE

    def k_norm(t):
        return _rms_rows(t, kg_ref[...])

    def body(c, rows):
        h = h_ref[rows, :]

        def class_major(tile, out_ref, fn):
            acc = _dot(h, w_ref[:, tile * GROUP_WIDTH:(tile + 1) * GROUP_WIDTH])
            d = out_ref.shape[1]
            n = ROW_CHUNK // d
            for hh in range(nh):
                cols = slice(hh * HEAD_DIM, (hh + 1) * HEAD_DIM)
                vals = fn(acc[:, cols])
                if d == 1:
                    out_ref[0, 0, rows, cols] = vals.astype(BF16)
                    continue
                scr_ref[tile, hh] = vals
                dst = pl.ds(pl.multiple_of(c * n, n), n)
                for r in range(d):
                    out_ref[0, r, dst, cols] = scr_ref[tile, hh, pl.ds(r, n, stride=d), :].astype(BF16)

        for g, (q_ref, k_ref, v_ref) in enumerate(((q0_ref, k0_ref, v0_ref), (q1_ref, k1_ref, v1_ref),
                                                   (q2_ref, k2_ref, v2_ref))):
            class_major(g, q_ref, q_norm)
            class_major(3 + g, k_ref, k_norm)
            class_major(6 + g, v_ref, lambda t: t)
        mq = _dot(h, w_ref[:, 9 * GROUP_WIDTH:10 * GROUP_WIDTH])
        mq_ref[rows, :] = _head_norm(mq, mg_ref[...], ATTN_SCALE).astype(BF16)
        for s in range(3):
            gate = _dot(h, w_ref[:, gates_col0 + s * D:gates_col0 + (s + 1) * D])
            gates_ref[rows, s * D:(s + 1) * D] = _sigmoid(gate).astype(BF16)

    _for_row_chunks(h_ref.shape[0], body)


def _proj_attn(h, w_at, q_gain, k_gain, mq_gain, B, L):
    T, D = h.shape
    tm = TOKEN_TILE
    nt = L // tm
    n_groups = len(DA_DILATIONS)
    gates_width = w_at.shape[1] - (3 * n_groups + 1) * GROUP_WIDTH
    const = lambda i: (0, 0)

    def cls(d):
        return (pl.BlockSpec((1, d, tm // d, GROUP_WIDTH), lambda i: (i // nt, 0, i % nt, 0)),
                jax.ShapeDtypeStruct((B, d, L // d, GROUP_WIDTH), BF16))

    outs = [cls(d) for _ in range(3) for d in DA_DILATIONS]
    for width in (GROUP_WIDTH, gates_width):
        outs.append((pl.BlockSpec((tm, width), lambda i: (i, 0)),
                     jax.ShapeDtypeStruct((T, width), BF16)))
    gain_spec = pl.BlockSpec((1, HEAD_DIM), const)
    return pl.pallas_call(
        _proj_attn_kernel,
        grid=(T // tm,),
        in_specs=[pl.BlockSpec((tm, D), lambda i: (i, 0)), _resident(w_at.shape, const),
                  gain_spec, gain_spec, gain_spec],
        out_specs=[o[0] for o in outs],
        out_shape=[o[1] for o in outs],
        scratch_shapes=[pltpu.VMEM((3 * n_groups, GROUP_WIDTH // HEAD_DIM, ROW_CHUNK, HEAD_DIM), F32)],
        compiler_params=_params("arbitrary"),
        name="proj_attn",
    )(h, w_at, q_gain, k_gain, mq_gain)


def _hgrn_kernel(q_ref, lffw_ref, kfw_ref, lfbw_ref, kbw_ref, v_ref, g_ref, gain_ref,
                 o_ref, ofw_ref, obw_ref, sfw_ref, sbw_ref):
    L = q_ref.shape[1]
    C = HG_CHUNK
    n_chunks = L // C
    row = lax.broadcasted_iota(jnp.int32, (C, C), 0)
    col = lax.broadcasted_iota(jnp.int32, (C, C), 1)
    lower = row >= col
    upper = row <= col
    tri_fw = jnp.where(lower, 1.0, 0.0).astype(BF16)
    tri_bw = jnp.where(upper, 1.0, 0.0).astype(BF16)

    sfw_ref[...] = jnp.zeros_like(sfw_ref)
    sbw_ref[...] = jnp.zeros_like(sbw_ref)

    def chunk(c, lf_ref, k_ref, s, out_ref, tri, causal, mid_row, end_row):
        rows = pl.ds(pl.multiple_of(c * C, C), C)
        lf = lf_ref[0, rows, :]
        lf_hi = lf.astype(BF16)
        lf_lo = (lf - lf_hi.astype(F32)).astype(BF16)
        bb = _dot(tri, jnp.concatenate([lf_hi, lf_lo], axis=1))
        b = bb[:, :HEAD_DIM] + bb[:, HEAD_DIM:]
        b_mid = b[mid_row:mid_row + 1, :]
        b_end = b[end_row:end_row + 1, :]
        q_in = q_ref[0, rows, :].astype(F32) * jnp.exp(b - b_mid)
        k_in = k_ref[0, rows, :].astype(F32) * jnp.exp(b_mid - b)
        q_st = (q_in * jnp.exp(b_mid)).astype(BF16)
        k_st = (k_in * jnp.exp(b_end - b_mid)).astype(BF16)
        v = v_ref[0, rows, :]
        a = _dot_nt(q_in.astype(BF16), k_in.astype(BF16))
        a = jnp.where(causal, a, 0.0).astype(BF16)
        out_ref[rows, :] = _dot(jnp.concatenate([q_st, a], axis=1),
                                jnp.concatenate([s.astype(BF16), v], axis=0))
        d_col = jnp.broadcast_to(jnp.exp(b_end), (HEAD_DIM, HEAD_DIM)).T
        return s * d_col + _dot_tn(k_st, v)

    def body(n, carry):
        s_fw = sfw_ref[...]
        s_bw = sbw_ref[...]
        for u in range(HG_UNROLL):
            c = n * HG_UNROLL + u
            s_fw = chunk(c, lffw_ref, kfw_ref, s_fw, ofw_ref, tri_fw, lower, C // 2 - 1, C - 1)
            s_bw = chunk(n_chunks - 1 - c, lfbw_ref, kbw_ref, s_bw, obw_ref, tri_bw, upper,
                         C // 2, 0)
        sfw_ref[...] = s_fw
        sbw_ref[...] = s_bw
        return carry

    lax.fori_loop(0, n_chunks // HG_UNROLL, body, 0)

    o = ofw_ref[...] + obw_ref[...]
    o_ref[0] = (_rms_rows(o, gain_ref[...]) * g_ref[0].astype(F32)).astype(BF16)


def _hgrn(hq, lf_fw, k_fw, lf_bw, k_bw, hv, hg, gain):
    B, L, W = hq.shape
    H = W // HEAD_DIM
    spec = pl.BlockSpec((1, L, HEAD_DIM), lambda b, h: (b, 0, h))
    return pl.pallas_call(
        _hgrn_kernel,
        grid=(B, H),
        in_specs=[spec] * 7 + [pl.BlockSpec((1, HEAD_DIM), lambda b, h: (0, 0))],
        out_specs=spec,
        out_shape=jax.ShapeDtypeStruct((B, L, W), BF16),
        scratch_shapes=[pltpu.VMEM((L, HEAD_DIM), F32), pltpu.VMEM((L, HEAD_DIM), F32),
                        pltpu.VMEM((HEAD_DIM, HEAD_DIM), F32), pltpu.VMEM((HEAD_DIM, HEAD_DIM), F32)],
        compiler_params=_params("arbitrary", "arbitrary"),
        name="hgrn2",
    )(hq, lf_fw, k_fw, lf_bw, k_bw, hv, hg, gain)


DA_TQ = 2048
DA_BQ = 128
DA_BK = DA_BQ + 2 * DA_RADIUS
DA_BLOCKS = DA_TQ // DA_BQ
DA_MERGE_ROWS = 256


def _dilated_kernel(q0_ref, q1_ref, q2_ref, k0_ref, k1_ref, k2_ref, v0_ref, v1_ref, v2_ref,
                    o_ref, og_ref, lse_ref, bias_ref, *, slopes):
    h = pl.program_id(1)
    t = pl.program_id(2)
    bq, bk = DA_BQ, DA_BK
    groups = tuple(zip(DA_DILATIONS, (q0_ref, q1_ref, q2_ref), (k0_ref, k1_ref, k2_ref),
                       (v0_ref, v1_ref, v2_ref)))

    row = lax.broadcasted_iota(jnp.int32, (bq, bk), 0)
    col = lax.broadcasted_iota(jnp.int32, (bq, bk), 1)
    for g, d in enumerate(DA_DILATIONS):
        slope = jnp.float32(slopes[g * DA_HEADS_PER_GROUP])
        for i in range(1, DA_HEADS_PER_GROUP):
            slope = jnp.where(h == i, jnp.float32(slopes[g * DA_HEADS_PER_GROUP + i]), slope)
        slope = slope * float(d)
        for variant, offset in enumerate((-DA_RADIUS, 0, -2 * DA_RADIUS)):
            dist = jnp.abs(col - row + offset)
            bias_ref[g, variant] = jnp.where(dist <= DA_RADIUS, -slope * dist.astype(F32), NEG_INF)

    ones = jnp.ones((bk, HEAD_DIM), BF16)

    def block(idx, g, d, q_ref, k_ref, v_ref):
        Ld = k_ref.shape[2]
        per_class = DA_TQ // d // bq
        r = 0 if d == 1 else idx // per_class
        i = idx if d == 1 else idx % per_class
        n0 = t * (DA_TQ // d) + i * bq
        k0 = pl.multiple_of(jnp.clip(n0 - DA_RADIUS, 0, Ld - bk), DA_RADIUS)
        variant = jnp.where(n0 == 0, 1, jnp.where(n0 == Ld - bq, 2, 0))
        q = q_ref[0, r, pl.ds(pl.multiple_of(i * bq, bq), bq), :]
        k = k_ref[0, r, pl.ds(k0, bk), :]
        v = v_ref[0, r, pl.ds(k0, bk), :]
        s = _dot_nt(q, k) + bias_ref[g, variant]
        m = jnp.max(s, axis=-1, keepdims=True)
        p = jnp.exp(s - m).astype(BF16)
        ol = _dot(p, jnp.concatenate([v, ones], axis=1))
        l = ol[:, HEAD_DIM:]
        o = ol[:, :HEAD_DIM] / l
        lse = m + jnp.log(l)
        if d == 1:
            dst = pl.ds(pl.multiple_of(i * bq, bq), bq)
        else:
            dst = pl.ds(i * (bq * d) + r, bq, stride=d)
        og_ref[g, dst, :] = o
        lse_ref[g, dst, :] = lse

    def body(idx, carry):
        for g, (d, q_ref, k_ref, v_ref) in enumerate(groups):
            block(idx, g, d, q_ref, k_ref, v_ref)
        return carry

    lax.fori_loop(0, DA_BLOCKS, body, 0, unroll=4)

    for c in range(DA_TQ // DA_MERGE_ROWS):
        rows = slice(c * DA_MERGE_ROWS, (c + 1) * DA_MERGE_ROWS)
        ls = [lse_ref[g, rows, :] for g in range(len(groups))]
        m = jnp.maximum(jnp.maximum(ls[0], ls[1]), ls[2])
        es = [jnp.exp(x - m) for x in ls]
        inv = 1.0 / (es[0] + es[1] + es[2])
        acc = (es[0] * inv) * og_ref[0, rows, :]
        for g in range(1, len(groups)):
            acc = acc + (es[g] * inv) * og_ref[g, rows, :]
        o_ref[0, rows, :] = acc.astype(BF16)


def _dilated(qs, ks, vs, slopes):
    B, _, L, W = qs[0].shape
    nh = W // HEAD_DIM
    assert L % DA_TQ == 0 and all(L // d >= DA_BK and DA_TQ // d >= DA_BQ for d in DA_DILATIONS)

    def q_spec(d):
        return pl.BlockSpec((1, d, DA_TQ // d, HEAD_DIM), lambda b, h, t: (b, 0, t, h))

    def kv_spec(d):
        return pl.BlockSpec((1, d, L // d, HEAD_DIM), lambda b, h, t: (b, 0, 0, h))

    n_groups = len(DA_DILATIONS)
    return pl.pallas_call(
        functools.partial(_dilated_kernel, slopes=tuple(slopes)),
        grid=(B, nh, L // DA_TQ),
        in_specs=([q_spec(d) for d in DA_DILATIONS] + [kv_spec(d) for d in DA_DILATIONS] * 2),
        out_specs=pl.BlockSpec((1, DA_TQ, HEAD_DIM), lambda b, h, t: (b, t, h)),
        out_shape=jax.ShapeDtypeStruct((B, L, W), BF16),
        scratch_shapes=[pltpu.VMEM((n_groups, DA_TQ, HEAD_DIM), F32),
                        pltpu.VMEM((n_groups, DA_TQ, HEAD_DIM), F32),
                        pltpu.VMEM((n_groups, 3, DA_BQ, DA_BK), F32)],
        compiler_params=_params("arbitrary", "arbitrary", "arbitrary"),
        name="dilated",
    )(*qs, *ks, *vs)


def _mix_kernel(x_ref, ohg_ref, oda_ref, mq_ref, mk_ref, mv_ref, gates_ref,
                whg_ref, wda_ref, wmem_ref, wout_ref, out_ref):
    D = x_ref.shape[-1]

    def body(c, rows):
        mem_parts = []
        for h in range(MEM_HEADS):
            sl = slice(h * HEAD_DIM, (h + 1) * HEAD_DIM)
            s = _dot_nt(mq_ref[rows, sl], mk_ref[0, :, sl])
            p = jnp.exp(s - jnp.max(s, axis=-1, keepdims=True))
            l = jnp.sum(p, axis=-1, keepdims=True)
            mem_parts.append((_dot(p.astype(BF16), mv_ref[0, :, sl]) / l).astype(BF16))
        o_mem = jnp.concatenate(mem_parts, axis=-1)

        merged = (gates_ref[rows, 0:D].astype(F32) * _dot(ohg_ref[rows, :], whg_ref[...])
                  + gates_ref[rows, D:2 * D].astype(F32) * _dot(oda_ref[rows, :], wda_ref[...])
                  + gates_ref[rows, 2 * D:3 * D].astype(F32) * _dot(o_mem, wmem_ref[...]))
        out_ref[rows, :] = x_ref[rows, :] + _dot(merged.astype(BF16), wout_ref[...])

    _for_row_chunks(x_ref.shape[0], body)


def _mix(x2, o_hg, o_da, mem_q, mem_k, mem_v, gates, w_hg, w_da, w_mem, w_out, L):
    T, D = x2.shape
    tm = TOKEN_TILE
    nt = L // tm
    M = mem_k.shape[1]
    const = lambda i: (0, 0)

    def tok(width):
        return pl.BlockSpec((tm, width), lambda i: (i, 0))

    mem_spec = pl.BlockSpec((1, M, mem_k.shape[2]), lambda i: (i // nt, 0, 0))
    return pl.pallas_call(
        _mix_kernel,
        grid=(T // tm,),
        in_specs=[tok(D), tok(o_hg.shape[1]), tok(o_da.shape[1]), tok(mem_q.shape[1]),
                  mem_spec, mem_spec, tok(gates.shape[1]),
                  _resident(w_hg.shape, const), _resident(w_da.shape, const),
                  _resident(w_mem.shape, const), _resident(w_out.shape, const)],
        out_specs=tok(D),
        out_shape=jax.ShapeDtypeStruct((T, D), F32),
        compiler_params=_params("arbitrary"),
        name="mix",
    )(x2, o_hg, o_da, mem_q, mem_k, mem_v, gates, w_hg, w_da, w_mem, w_out)


def _ffn_kernel(x_ref, gain_ref, wa_ref, wb_ref, wo_ref, out_ref):
    def body(c, rows):
        x = x_ref[rows, :]
        h = _rms_rows(x, gain_ref[...]).astype(BF16)
        a = _dot(h, wa_ref[...])
        b = _dot(h, wb_ref[...])
        out_ref[rows, :] = x + _dot((_silu(a) * b).astype(BF16), wo_ref[...])

    _for_row_chunks(x_ref.shape[0], body)


def _ffn(x2, gain, w_in, w_out):
    T, D = x2.shape
    d_ff = w_out.shape[0]
    tm = TOKEN_TILE
    return pl.pallas_call(
        _ffn_kernel,
        grid=(T // tm,),
        in_specs=[
            pl.BlockSpec((tm, D), lambda i: (i, 0)),
            pl.BlockSpec((1, D), lambda i: (0, 0)),
            _resident((D, d_ff), lambda i: (0, 0)),
            _resident((D, d_ff), lambda i: (0, 1)),
            _resident((d_ff, D), lambda i: (0, 0)),
        ],
        out_specs=pl.BlockSpec((tm, D), lambda i: (i, 0)),
        out_shape=jax.ShapeDtypeStruct((T, D), F32),
        compiler_params=_params("arbitrary"),
        name="ffn",
    )(x2, gain, w_in, w_in, w_out)


def kernel(x, mem, norm_mix_gain, norm_mem_gain, w_in, lb_logits_fw, lb_logits_bw, hg_norm_gain,
           da_q_gain, da_k_gain, w_mem_kv, mem_q_gain, mem_k_gain, w_proj_hg, w_proj_da,
           w_proj_mem, w_out, norm_ffn_gain, w_ffn_in, w_ffn_out):
    B, L, D = x.shape
    depth = w_in.shape[0]
    assert depth == 1, "lower-bound tables are evaluated for a single layer"
    T = B * L
    slopes = _alibi_slopes(DA_HEADS)
    hg_cols = 5 * w_proj_hg.shape[1]
    x2 = x.reshape(T, D)
    for l in range(depth):
        mem_k, mem_v = _mem_kv(mem, norm_mem_gain[l][None], w_mem_kv[l].astype(BF16),
                               mem_k_gain[l][None])
        h, hq, lf_fw, k_fw, lf_bw, k_bw, hv, hg = _proj_hgrn(
            x2, norm_mix_gain[l][None], w_in[l, :, :hg_cols].astype(BF16),
            lb_logits_fw.astype(F32), lb_logits_bw.astype(F32))
        (q0, q1, q2, k0, k1, k2, v0, v1, v2, mem_q, gates) = _proj_attn(
            h, w_in[l, :, hg_cols:].astype(BF16), da_q_gain[l][None], da_k_gain[l][None],
            mem_q_gain[l][None], B, L)

        def seq(t):
            return t.reshape(B, L, t.shape[-1])

        o_hg = _hgrn(seq(hq), seq(lf_fw), seq(k_fw), seq(lf_bw), seq(k_bw), seq(hv), seq(hg),
                     hg_norm_gain[l][None])
        o_da = _dilated((q0, q1, q2), (k0, k1, k2), (v0, v1, v2), slopes)

        x2 = _mix(x2, o_hg.reshape(T, -1), o_da.reshape(T, -1), mem_q, mem_k, mem_v, gates,
                  w_proj_hg[l].astype(BF16), w_proj_da[l].astype(BF16),
                  w_proj_mem[l].astype(BF16), w_out[l].astype(BF16), L)
        x2 = _ffn(x2, norm_ffn_gain[l][None], w_ffn_in[l].astype(BF16), w_ffn_out[l].astype(BF16))
    return x2.reshape(B, L, D)
```

```python
import functools

import jax
import jax.numpy as jnp
from jax import lax
from jax.experimental import pallas as pl
from jax.experimental.pallas import tpu as pltpu

RMS_EPS = 1e-6
NEG_INF = -1e30
HEAD_DIM = 128
HG_CHUNK = 128
HG_UNROLL = 4
DA_CONFIGS = ((128, 1), (512, 4), (2048, 16))
DA_DILATIONS = tuple(d for _, d in DA_CONFIGS)
DA_RADIUS = 64
DA_HEADS_PER_GROUP = 4
DA_HEADS = DA_HEADS_PER_GROUP * len(DA_CONFIGS)
MEM_HEADS = 4
GROUP_WIDTH = DA_HEADS_PER_GROUP * HEAD_DIM
ATTN_SCALE = HEAD_DIM ** -0.5

BF16 = jnp.bfloat16
F32 = jnp.float32

VMEM_LIMIT_BYTES = 56 * 1024 * 1024

TOKEN_TILE = 512
ROW_CHUNK = 256


def _alibi_slopes(n):
    return [float(2.0 ** (-8.0 * (i + 1) / n)) for i in range(n)]


def _sigmoid(x):
    return 1.0 / (1.0 + jnp.exp(-x))


def _silu(x):
    return x * _sigmoid(x)


def _rms_rows(x, gain):
    ms = jnp.mean(x * x, axis=-1, keepdims=True)
    return x * lax.rsqrt(ms + RMS_EPS) * gain


def _head_norm(x, gain, scale):
    parts = []
    for h in range(x.shape[-1] // HEAD_DIM):
        parts.append(_rms_rows(x[:, h * HEAD_DIM:(h + 1) * HEAD_DIM], gain) * scale)
    return jnp.concatenate(parts, axis=-1)


def _dot(a, b):
    return jnp.dot(a, b, preferred_element_type=F32)


def _dot_nt(a, b):
    return lax.dot_general(a, b, (((1,), (1,)), ((), ())), preferred_element_type=F32)


def _dot_tn(a, b):
    return lax.dot_general(a, b, (((0,), (0,)), ((), ())), preferred_element_type=F32)


def _params(*semantics):
    return pltpu.CompilerParams(dimension_semantics=semantics,
                                vmem_limit_bytes=VMEM_LIMIT_BYTES)


def _resident(shape, index_map):
    return pl.BlockSpec(shape, index_map, pipeline_mode=pl.Buffered(1))


def _for_row_chunks(n_rows, body):
    def step(c, carry):
        body(c, pl.ds(pl.multiple_of(c * ROW_CHUNK, ROW_CHUNK), ROW_CHUNK))
        return carry
    lax.fori_loop(0, n_rows // ROW_CHUNK, step, 0)


def _mem_kv_kernel(mem_ref, gain_ref, w_ref, kgain_ref, k_ref, v_ref):
    mem_n = _rms_rows(mem_ref[0], gain_ref[...]).astype(BF16)
    kv = _dot(mem_n, w_ref[...])
    width = k_ref.shape[-1]
    k_ref[0] = _head_norm(kv[:, :width], kgain_ref[...], 1.0).astype(BF16)
    v_ref[0] = kv[:, width:].astype(BF16)


def _mem_kv(mem, gain, w_kv, k_gain):
    B, M, D = mem.shape
    width = w_kv.shape[1] // 2
    return pl.pallas_call(
        _mem_kv_kernel,
        grid=(B,),
        in_specs=[
            pl.BlockSpec((1, M, D), lambda b: (b, 0, 0)),
            pl.BlockSpec((1, D), lambda b: (0, 0)),
            pl.BlockSpec((D, 2 * width), lambda b: (0, 0)),
            pl.BlockSpec((1, HEAD_DIM), lambda b: (0, 0)),
        ],
        out_specs=[
            pl.BlockSpec((1, M, width), lambda b: (b, 0, 0)),
            pl.BlockSpec((1, M, width), lambda b: (b, 0, 0)),
        ],
        out_shape=[jax.ShapeDtypeStruct((B, M, width), BF16)] * 2,
        compiler_params=_params("arbitrary"),
        name="mem_kv",
    )(mem, gain, w_kv, k_gain)


def _lower_bound(logits):
    e = jnp.exp(logits - jnp.max(logits, axis=0, keepdims=True))
    return e[0:1] / jnp.sum(e, axis=0, keepdims=True)


def _proj_hgrn_kernel(x_ref, gain_ref, w_ref, lbfw_ref, lbbw_ref,
                      h_ref, hq_ref, lffw_ref, kfw_ref, lfbw_ref, kbw_ref, hv_ref, hg_ref):
    W = hq_ref.shape[1]

    def section(h, s):
        return _dot(h, w_ref[:, s * W:(s + 1) * W])

    row = lax.broadcasted_iota(jnp.int32, (ROW_CHUNK, ROW_CHUNK), 0)
    col = lax.broadcasted_iota(jnp.int32, (ROW_CHUNK, ROW_CHUNK), 1)
    same_chunk = (row // HG_CHUNK) == (col // HG_CHUNK)
    tri_fw = jnp.where(same_chunk & (row >= col), 1.0, 0.0).astype(BF16)
    tri_bw = jnp.where(same_chunk & (row <= col), 1.0, 0.0).astype(BF16)

    def forget(acc, lb_ref, tri, rows, b_ref, k_ref):
        lb = _lower_bound(lb_ref[...])
        f = lb + (1.0 - lb) * _sigmoid(acc)
        k_ref[rows, :] = (1.0 - f).astype(BF16)
        lf = jnp.log(f)
        lf_hi = lf.astype(BF16)
        lf_lo = (lf - lf_hi.astype(F32)).astype(BF16)
        bb = _dot(tri, jnp.concatenate([lf_hi, lf_lo], axis=1))
        b_ref[rows, :] = bb[:, :W] + bb[:, W:]

    def body(c, rows):
        h = _rms_rows(x_ref[rows, :], gain_ref[...]).astype(BF16)
        h_ref[rows, :] = h
        hq_ref[rows, :] = (_silu(section(h, 0)) * ATTN_SCALE).astype(BF16)
        forget(section(h, 1), lbfw_ref, tri_fw, rows, lffw_ref, kfw_ref)
        forget(section(h, 2), lbbw_ref, tri_bw, rows, lfbw_ref, kbw_ref)
        hv_ref[rows, :] = section(h, 3).astype(BF16)
        hg_ref[rows, :] = _silu(section(h, 4)).astype(BF16)

    _for_row_chunks(x_ref.shape[0], body)


def _proj_hgrn(x2, gain, w_hg, lb_fw, lb_bw):
    T, D = x2.shape
    W = w_hg.shape[1] // 5
    tm = TOKEN_TILE
    const = lambda i: (0, 0)
    tok = lambda width: pl.BlockSpec((tm, width), lambda i: (i, 0))
    outs = [(D, BF16), (W, BF16), (W, F32), (W, BF16), (W, F32), (W, BF16), (W, BF16), (W, BF16)]
    return pl.pallas_call(
        _proj_hgrn_kernel,
        grid=(T // tm,),
        in_specs=[tok(D), pl.BlockSpec((1, D), const), _resident(w_hg.shape, const),
                  pl.BlockSpec(lb_fw.shape, const), pl.BlockSpec(lb_bw.shape, const)],
        out_specs=[tok(w) for w, _ in outs],
        out_shape=[jax.ShapeDtypeStruct((T, w), dt) for w, dt in outs],
        compiler_params=_params("arbitrary"),
        name="proj_hgrn",
    )(x2, gain, w_hg, lb_fw, lb_bw)


def _proj_attn_kernel(h_ref, w_ref, qg_ref, kg_ref, mg_ref,
                      q0_ref, q1_ref, q2_ref, k0_ref, k1_ref, k2_ref, v0_ref, v1_ref, v2_ref,
                      mq_ref, gates_ref, scr_ref):
    nh = GROUP_WIDTH // HEAD_DIM
    gates_col0 = w_ref.shape[1] - gates_ref.shape[1]
    D = gates_ref.shape[1] // 3

    def q_norm(t):
        return _rms_rows(t, qg_ref[...]) * ATTN_SCALE

    def k_norm(t):
        return _rms_rows(t, kg_ref[...])

    def body(c, rows):
        h = h_ref[rows, :]

        def class_major(tile, out_ref, fn):
            acc = _dot(h, w_ref[:, tile * GROUP_WIDTH:(tile + 1) * GROUP_WIDTH])
            d = out_ref.shape[1]
            n = ROW_CHUNK // d
            for hh in range(nh):
                cols = slice(hh * HEAD_DIM, (hh + 1) * HEAD_DIM)
                vals = fn(acc[:, cols])
                if d == 1:
                    out_ref[0, 0, rows, cols] = vals.astype(BF16)
                    continue
                scr_ref[tile, hh] = vals
                dst = pl.ds(pl.multiple_of(c * n, n), n)
                for r in range(d):
                    out_ref[0, r, dst, cols] = scr_ref[tile, hh, pl.ds(r, n, stride=d), :].astype(BF16)

        for g, (q_ref, k_ref, v_ref) in enumerate(((q0_ref, k0_ref, v0_ref), (q1_ref, k1_ref, v1_ref),
                                                   (q2_ref, k2_ref, v2_ref))):
            class_major(g, q_ref, q_norm)
            class_major(3 + g, k_ref, k_norm)
            class_major(6 + g, v_ref, lambda t: t)
        mq = _dot(h, w_ref[:, 9 * GROUP_WIDTH:10 * GROUP_WIDTH])
        mq_ref[rows, :] = _head_norm(mq, mg_ref[...], ATTN_SCALE).astype(BF16)
        for s in range(3):
            gate = _dot(h, w_ref[:, gates_col0 + s * D:gates_col0 + (s + 1) * D])
            gates_ref[rows, s * D:(s + 1) * D] = _sigmoid(gate).astype(BF16)

    _for_row_chunks(h_ref.shape[0], body)


def _proj_attn(h, w_at, q_gain, k_gain, mq_gain, B, L):
    T, D = h.shape
    tm = TOKEN_TILE
    nt = L // tm
    n_groups = len(DA_DILATIONS)
    gates_width = w_at.shape[1] - (3 * n_groups + 1) * GROUP_WIDTH
    const = lambda i: (0, 0)

    def cls(d):
        return (pl.BlockSpec((1, d, tm // d, GROUP_WIDTH), lambda i: (i // nt, 0, i % nt, 0)),
                jax.ShapeDtypeStruct((B, d, L // d, GROUP_WIDTH), BF16))

    outs = [cls(d) for _ in range(3) for d in DA_DILATIONS]
    for width in (GROUP_WIDTH, gates_width):
        outs.append((pl.BlockSpec((tm, width), lambda i: (i, 0)),
                     jax.ShapeDtypeStruct((T, width), BF16)))
    gain_spec = pl.BlockSpec((1, HEAD_DIM), const)
    return pl.pallas_call(
        _proj_attn_kernel,
        grid=(T // tm,),
        in_specs=[pl.BlockSpec((tm, D), lambda i: (i, 0)), _resident(w_at.shape, const),
                  gain_spec, gain_spec, gain_spec],
        out_specs=[o[0] for o in outs],
        out_shape=[o[1] for o in outs],
        scratch_shapes=[pltpu.VMEM((3 * n_groups, GROUP_WIDTH // HEAD_DIM, ROW_CHUNK, HEAD_DIM), F32)],
        compiler_params=_params("arbitrary"),
        name="proj_attn",
    )(h, w_at, q_gain, k_gain, mq_gain)


HG_PAIR = 2 * HEAD_DIM


def _head_diag(m):
    return jnp.concatenate([m[:HEAD_DIM, :HEAD_DIM], m[HEAD_DIM:, HEAD_DIM:]], axis=1)


def _hgrn_kernel(q_ref, bfw_ref, kfw_ref, bbw_ref, kbw_ref, v_ref, g_ref, gain_ref,
                 o_ref, ofw_ref, obw_ref, sfw_ref, sbw_ref):
    L = q_ref.shape[1]
    C = HG_CHUNK
    n_chunks = L // C
    row = lax.broadcasted_iota(jnp.int32, (C, HG_PAIR), 0)
    col = lax.broadcasted_iota(jnp.int32, (C, HG_PAIR), 1)
    first_head = col < HEAD_DIM
    src = jnp.where(first_head, col, col - HEAD_DIM)
    lower = row >= src
    upper = row <= src

    sfw_ref[...] = jnp.zeros_like(sfw_ref)
    sbw_ref[...] = jnp.zeros_like(sbw_ref)

    def chunk(c, b_ref, k_ref, s, out_ref, causal, mid_row, end_row):
        rows = pl.ds(pl.multiple_of(c * C, C), C)
        b = b_ref[0, rows, :]
        b_mid = b[mid_row:mid_row + 1, :]
        b_end = b[end_row:end_row + 1, :]
        q_in = q_ref[0, rows, :].astype(F32) * jnp.exp(b - b_mid)
        k_in = k_ref[0, rows, :].astype(F32) * jnp.exp(b_mid - b)
        q_st = (q_in * jnp.exp(b_mid)).astype(BF16)
        k_st = (k_in * jnp.exp(b_end - b_mid)).astype(BF16)
        v = v_ref[0, rows, :]
        k_bf = k_in.astype(BF16)
        zero = jnp.zeros_like(k_bf)
        k_diag = jnp.concatenate([jnp.where(first_head, k_bf, zero),
                                  jnp.where(first_head, zero, k_bf)], axis=0)
        a = _dot_nt(q_in.astype(BF16), k_diag)
        a = jnp.where(causal, a, 0.0).astype(BF16)
        lhs = jnp.concatenate(
            [jnp.concatenate([q_st[:, :HEAD_DIM], a[:, :HEAD_DIM]], axis=1),
             jnp.concatenate([q_st[:, HEAD_DIM:], a[:, HEAD_DIM:]], axis=1)], axis=0)
        rhs = jnp.concatenate([s.astype(BF16), v], axis=0)
        out_ref[rows, :] = _head_diag(_dot(lhs, rhs))
        d = jnp.exp(b_end)
        d_col = jnp.concatenate(
            [jnp.broadcast_to(d[:, :HEAD_DIM], (HEAD_DIM, HEAD_DIM)).T,
             jnp.broadcast_to(d[:, HEAD_DIM:], (HEAD_DIM, HEAD_DIM)).T], axis=1)
        return s * d_col + _head_diag(_dot_tn(k_st, v))

    def body(n, carry):
        s_fw = sfw_ref[...]
        s_bw = sbw_ref[...]
        for u in range(HG_UNROLL):
            c = n * HG_UNROLL + u
            s_fw = chunk(c, bfw_ref, kfw_ref, s_fw, ofw_ref, lower, C // 2 - 1, C - 1)
            s_bw = chunk(n_chunks - 1 - c, bbw_ref, kbw_ref, s_bw, obw_ref, upper, C // 2, 0)
        sfw_ref[...] = s_fw
        sbw_ref[...] = s_bw
        return carry

    lax.fori_loop(0, n_chunks // HG_UNROLL, body, 0)

    def finish(c, rows):
        o = ofw_ref[rows, :] + obw_ref[rows, :]
        o_ref[0, rows, :] = (_head_norm(o, gain_ref[...], 1.0)
                             * g_ref[0, rows, :].astype(F32)).astype(BF16)

    _for_row_chunks(L, finish)


def _hgrn(hq, b_fw, k_fw, b_bw, k_bw, hv, hg, gain):
    B, L, W = hq.shape
    spec = pl.BlockSpec((1, L, HG_PAIR), lambda b, h: (b, 0, h))
    return pl.pallas_call(
        _hgrn_kernel,
        grid=(B, W // HG_PAIR),
        in_specs=[spec] * 7 + [pl.BlockSpec((1, HEAD_DIM), lambda b, h: (0, 0))],
        out_specs=spec,
        out_shape=jax.ShapeDtypeStruct((B, L, W), BF16),
        scratch_shapes=[pltpu.VMEM((L, HG_PAIR), F32), pltpu.VMEM((L, HG_PAIR), F32),
                        pltpu.VMEM((HEAD_DIM, HG_PAIR), F32), pltpu.VMEM((HEAD_DIM, HG_PAIR), F32)],
        compiler_params=_params("arbitrary", "arbitrary"),
        name="hgrn2",
    )(hq, b_fw, k_fw, b_bw, k_bw, hv, hg, gain)


DA_TQ = 2048
DA_BQ = 128
DA_BK = DA_BQ + 2 * DA_RADIUS
DA_BLOCKS = DA_TQ // DA_BQ
DA_MERGE_ROWS = 256


def _dilated_kernel(q0_ref, q1_ref, q2_ref, k0_ref, k1_ref, k2_ref, v0_ref, v1_ref, v2_ref,
                    o_ref, og_ref, lse_ref, bias_ref, *, slopes):
    h = pl.program_id(1)
    t = pl.program_id(2)
    bq, bk = DA_BQ, DA_BK
    groups = tuple(zip(DA_DILATIONS, (q0_ref, q1_ref, q2_ref), (k0_ref, k1_ref, k2_ref),
                       (v0_ref, v1_ref, v2_ref)))

    row = lax.broadcasted_iota(jnp.int32, (bq, bk), 0)
    col = lax.broadcasted_iota(jnp.int32, (bq, bk), 1)
    for g, d in enumerate(DA_DILATIONS):
        slope = jnp.float32(slopes[g * DA_HEADS_PER_GROUP])
        for i in range(1, DA_HEADS_PER_GROUP):
            slope = jnp.where(h == i, jnp.float32(slopes[g * DA_HEADS_PER_GROUP + i]), slope)
        slope = slope * float(d)
        for variant, offset in enumerate((-DA_RADIUS, 0, -2 * DA_RADIUS)):
            dist = jnp.abs(col - row + offset)
            bias_ref[g, variant] = jnp.where(dist <= DA_RADIUS, -slope * dist.astype(F32), NEG_INF)

    ones = jnp.ones((bk, HEAD_DIM), BF16)

    def block(idx, g, d, q_ref, k_ref, v_ref):
        Ld = k_ref.shape[2]
        per_class = DA_TQ // d // bq
        r = 0 if d == 1 else idx // per_class
        i = idx if d == 1 else idx % per_class
        n0 = t * (DA_TQ // d) + i * bq
        k0 = pl.multiple_of(jnp.clip(n0 - DA_RADIUS, 0, Ld - bk), DA_RADIUS)
        variant = jnp.where(n0 == 0, 1, jnp.where(n0 == Ld - bq, 2, 0))
        q = q_ref[0, r, pl.ds(pl.multiple_of(i * bq, bq), bq), :]
        k = k_ref[0, r, pl.ds(k0, bk), :]
        v = v_ref[0, r, pl.ds(k0, bk), :]
        s = _dot_nt(q, k) + bias_ref[g, variant]
        m = jnp.max(s, axis=-1, keepdims=True)
        p = jnp.exp(s - m).astype(BF16)
        ol = _dot(p, jnp.concatenate([v, ones], axis=1))
        l = ol[:, HEAD_DIM:]
        o = ol[:, :HEAD_DIM] / l
        lse = m + jnp.log(l)
        if d == 1:
            dst = pl.ds(pl.multiple_of(i * bq, bq), bq)
        else:
            dst = pl.ds(i * (bq * d) + r, bq, stride=d)
        og_ref[g, dst, :] = o
        lse_ref[g, dst, :] = lse

    def body(idx, carry):
        for g, (d, q_ref, k_ref, v_ref) in enumerate(groups):
            block(idx, g, d, q_ref, k_ref, v_ref)
        return carry

    lax.fori_loop(0, DA_BLOCKS, body, 0, unroll=4)

    for c in range(DA_TQ // DA_MERGE_ROWS):
        rows = slice(c * DA_MERGE_ROWS, (c + 1) * DA_MERGE_ROWS)
        ls = [lse_ref[g, rows, :] for g in range(len(groups))]
        m = jnp.maximum(jnp.maximum(ls[0], ls[1]), ls[2])
        es = [jnp.exp(x - m) for x in ls]
        inv = 1.0 / (es[0] + es[1] + es[2])
        acc = (es[0] * inv) * og_ref[0, rows, :]
        for g in range(1, len(groups)):
            acc = acc + (es[g] * inv) * og_ref[g, rows, :]
        o_ref[0, rows, :] = acc.astype(BF16)


def _dilated(qs, ks, vs, slopes):
    B, _, L, W = qs[0].shape
    nh = W // HEAD_DIM
    assert L % DA_TQ == 0 and all(L // d >= DA_BK and DA_TQ // d >= DA_BQ for d in DA_DILATIONS)

    def q_spec(d):
        return pl.BlockSpec((1, d, DA_TQ // d, HEAD_DIM), lambda b, h, t: (b, 0, t, h))

    def kv_spec(d):
        return pl.BlockSpec((1, d, L // d, HEAD_DIM), lambda b, h, t: (b, 0, 0, h))

    n_groups = len(DA_DILATIONS)
    return pl.pallas_call(
        functools.partial(_dilated_kernel, slopes=tuple(slopes)),
        grid=(B, nh, L // DA_TQ),
        in_specs=([q_spec(d) for d in DA_DILATIONS] + [kv_spec(d) for d in DA_DILATIONS] * 2),
        out_specs=pl.BlockSpec((1, DA_TQ, HEAD_DIM), lambda b, h, t: (b, t, h)),
        out_shape=jax.ShapeDtypeStruct((B, L, W), BF16),
        scratch_shapes=[pltpu.VMEM((n_groups, DA_TQ, HEAD_DIM), F32),
                        pltpu.VMEM((n_groups, DA_TQ, HEAD_DIM), F32),
                        pltpu.VMEM((n_groups, 3, DA_BQ, DA_BK), F32)],
        compiler_params=_params("arbitrary", "arbitrary", "arbitrary"),
        name="dilated",
    )(*qs, *ks, *vs)


def _mix_kernel(x_ref, ohg_ref, oda_ref, mq_ref, mk_ref, mv_ref, gates_ref,
                whg_ref, wda_ref, wmem_ref, wout_ref, out_ref):
    D = x_ref.shape[-1]

    def body(c, rows):
        mem_parts = []
        for h in range(MEM_HEADS):
            sl = slice(h * HEAD_DIM, (h + 1) * HEAD_DIM)
            s = _dot_nt(mq_ref[rows, sl], mk_ref[0, :, sl])
            p = jnp.exp(s - jnp.max(s, axis=-1, keepdims=True))
            l = jnp.sum(p, axis=-1, keepdims=True)
            mem_parts.append((_dot(p.astype(BF16), mv_ref[0, :, sl]) / l).astype(BF16))
        o_mem = jnp.concatenate(mem_parts, axis=-1)

        merged = (gates_ref[rows, 0:D].astype(F32) * _dot(ohg_ref[rows, :], whg_ref[...])
                  + gates_ref[rows, D:2 * D].astype(F32) * _dot(oda_ref[rows, :], wda_ref[...])
                  + gates_ref[rows, 2 * D:3 * D].astype(F32) * _dot(o_mem, wmem_ref[...]))
        out_ref[rows, :] = x_ref[rows, :] + _dot(merged.astype(BF16), wout_ref[...])

    _for_row_chunks(x_ref.shape[0], body)


def _mix(x2, o_hg, o_da, mem_q, mem_k, mem_v, gates, w_hg, w_da, w_mem, w_out, L):
    T, D = x2.shape
    tm = TOKEN_TILE
    nt = L // tm
    M = mem_k.shape[1]
    const = lambda i: (0, 0)

    def tok(width):
        return pl.BlockSpec((tm, width), lambda i: (i, 0))

    mem_spec = pl.BlockSpec((1, M, mem_k.shape[2]), lambda i: (i // nt, 0, 0))
    return pl.pallas_call(
        _mix_kernel,
        grid=(T // tm,),
        in_specs=[tok(D), tok(o_hg.shape[1]), tok(o_da.shape[1]), tok(mem_q.shape[1]),
                  mem_spec, mem_spec, tok(gates.shape[1]),
                  _resident(w_hg.shape, const), _resident(w_da.shape, const),
                  _resident(w_mem.shape, const), _resident(w_out.shape, const)],
        out_specs=tok(D),
        out_shape=jax.ShapeDtypeStruct((T, D), F32),
        compiler_params=_params("arbitrary"),
        name="mix",
    )(x2, o_hg, o_da, mem_q, mem_k, mem_v, gates, w_hg, w_da, w_mem, w_out)


def _ffn_kernel(x_ref, gain_ref, wa_ref, wb_ref, wo_ref, out_ref):
    def body(c, rows):
        x = x_ref[rows, :]
        h = _rms_rows(x, gain_ref[...]).astype(BF16)
        a = _dot(h, wa_ref[...])
        b = _dot(h, wb_ref[...])
        out_ref[rows, :] = x + _dot((_silu(a) * b).astype(BF16), wo_ref[...])

    _for_row_chunks(x_ref.shape[0], body)


def _ffn(x2, gain, w_in, w_out):
    T, D = x2.shape
    d_ff = w_out.shape[0]
    tm = TOKEN_TILE
    return pl.pallas_call(
        _ffn_kernel,
        grid=(T // tm,),
        in_specs=[
            pl.BlockSpec((tm, D), lambda i: (i, 0)),
            pl.BlockSpec((1, D), lambda i: (0, 0)),
            _resident((D, d_ff), lambda i: (0, 0)),
            _resident((D, d_ff), lambda i: (0, 1)),
            _resident((d_ff, D), lambda i: (0, 0)),
        ],
        out_specs=pl.BlockSpec((tm, D), lambda i: (i, 0)),
        out_shape=jax.ShapeDtypeStruct((T, D), F32),
        compiler_params=_params("arbitrary"),
        name="ffn",
    )(x2, gain, w_in, w_in, w_out)


def kernel(x, mem, norm_mix_gain, norm_mem_gain, w_in, lb_logits_fw, lb_logits_bw, hg_norm_gain,
           da_q_gain, da_k_gain, w_mem_kv, mem_q_gain, mem_k_gain, w_proj_hg, w_proj_da,
           w_proj_mem, w_out, norm_ffn_gain, w_ffn_in, w_ffn_out):
    B, L, D = x.shape
    depth = w_in.shape[0]
    assert depth == 1, "lower-bound tables are evaluated for a single layer"
    T = B * L
    slopes = _alibi_slopes(DA_HEADS)
    hg_cols = 5 * w_proj_hg.shape[1]
    x2 = x.reshape(T, D)
    for l in range(depth):
        mem_k, mem_v = _mem_kv(mem, norm_mem_gain[l][None], w_mem_kv[l].astype(BF16),
                               mem_k_gain[l][None])
        h, hq, lf_fw, k_fw, lf_bw, k_bw, hv, hg = _proj_hgrn(
            x2, norm_mix_gain[l][None], w_in[l, :, :hg_cols].astype(BF16),
            lb_logits_fw.astype(F32), lb_logits_bw.astype(F32))
        (q0, q1, q2, k0, k1, k2, v0, v1, v2, mem_q, gates) = _proj_attn(
            h, w_in[l, :, hg_cols:].astype(BF16), da_q_gain[l][None], da_k_gain[l][None],
            mem_q_gain[l][None], B, L)

        def seq(t):
            return t.reshape(B, L, t.shape[-1])

        o_hg = _hgrn(seq(hq), seq(lf_fw), seq(k_fw), seq(lf_bw), seq(k_bw), seq(hv), seq(hg),
                     hg_norm_gain[l][None])
        o_da = _dilated((q0, q1, q2), (k0, k1, k2), (v0, v1, v2), slopes)

        x2 = _mix(x2, o_hg.reshape(T, -1), o_da.reshape(T, -1), mem_q, mem_k, mem_v, gates,
                  w_proj_hg[l].astype(BF16), w_proj_da[l].astype(BF16),
                  w_proj_mem[l].astype(BF16), w_out[l].astype(BF16), L)
        x2 = _ffn(x2, norm_ffn_gain[l][None], w_ffn_in[l].astype(BF16), w_ffn_out[l].astype(BF16))
    return x2.reshape(B, L, D)
```

```python
import functools

import jax
import jax.numpy as jnp
from jax import lax
from jax.experimental import pallas as pl
from jax.experimental.pallas import tpu as pltpu

RMS_EPS = 1e-6
NEG_INF = -1e30
HEAD_DIM = 128
HG_CHUNK = 128
HG_UNROLL = 4
DA_CONFIGS = ((128, 1), (512, 4), (2048, 16))
DA_DILATIONS = tuple(d for _, d in DA_CONFIGS)
DA_RADIUS = 64
DA_HEADS_PER_GROUP = 4
DA_HEADS = DA_HEADS_PER_GROUP * len(DA_CONFIGS)
MEM_HEADS = 4
GROUP_WIDTH = DA_HEADS_PER_GROUP * HEAD_DIM
ATTN_SCALE = HEAD_DIM ** -0.5

BF16 = jnp.bfloat16
F32 = jnp.float32

VMEM_LIMIT_BYTES = 56 * 1024 * 1024

TOKEN_TILE = 512
ROW_CHUNK = 256
MXU_TILE = 256
MXU_HEADS = MXU_TILE // HEAD_DIM


def _alibi_slopes(n):
    return [float(2.0 ** (-8.0 * (i + 1) / n)) for i in range(n)]


def _sigmoid(x):
    return 1.0 / (1.0 + jnp.exp(-x))


def _silu(x):
    return x * _sigmoid(x)


def _rms_rows(x, gain):
    ms = jnp.mean(x * x, axis=-1, keepdims=True)
    return x * lax.rsqrt(ms + RMS_EPS) * gain


def _head_norm(x, gain, scale):
    parts = []
    for h in range(x.shape[-1] // HEAD_DIM):
        parts.append(_rms_rows(x[:, h * HEAD_DIM:(h + 1) * HEAD_DIM], gain) * scale)
    return jnp.concatenate(parts, axis=-1)


def _dot(a, b):
    return jnp.dot(a, b, preferred_element_type=F32)


def _dot_nt(a, b):
    return lax.dot_general(a, b, (((1,), (1,)), ((), ())), preferred_element_type=F32)


def _dot_tn(a, b):
    return lax.dot_general(a, b, (((0,), (0,)), ((), ())), preferred_element_type=F32)


def _params(*semantics):
    return pltpu.CompilerParams(dimension_semantics=semantics,
                                vmem_limit_bytes=VMEM_LIMIT_BYTES)


def _resident(shape, index_map):
    return pl.BlockSpec(shape, index_map, pipeline_mode=pl.Buffered(1))


def _for_row_chunks(n_rows, body):
    def step(c, carry):
        body(c, pl.ds(pl.multiple_of(c * ROW_CHUNK, ROW_CHUNK), ROW_CHUNK))
        return carry
    lax.fori_loop(0, n_rows // ROW_CHUNK, step, 0, unroll=True)


def _mem_kv_kernel(mem_ref, gain_ref, w_ref, kgain_ref, k_ref, v_ref):
    mem_n = _rms_rows(mem_ref[0], gain_ref[...]).astype(BF16)
    kv = _dot(mem_n, w_ref[...])
    width = k_ref.shape[-1]
    k_ref[0] = _head_norm(kv[:, :width], kgain_ref[...], 1.0).astype(BF16)
    v_ref[0] = kv[:, width:].astype(BF16)


def _mem_kv(mem, gain, w_kv, k_gain):
    B, M, D = mem.shape
    width = w_kv.shape[1] // 2
    return pl.pallas_call(
        _mem_kv_kernel,
        grid=(B,),
        in_specs=[
            pl.BlockSpec((1, M, D), lambda b: (b, 0, 0)),
            pl.BlockSpec((1, D), lambda b: (0, 0)),
            pl.BlockSpec((D, 2 * width), lambda b: (0, 0)),
            pl.BlockSpec((1, HEAD_DIM), lambda b: (0, 0)),
        ],
        out_specs=[
            pl.BlockSpec((1, M, width), lambda b: (b, 0, 0)),
            pl.BlockSpec((1, M, width), lambda b: (b, 0, 0)),
        ],
        out_shape=[jax.ShapeDtypeStruct((B, M, width), BF16)] * 2,
        compiler_params=_params("arbitrary"),
        name="mem_kv",
    )(mem, gain, w_kv, k_gain)


def _lower_bound(logits):
    e = jnp.exp(logits - jnp.max(logits, axis=0, keepdims=True))
    return e[0:1] / jnp.sum(e, axis=0, keepdims=True)


def _proj_hgrn_kernel(x_ref, gain_ref, w_ref, lbfw_ref, lbbw_ref,
                      h_ref, hq_ref, lffw_ref, kfw_ref, lfbw_ref, kbw_ref, hv_ref, hg_ref):
    W = hq_ref.shape[1]

    def section(rows, s):
        return _dot(h_ref[rows, :], w_ref[:, s * W:(s + 1) * W])

    row = lax.broadcasted_iota(jnp.int32, (ROW_CHUNK, ROW_CHUNK), 0)
    col = lax.broadcasted_iota(jnp.int32, (ROW_CHUNK, ROW_CHUNK), 1)
    same_chunk = (row // HG_CHUNK) == (col // HG_CHUNK)
    tri_fw = jnp.where(same_chunk & (row >= col), 1.0, 0.0).astype(BF16)
    tri_bw = jnp.where(same_chunk & (row <= col), 1.0, 0.0).astype(BF16)

    def forget(acc, lb_ref, tri, rows, b_ref, k_ref):
        lb = _lower_bound(lb_ref[...])
        f = lb + (1.0 - lb) * _sigmoid(acc)
        k_ref[rows, :] = (1.0 - f).astype(BF16)
        lf = jnp.log(f)
        lf_hi = lf.astype(BF16)
        lf_lo = (lf - lf_hi.astype(F32)).astype(BF16)
        bb = _dot(tri, jnp.concatenate([lf_hi, lf_lo], axis=1))
        b_ref[rows, :] = bb[:, :W] + bb[:, W:]

    def body(c, rows):
        h_ref[rows, :] = _rms_rows(x_ref[rows, :], gain_ref[...]).astype(BF16)
        hq_ref[rows, :] = (_silu(section(rows, 0)) * ATTN_SCALE).astype(BF16)
        forget(section(rows, 1), lbfw_ref, tri_fw, rows, lffw_ref, kfw_ref)
        forget(section(rows, 2), lbbw_ref, tri_bw, rows, lfbw_ref, kbw_ref)
        hv_ref[rows, :] = section(rows, 3).astype(BF16)
        hg_ref[rows, :] = _silu(section(rows, 4)).astype(BF16)

    _for_row_chunks(x_ref.shape[0], body)


def _proj_hgrn(x2, gain, w_hg, lb_fw, lb_bw):
    T, D = x2.shape
    W = w_hg.shape[1] // 5
    tm = TOKEN_TILE
    const = lambda i: (0, 0)
    tok = lambda width: pl.BlockSpec((tm, width), lambda i: (i, 0))
    outs = [(D, BF16), (W, BF16), (W, F32), (W, BF16), (W, F32), (W, BF16), (W, BF16), (W, BF16)]
    return pl.pallas_call(
        _proj_hgrn_kernel,
        grid=(T // tm,),
        in_specs=[tok(D), pl.BlockSpec((1, D), const), _resident(w_hg.shape, const),
                  pl.BlockSpec(lb_fw.shape, const), pl.BlockSpec(lb_bw.shape, const)],
        out_specs=[tok(w) for w, _ in outs],
        out_shape=[jax.ShapeDtypeStruct((T, w), dt) for w, dt in outs],
        compiler_params=_params("arbitrary"),
        name="proj_hgrn",
    )(x2, gain, w_hg, lb_fw, lb_bw)


def _proj_attn_kernel(h_ref, w_ref, qg_ref, kg_ref, mg_ref,
                      q0_ref, q1_ref, q2_ref, k0_ref, k1_ref, k2_ref, v0_ref, v1_ref, v2_ref,
                      mq_ref, gates_ref, scr_ref):
    nh = GROUP_WIDTH // HEAD_DIM
    gates_col0 = w_ref.shape[1] - gates_ref.shape[1]
    D = gates_ref.shape[1] // 3

    def q_norm(t):
        return _rms_rows(t, qg_ref[...]) * ATTN_SCALE

    def k_norm(t):
        return _rms_rows(t, kg_ref[...])

    def body(c, rows):
        def project(col0, width):
            return _dot(h_ref[rows, :], w_ref[:, col0:col0 + width])

        def class_major(tile, out_ref, fn):
            d = out_ref.shape[1]
            n = ROW_CHUNK // d
            for hh in range(nh):
                if hh % MXU_HEADS == 0:
                    acc = project(tile * GROUP_WIDTH + hh * HEAD_DIM, MXU_TILE)
                lanes = slice((hh % MXU_HEADS) * HEAD_DIM, (hh % MXU_HEADS + 1) * HEAD_DIM)
                cols = slice(hh * HEAD_DIM, (hh + 1) * HEAD_DIM)
                vals = fn(acc[:, lanes])
                if d == 1:
                    out_ref[0, 0, rows, cols] = vals.astype(BF16)
                    continue
                scr_ref[tile, hh] = vals
                dst = pl.ds(pl.multiple_of(c * n, n), n)
                for r in range(d):
                    out_ref[0, r, dst, cols] = scr_ref[tile, hh, pl.ds(r, n, stride=d), :].astype(BF16)

        for g, (q_ref, k_ref, v_ref) in enumerate(((q0_ref, k0_ref, v0_ref), (q1_ref, k1_ref, v1_ref),
                                                   (q2_ref, k2_ref, v2_ref))):
            class_major(g, q_ref, q_norm)
            class_major(3 + g, k_ref, k_norm)
            class_major(6 + g, v_ref, lambda t: t)
        for j in range(GROUP_WIDTH // MXU_TILE):
            cols = slice(j * MXU_TILE, (j + 1) * MXU_TILE)
            mq = project(9 * GROUP_WIDTH + j * MXU_TILE, MXU_TILE)
            mq_ref[rows, cols] = _head_norm(mq, mg_ref[...], ATTN_SCALE).astype(BF16)
        for j in range(3 * D // MXU_TILE):
            cols = slice(j * MXU_TILE, (j + 1) * MXU_TILE)
            gate = project(gates_col0 + j * MXU_TILE, MXU_TILE)
            gates_ref[rows, cols] = _sigmoid(gate).astype(BF16)

    _for_row_chunks(h_ref.shape[0], body)


def _proj_attn(h, w_at, q_gain, k_gain, mq_gain, B, L):
    T, D = h.shape
    tm = TOKEN_TILE
    nt = L // tm
    n_groups = len(DA_DILATIONS)
    gates_width = w_at.shape[1] - (3 * n_groups + 1) * GROUP_WIDTH
    const = lambda i: (0, 0)

    def cls(d):
        return (pl.BlockSpec((1, d, tm // d, GROUP_WIDTH), lambda i: (i // nt, 0, i % nt, 0)),
                jax.ShapeDtypeStruct((B, d, L // d, GROUP_WIDTH), BF16))

    outs = [cls(d) for _ in range(3) for d in DA_DILATIONS]
    for width in (GROUP_WIDTH, gates_width):
        outs.append((pl.BlockSpec((tm, width), lambda i: (i, 0)),
                     jax.ShapeDtypeStruct((T, width), BF16)))
    gain_spec = pl.BlockSpec((1, HEAD_DIM), const)
    return pl.pallas_call(
        _proj_attn_kernel,
        grid=(T // tm,),
        in_specs=[pl.BlockSpec((tm, D), lambda i: (i, 0)), _resident(w_at.shape, const),
                  gain_spec, gain_spec, gain_spec],
        out_specs=[o[0] for o in outs],
        out_shape=[o[1] for o in outs],
        scratch_shapes=[pltpu.VMEM((3 * n_groups, GROUP_WIDTH // HEAD_DIM, ROW_CHUNK, HEAD_DIM), F32)],
        compiler_params=_params("arbitrary"),
        name="proj_attn",
    )(h, w_at, q_gain, k_gain, mq_gain)


HG_PAIR = 2 * HEAD_DIM


def _head_diag(m):
    return jnp.concatenate([m[:HEAD_DIM, :HEAD_DIM], m[HEAD_DIM:, HEAD_DIM:]], axis=1)


def _hgrn_kernel(q_ref, bfw_ref, kfw_ref, bbw_ref, kbw_ref, v_ref, g_ref, gain_ref,
                 o_ref, acc_ref, sfw_ref, sbw_ref):
    L = q_ref.shape[1]
    C = HG_CHUNK
    n_chunks = L // C
    row = lax.broadcasted_iota(jnp.int32, (C, HG_PAIR), 0)
    col = lax.broadcasted_iota(jnp.int32, (C, HG_PAIR), 1)
    first_head = col < HEAD_DIM
    src = jnp.where(first_head, col, col - HEAD_DIM)
    lower = row >= src
    upper = row <= src

    sfw_ref[...] = jnp.zeros_like(sfw_ref)
    sbw_ref[...] = jnp.zeros_like(sbw_ref)

    def emit(rows, o, second):
        if not second:
            acc_ref[rows, :] = o
            return
        o = o + acc_ref[rows, :]
        o_ref[0, rows, :] = (_head_norm(o, gain_ref[...], 1.0)
                             * g_ref[0, rows, :].astype(F32)).astype(BF16)

    def chunk(c, b_ref, k_ref, s, second, causal, mid_row, end_row):
        rows = pl.ds(pl.multiple_of(c * C, C), C)
        b = b_ref[0, rows, :]
        b_mid = b[mid_row:mid_row + 1, :]
        b_end = b[end_row:end_row + 1, :]
        q_in = q_ref[0, rows, :].astype(F32) * jnp.exp(b - b_mid)
        k_in = k_ref[0, rows, :].astype(F32) * jnp.exp(b_mid - b)
        q_st = (q_in * jnp.exp(b_mid)).astype(BF16)
        k_st = (k_in * jnp.exp(b_end - b_mid)).astype(BF16)
        v = v_ref[0, rows, :]
        k_bf = k_in.astype(BF16)
        zero = jnp.zeros_like(k_bf)
        k_diag = jnp.concatenate([jnp.where(first_head, k_bf, zero),
                                  jnp.where(first_head, zero, k_bf)], axis=0)
        a = _dot_nt(q_in.astype(BF16), k_diag)
        a = jnp.where(causal, a, 0.0).astype(BF16)
        lhs = jnp.concatenate(
            [jnp.concatenate([q_st[:, :HEAD_DIM], a[:, :HEAD_DIM]], axis=1),
             jnp.concatenate([q_st[:, HEAD_DIM:], a[:, HEAD_DIM:]], axis=1)], axis=0)
        rhs = jnp.concatenate([s.astype(BF16), v], axis=0)
        emit(rows, _head_diag(_dot(lhs, rhs)), second)
        d = jnp.exp(b_end)
        d_col = jnp.concatenate(
            [jnp.broadcast_to(d[:, :HEAD_DIM], (HEAD_DIM, HEAD_DIM)).T,
             jnp.broadcast_to(d[:, HEAD_DIM:], (HEAD_DIM, HEAD_DIM)).T], axis=1)
        return s * d_col + _head_diag(_dot_tn(k_st, v))

    def body(n, carry, second):
        s_fw = sfw_ref[...]
        s_bw = sbw_ref[...]
        for u in range(HG_UNROLL):
            c = n * HG_UNROLL + u
            s_fw = chunk(c, bfw_ref, kfw_ref, s_fw, second, lower, C // 2 - 1, C - 1)
            s_bw = chunk(n_chunks - 1 - c, bbw_ref, kbw_ref, s_bw, second, upper, C // 2, 0)
        sfw_ref[...] = s_fw
        sbw_ref[...] = s_bw
        return carry

    trips = n_chunks // HG_UNROLL
    lax.fori_loop(0, trips // 2, functools.partial(body, second=False), 0)
    lax.fori_loop(trips // 2, trips, functools.partial(body, second=True), 0)


def _hgrn(hq, b_fw, k_fw, b_bw, k_bw, hv, hg, gain):
    B, L, W = hq.shape
    spec = pl.BlockSpec((1, L, HG_PAIR), lambda b, h: (b, 0, h))
    return pl.pallas_call(
        _hgrn_kernel,
        grid=(B, W // HG_PAIR),
        in_specs=[spec] * 7 + [pl.BlockSpec((1, HEAD_DIM), lambda b, h: (0, 0))],
        out_specs=spec,
        out_shape=jax.ShapeDtypeStruct((B, L, W), BF16),
        scratch_shapes=[pltpu.VMEM((L, HG_PAIR), F32),
                        pltpu.VMEM((HEAD_DIM, HG_PAIR), F32), pltpu.VMEM((HEAD_DIM, HG_PAIR), F32)],
        compiler_params=_params("arbitrary", "arbitrary"),
        name="hgrn2",
    )(hq, b_fw, k_fw, b_bw, k_bw, hv, hg, gain)


DA_TQ = 2048
DA_BQ = 128
DA_BK = DA_BQ + 2 * DA_RADIUS
DA_BLOCKS = DA_TQ // DA_BQ
DA_MERGE_ROWS = 256


def _dilated_kernel(q0_ref, q1_ref, q2_ref, k0_ref, k1_ref, k2_ref, v0_ref, v1_ref, v2_ref,
                    o_ref, og_ref, l_ref, m_ref, bias_ref, *, slopes):
    h = pl.program_id(1)
    t = pl.program_id(2)
    bq, bk = DA_BQ, DA_BK
    groups = tuple(zip(DA_DILATIONS, (q0_ref, q1_ref, q2_ref), (k0_ref, k1_ref, k2_ref),
                       (v0_ref, v1_ref, v2_ref)))

    row = lax.broadcasted_iota(jnp.int32, (bq, bk), 0)
    col = lax.broadcasted_iota(jnp.int32, (bq, bk), 1)
    for g, d in enumerate(DA_DILATIONS):
        slope = jnp.float32(slopes[g * DA_HEADS_PER_GROUP])
        for i in range(1, DA_HEADS_PER_GROUP):
            slope = jnp.where(h == i, jnp.float32(slopes[g * DA_HEADS_PER_GROUP + i]), slope)
        slope = slope * float(d)
        for variant, offset in enumerate((-DA_RADIUS, 0, -2 * DA_RADIUS)):
            dist = jnp.abs(col - row + offset)
            bias_ref[g, variant] = jnp.where(dist <= DA_RADIUS, -slope * dist.astype(F32), NEG_INF)

    ones = jnp.ones((bk, HEAD_DIM), BF16)

    def block(idx, g, d, q_ref, k_ref, v_ref):
        Ld = k_ref.shape[2]
        per_class = DA_TQ // d // bq
        r = 0 if d == 1 else idx // per_class
        i = idx if d == 1 else idx % per_class
        n0 = t * (DA_TQ // d) + i * bq
        k0 = pl.multiple_of(jnp.clip(n0 - DA_RADIUS, 0, Ld - bk), DA_RADIUS)
        variant = jnp.where(n0 == 0, 1, jnp.where(n0 == Ld - bq, 2, 0))
        q = q_ref[0, r, pl.ds(pl.multiple_of(i * bq, bq), bq), :]
        k = k_ref[0, r, pl.ds(k0, bk), :]
        v = v_ref[0, r, pl.ds(k0, bk), :]
        s = _dot_nt(q, k) + bias_ref[g, variant]
        m = jnp.max(s, axis=-1, keepdims=True)
        p = jnp.exp(s - m).astype(BF16)
        ol = _dot(p, jnp.concatenate([v, ones], axis=1))
        if d == 1:
            dst = pl.ds(pl.multiple_of(i * bq, bq), bq)
        else:
            dst = pl.ds(i * (bq * d) + r, bq, stride=d)
        og_ref[g, dst, :] = ol[:, :HEAD_DIM]
        l_ref[g, dst, :] = ol[:, HEAD_DIM:]
        m_ref[g, dst, :] = jnp.broadcast_to(m, (bq, HEAD_DIM))

    def body(idx, carry):
        for g, (d, q_ref, k_ref, v_ref) in enumerate(groups):
            block(idx, g, d, q_ref, k_ref, v_ref)
        return carry

    lax.fori_loop(0, DA_BLOCKS, body, 0, unroll=4)

    for c in range(DA_TQ // DA_MERGE_ROWS):
        rows = slice(c * DA_MERGE_ROWS, (c + 1) * DA_MERGE_ROWS)
        ms = [m_ref[g, rows, :] for g in range(len(groups))]
        m = jnp.maximum(jnp.maximum(ms[0], ms[1]), ms[2])
        es = [jnp.exp(x - m) for x in ms]
        num = es[0] * og_ref[0, rows, :]
        den = es[0] * l_ref[0, rows, :]
        for g in range(1, len(groups)):
            num = num + es[g] * og_ref[g, rows, :]
            den = den + es[g] * l_ref[g, rows, :]
        o_ref[0, rows, :] = (num / den).astype(BF16)


def _dilated(qs, ks, vs, slopes):
    B, _, L, W = qs[0].shape
    nh = W // HEAD_DIM
    assert L % DA_TQ == 0 and all(L // d >= DA_BK and DA_TQ // d >= DA_BQ for d in DA_DILATIONS)

    def q_spec(d):
        return pl.BlockSpec((1, d, DA_TQ // d, HEAD_DIM), lambda b, h, t: (b, 0, t, h))

    def kv_spec(d):
        return pl.BlockSpec((1, d, L // d, HEAD_DIM), lambda b, h, t: (b, 0, 0, h))

    n_groups = len(DA_DILATIONS)
    return pl.pallas_call(
        functools.partial(_dilated_kernel, slopes=tuple(slopes)),
        grid=(B, nh, L // DA_TQ),
        in_specs=([q_spec(d) for d in DA_DILATIONS] + [kv_spec(d) for d in DA_DILATIONS] * 2),
        out_specs=pl.BlockSpec((1, DA_TQ, HEAD_DIM), lambda b, h, t: (b, t, h)),
        out_shape=jax.ShapeDtypeStruct((B, L, W), BF16),
        scratch_shapes=[pltpu.VMEM((n_groups, DA_TQ, HEAD_DIM), F32),
                        pltpu.VMEM((n_groups, DA_TQ, HEAD_DIM), F32),
                        pltpu.VMEM((n_groups, DA_TQ, HEAD_DIM), F32),
                        pltpu.VMEM((n_groups, 3, DA_BQ, DA_BK), F32)],
        compiler_params=_params("arbitrary", "arbitrary", "arbitrary"),
        name="dilated",
    )(*qs, *ks, *vs)


def _mix_kernel(x_ref, ohg_ref, oda_ref, mq_ref, mk_ref, mv_ref, gates_ref,
                whg_ref, wda_ref, wmem_ref, wout_ref, out_ref):
    D = x_ref.shape[-1]

    def body(c, rows):
        mem_parts = []
        ones = jnp.ones((mk_ref.shape[1], HEAD_DIM), BF16)
        for h in range(MEM_HEADS):
            sl = slice(h * HEAD_DIM, (h + 1) * HEAD_DIM)
            s = _dot_nt(mq_ref[rows, sl], mk_ref[0, :, sl])
            p = jnp.exp(s - jnp.max(s, axis=-1, keepdims=True)).astype(BF16)
            ol = _dot(p, jnp.concatenate([mv_ref[0, :, sl], ones], axis=1))
            mem_parts.append((ol[:, :HEAD_DIM] / ol[:, HEAD_DIM:]).astype(BF16))
        o_mem = jnp.concatenate(mem_parts, axis=-1)

        merged = (gates_ref[rows, 0:D].astype(F32) * _dot(ohg_ref[rows, :], whg_ref[...])
                  + gates_ref[rows, D:2 * D].astype(F32) * _dot(oda_ref[rows, :], wda_ref[...])
                  + gates_ref[rows, 2 * D:3 * D].astype(F32) * _dot(o_mem, wmem_ref[...]))
        out_ref[rows, :] = x_ref[rows, :] + _dot(merged.astype(BF16), wout_ref[...])

    _for_row_chunks(x_ref.shape[0], body)


def _mix(x2, o_hg, o_da, mem_q, mem_k, mem_v, gates, w_hg, w_da, w_mem, w_out, L):
    T, D = x2.shape
    tm = TOKEN_TILE
    nt = L // tm
    M = mem_k.shape[1]
    const = lambda i: (0, 0)

    def tok(width):
        return pl.BlockSpec((tm, width), lambda i: (i, 0))

    mem_spec = pl.BlockSpec((1, M, mem_k.shape[2]), lambda i: (i // nt, 0, 0))
    return pl.pallas_call(
        _mix_kernel,
        grid=(T // tm,),
        in_specs=[tok(D), tok(o_hg.shape[1]), tok(o_da.shape[1]), tok(mem_q.shape[1]),
                  mem_spec, mem_spec, tok(gates.shape[1]),
                  _resident(w_hg.shape, const), _resident(w_da.shape, const),
                  _resident(w_mem.shape, const), _resident(w_out.shape, const)],
        out_specs=tok(D),
        out_shape=jax.ShapeDtypeStruct((T, D), F32),
        compiler_params=_params("arbitrary"),
        name="mix",
    )(x2, o_hg, o_da, mem_q, mem_k, mem_v, gates, w_hg, w_da, w_mem, w_out)


FFN_TILE = 256


def _ffn_kernel(x_ref, gain_ref, wa_ref, wb_ref, wo_ref, out_ref, h_ref, g_ref):
    d_ff = wa_ref.shape[1]

    def body(c, rows):
        h_ref[...] = _rms_rows(x_ref[rows, :], gain_ref[...]).astype(BF16)
        for j in range(d_ff // FFN_TILE):
            cols = slice(j * FFN_TILE, (j + 1) * FFN_TILE)
            a = _dot(h_ref[...], wa_ref[:, cols])
            b = _dot(h_ref[...], wb_ref[:, cols])
            g_ref[:, cols] = (_silu(a) * b).astype(BF16)
        out_ref[rows, :] = x_ref[rows, :] + _dot(g_ref[...], wo_ref[...])

    _for_row_chunks(x_ref.shape[0], body)


def _ffn(x2, gain, w_in, w_out):
    T, D = x2.shape
    d_ff = w_out.shape[0]
    tm = TOKEN_TILE
    return pl.pallas_call(
        _ffn_kernel,
        grid=(T // tm,),
        in_specs=[
            pl.BlockSpec((tm, D), lambda i: (i, 0)),
            pl.BlockSpec((1, D), lambda i: (0, 0)),
            _resident((D, d_ff), lambda i: (0, 0)),
            _resident((D, d_ff), lambda i: (0, 1)),
            _resident((d_ff, D), lambda i: (0, 0)),
        ],
        out_specs=pl.BlockSpec((tm, D), lambda i: (i, 0)),
        out_shape=jax.ShapeDtypeStruct((T, D), F32),
        scratch_shapes=[pltpu.VMEM((ROW_CHUNK, D), BF16), pltpu.VMEM((ROW_CHUNK, d_ff), BF16)],
        compiler_params=_params("arbitrary"),
        name="ffn",
    )(x2, gain, w_in, w_in, w_out)


def kernel(x, mem, norm_mix_gain, norm_mem_gain, w_in, lb_logits_fw, lb_logits_bw, hg_norm_gain,
           da_q_gain, da_k_gain, w_mem_kv, mem_q_gain, mem_k_gain, w_proj_hg, w_proj_da,
           w_proj_mem, w_out, norm_ffn_gain, w_ffn_in, w_ffn_out):
    B, L, D = x.shape
    depth = w_in.shape[0]
    assert depth == 1, "lower-bound tables are evaluated for a single layer"
    T = B * L
    slopes = _alibi_slopes(DA_HEADS)
    hg_cols = 5 * w_proj_hg.shape[1]
    x2 = x.reshape(T, D)
    for l in range(depth):
        mem_k, mem_v = _mem_kv(mem, norm_mem_gain[l][None], w_mem_kv[l].astype(BF16),
                               mem_k_gain[l][None])
        h, hq, lf_fw, k_fw, lf_bw, k_bw, hv, hg = _proj_hgrn(
            x2, norm_mix_gain[l][None], w_in[l, :, :hg_cols].astype(BF16),
            lb_logits_fw.astype(F32), lb_logits_bw.astype(F32))
        (q0, q1, q2, k0, k1, k2, v0, v1, v2, mem_q, gates) = _proj_attn(
            h, w_in[l, :, hg_cols:].astype(BF16), da_q_gain[l][None], da_k_gain[l][None],
            mem_q_gain[l][None], B, L)

        def seq(t):
            return t.reshape(B, L, t.shape[-1])

        o_hg = _hgrn(seq(hq), seq(lf_fw), seq(k_fw), seq(lf_bw), seq(k_bw), seq(hv), seq(hg),
                     hg_norm_gain[l][None])
        o_da = _dilated((q0, q1, q2), (k0, k1, k2), (v0, v1, v2), slopes)

        x2 = _mix(x2, o_hg.reshape(T, -1), o_da.reshape(T, -1), mem_q, mem_k, mem_v, gates,
                  w_proj_hg[l].astype(BF16), w_proj_da[l].astype(BF16),
                  w_proj_mem[l].astype(BF16), w_out[l].astype(BF16), L)
        x2 = _ffn(x2, norm_ffn_gain[l][None], w_ffn_in[l].astype(BF16), w_ffn_out[l].astype(BF16))
    return x2.reshape(B, L, D)
```

```python
import functools

import jax
import jax.numpy as jnp
from jax import lax
from jax.experimental import pallas as pl
from jax.experimental.pallas import tpu as pltpu

RMS_EPS = 1e-6
NEG_INF = -1e30
HEAD_DIM = 128
HG_CHUNK = 128
HG_UNROLL = 8
DA_CONFIGS = ((128, 1), (512, 4), (2048, 16))
DA_DILATIONS = tuple(d for _, d in DA_CONFIGS)
DA_RADIUS = 64
DA_HEADS_PER_GROUP = 4
DA_HEADS = DA_HEADS_PER_GROUP * len(DA_CONFIGS)
MEM_HEADS = 4
GROUP_WIDTH = DA_HEADS_PER_GROUP * HEAD_DIM
ATTN_SCALE = HEAD_DIM ** -0.5

BF16 = jnp.bfloat16
F32 = jnp.float32

VMEM_LIMIT_BYTES = 56 * 1024 * 1024

TOKEN_TILE = 512
WIDE_TOKEN_TILE = 1024
ROW_CHUNK = 256


def _alibi_slopes(n):
    return [float(2.0 ** (-8.0 * (i + 1) / n)) for i in range(n)]


def _sigmoid(x):
    return 1.0 / (1.0 + jnp.exp(-x))


def _silu(x):
    return x * _sigmoid(x)


def _rms_rows(x, gain):
    ms = jnp.mean(x * x, axis=-1, keepdims=True)
    return x * lax.rsqrt(ms + RMS_EPS) * gain


def _head_norm(x, gain, scale):
    parts = []
    for h in range(x.shape[-1] // HEAD_DIM):
        parts.append(_rms_rows(x[:, h * HEAD_DIM:(h + 1) * HEAD_DIM], gain) * scale)
    return jnp.concatenate(parts, axis=-1)


def _dot(a, b):
    return jnp.dot(a, b, preferred_element_type=F32)


def _dot_nt(a, b):
    return lax.dot_general(a, b, (((1,), (1,)), ((), ())), preferred_element_type=F32)


def _dot_tn(a, b):
    return lax.dot_general(a, b, (((0,), (0,)), ((), ())), preferred_element_type=F32)


def _params(*semantics):
    return pltpu.CompilerParams(dimension_semantics=semantics,
                                vmem_limit_bytes=VMEM_LIMIT_BYTES)


def _resident(shape, index_map):
    return pl.BlockSpec(shape, index_map, pipeline_mode=pl.Buffered(1))


def _for_row_chunks(n_rows, body):
    def step(c, carry):
        body(c, pl.ds(pl.multiple_of(c * ROW_CHUNK, ROW_CHUNK), ROW_CHUNK))
        return carry
    lax.fori_loop(0, n_rows // ROW_CHUNK, step, 0, unroll=2)


def _mem_kv_kernel(mem_ref, gain_ref, w_ref, kgain_ref, k_ref, v_ref):
    mem_n = _rms_rows(mem_ref[0], gain_ref[...]).astype(BF16)
    kv = _dot(mem_n, w_ref[...])
    width = k_ref.shape[-1]
    k_ref[0] = _head_norm(kv[:, :width], kgain_ref[...], 1.0).astype(BF16)
    v_ref[0] = kv[:, width:].astype(BF16)


def _mem_kv(mem, gain, w_kv, k_gain):
    B, M, D = mem.shape
    width = w_kv.shape[1] // 2
    return pl.pallas_call(
        _mem_kv_kernel,
        grid=(B,),
        in_specs=[
            pl.BlockSpec((1, M, D), lambda b: (b, 0, 0)),
            pl.BlockSpec((1, D), lambda b: (0, 0)),
            pl.BlockSpec((D, 2 * width), lambda b: (0, 0)),
            pl.BlockSpec((1, HEAD_DIM), lambda b: (0, 0)),
        ],
        out_specs=[
            pl.BlockSpec((1, M, width), lambda b: (b, 0, 0)),
            pl.BlockSpec((1, M, width), lambda b: (b, 0, 0)),
        ],
        out_shape=[jax.ShapeDtypeStruct((B, M, width), BF16)] * 2,
        compiler_params=_params("arbitrary"),
        name="mem_kv",
    )(mem, gain, w_kv, k_gain)


def _lower_bound(logits):
    e = jnp.exp(logits - jnp.max(logits, axis=0, keepdims=True))
    return e[0:1] / jnp.sum(e, axis=0, keepdims=True)


def _proj_hgrn_kernel(x_ref, gain_ref, w_ref, lbfw_ref, lbbw_ref,
                      h_ref, hq_ref, lffw_ref, kfw_ref, lfbw_ref, kbw_ref, hv_ref, hg_ref):
    W = hq_ref.shape[1]

    def section(rows, s):
        return _dot(h_ref[rows, :], w_ref[:, s * W:(s + 1) * W])

    row = lax.broadcasted_iota(jnp.int32, (ROW_CHUNK, ROW_CHUNK), 0)
    col = lax.broadcasted_iota(jnp.int32, (ROW_CHUNK, ROW_CHUNK), 1)
    same_chunk = (row // HG_CHUNK) == (col // HG_CHUNK)
    tri_fw = jnp.where(same_chunk & (row >= col), 1.0, 0.0).astype(BF16)
    tri_bw = jnp.where(same_chunk & (row <= col), 1.0, 0.0).astype(BF16)

    def forget(s, lb_ref, tri, rows, b_ref, k_ref):
        lb = _lower_bound(lb_ref[...])
        f = lb + (1.0 - lb) * _sigmoid(section(rows, s))
        k_ref[rows, :] = (1.0 - f).astype(BF16)
        lf = jnp.log(f)
        lf_hi = lf.astype(BF16)
        lf_lo = (lf - lf_hi.astype(F32)).astype(BF16)
        bb = _dot(tri, jnp.concatenate([lf_hi, lf_lo], axis=1))
        b_ref[rows, :] = bb[:, :W] + bb[:, W:]

    def body(c, rows):
        h_ref[rows, :] = _rms_rows(x_ref[rows, :], gain_ref[...]).astype(BF16)
        hq_ref[rows, :] = (_silu(section(rows, 0)) * ATTN_SCALE).astype(BF16)
        forget(1, lbfw_ref, tri_fw, rows, lffw_ref, kfw_ref)
        forget(2, lbbw_ref, tri_bw, rows, lfbw_ref, kbw_ref)
        hv_ref[rows, :] = section(rows, 3).astype(BF16)
        hg_ref[rows, :] = _silu(section(rows, 4)).astype(BF16)

    _for_row_chunks(x_ref.shape[0], body)


def _proj_hgrn(x2, gain, w_hg, lb_fw, lb_bw):
    T, D = x2.shape
    W = w_hg.shape[1] // 5
    tm = TOKEN_TILE
    const = lambda i: (0, 0)
    tok = lambda width: pl.BlockSpec((tm, width), lambda i: (i, 0))
    outs = [(D, BF16), (W, BF16), (W, F32), (W, BF16), (W, F32), (W, BF16), (W, BF16), (W, BF16)]
    return pl.pallas_call(
        _proj_hgrn_kernel,
        grid=(T // tm,),
        in_specs=[tok(D), pl.BlockSpec((1, D), const), _resident(w_hg.shape, const),
                  pl.BlockSpec(lb_fw.shape, const), pl.BlockSpec(lb_bw.shape, const)],
        out_specs=[tok(w) for w, _ in outs],
        out_shape=[jax.ShapeDtypeStruct((T, w), dt) for w, dt in outs],
        compiler_params=_params("arbitrary"),
        name="proj_hgrn",
    )(x2, gain, w_hg, lb_fw, lb_bw)


def _proj_attn_kernel(h_ref, w_ref, qg_ref, kg_ref, mg_ref,
                      q0_ref, q1_ref, q2_ref, k0_ref, k1_ref, k2_ref, v0_ref, v1_ref, v2_ref,
                      mq_ref, gates_ref, scr_ref):
    nh = GROUP_WIDTH // HEAD_DIM
    gates_col0 = w_ref.shape[1] - gates_ref.shape[1]
    D = gates_ref.shape[1] // 3

    def q_norm(t):
        return _rms_rows(t, qg_ref[...]) * ATTN_SCALE

    def k_norm(t):
        return _rms_rows(t, kg_ref[...])

    def body(c, rows):
        def project(col0, width):
            return _dot(h_ref[rows, :], w_ref[:, col0:col0 + width])

        def class_major(tile, out_ref, fn):
            d = out_ref.shape[1]
            n = ROW_CHUNK // d
            acc = project(tile * GROUP_WIDTH, GROUP_WIDTH)
            for hh in range(nh):
                cols = slice(hh * HEAD_DIM, (hh + 1) * HEAD_DIM)
                vals = fn(acc[:, cols])
                if d == 1:
                    out_ref[0, 0, rows, cols] = vals.astype(BF16)
                    continue
                scr_ref[tile, hh] = vals
                dst = pl.ds(pl.multiple_of(c * n, n), n)
                for r in range(d):
                    out_ref[0, r, dst, cols] = scr_ref[tile, hh, pl.ds(r, n, stride=d), :].astype(BF16)

        for g, (q_ref, k_ref, v_ref) in enumerate(((q0_ref, k0_ref, v0_ref), (q1_ref, k1_ref, v1_ref),
                                                   (q2_ref, k2_ref, v2_ref))):
            class_major(g, q_ref, q_norm)
            class_major(3 + g, k_ref, k_norm)
            class_major(6 + g, v_ref, lambda t: t)
        mq = project(9 * GROUP_WIDTH, GROUP_WIDTH)
        mq_ref[rows, :] = _head_norm(mq, mg_ref[...], ATTN_SCALE).astype(BF16)
        for s in range(3):
            gate = project(gates_col0 + s * D, D)
            gates_ref[rows, s * D:(s + 1) * D] = _sigmoid(gate).astype(BF16)

    _for_row_chunks(h_ref.shape[0], body)


def _proj_attn(h, w_at, q_gain, k_gain, mq_gain, B, L):
    T, D = h.shape
    tm = TOKEN_TILE
    nt = L // tm
    n_groups = len(DA_DILATIONS)
    gates_width = w_at.shape[1] - (3 * n_groups + 1) * GROUP_WIDTH
    const = lambda i: (0, 0)

    def cls(d):
        return (pl.BlockSpec((1, d, tm // d, GROUP_WIDTH), lambda i: (i // nt, 0, i % nt, 0)),
                jax.ShapeDtypeStruct((B, d, L // d, GROUP_WIDTH), BF16))

    outs = [cls(d) for _ in range(3) for d in DA_DILATIONS]
    for width in (GROUP_WIDTH, gates_width):
        outs.append((pl.BlockSpec((tm, width), lambda i: (i, 0)),
                     jax.ShapeDtypeStruct((T, width), BF16)))
    gain_spec = pl.BlockSpec((1, HEAD_DIM), const)
    return pl.pallas_call(
        _proj_attn_kernel,
        grid=(T // tm,),
        in_specs=[pl.BlockSpec((tm, D), lambda i: (i, 0)), _resident(w_at.shape, const),
                  gain_spec, gain_spec, gain_spec],
        out_specs=[o[0] for o in outs],
        out_shape=[o[1] for o in outs],
        scratch_shapes=[pltpu.VMEM((3 * n_groups, GROUP_WIDTH // HEAD_DIM, ROW_CHUNK, HEAD_DIM), F32)],
        compiler_params=_params("arbitrary"),
        name="proj_attn",
    )(h, w_at, q_gain, k_gain, mq_gain)


HG_PAIR = 2 * HEAD_DIM


def _head_diag(m):
    return jnp.concatenate([m[:HEAD_DIM, :HEAD_DIM], m[HEAD_DIM:, HEAD_DIM:]], axis=1)


def _hgrn_kernel(q_ref, bfw_ref, kfw_ref, bbw_ref, kbw_ref, v_ref, g_ref, gain_ref,
                 o_ref, acc_ref, sfw_ref, sbw_ref):
    L = q_ref.shape[1]
    C = HG_CHUNK
    n_chunks = L // C
    row = lax.broadcasted_iota(jnp.int32, (C, HG_PAIR), 0)
    col = lax.broadcasted_iota(jnp.int32, (C, HG_PAIR), 1)
    first_head = col < HEAD_DIM
    src = jnp.where(first_head, col, col - HEAD_DIM)
    lower = row >= src
    upper = row <= src

    sfw_ref[...] = jnp.zeros_like(sfw_ref)
    sbw_ref[...] = jnp.zeros_like(sbw_ref)

    def emit(rows, o, second):
        if not second:
            acc_ref[rows, :] = o
            return
        o = o + acc_ref[rows, :]
        o_ref[0, rows, :] = (_head_norm(o, gain_ref[...], 1.0)
                             * g_ref[0, rows, :].astype(F32)).astype(BF16)

    def chunk(c, b_ref, k_ref, s, second, causal, mid_row, end_row):
        rows = pl.ds(pl.multiple_of(c * C, C), C)
        b = b_ref[0, rows, :]
        b_mid = b[mid_row:mid_row + 1, :]
        b_end = b[end_row:end_row + 1, :]
        q_in = q_ref[0, rows, :].astype(F32) * jnp.exp(b - b_mid)
        k_in = k_ref[0, rows, :].astype(F32) * jnp.exp(b_mid - b)
        q_st = (q_in * jnp.exp(b_mid)).astype(BF16)
        k_st = (k_in * jnp.exp(b_end - b_mid)).astype(BF16)
        v = v_ref[0, rows, :]
        k_bf = k_in.astype(BF16)
        zero = jnp.zeros_like(k_bf)
        k_diag = jnp.concatenate([jnp.where(first_head, k_bf, zero),
                                  jnp.where(first_head, zero, k_bf)], axis=0)
        a = _dot_nt(q_in.astype(BF16), k_diag)
        a = jnp.where(causal, a, 0.0).astype(BF16)
        lhs = jnp.concatenate(
            [jnp.concatenate([q_st[:, :HEAD_DIM], a[:, :HEAD_DIM]], axis=1),
             jnp.concatenate([q_st[:, HEAD_DIM:], a[:, HEAD_DIM:]], axis=1)], axis=0)
        rhs = jnp.concatenate([s.astype(BF16), v], axis=0)
        emit(rows, _head_diag(_dot(lhs, rhs)), second)
        d = jnp.exp(b_end)
        d_col = jnp.concatenate(
            [jnp.broadcast_to(d[:, :HEAD_DIM], (HEAD_DIM, HEAD_DIM)).T,
             jnp.broadcast_to(d[:, HEAD_DIM:], (HEAD_DIM, HEAD_DIM)).T], axis=1)
        return s * d_col + _head_diag(_dot_tn(k_st, v))

    def body(n, carry, second):
        s_fw = sfw_ref[...]
        s_bw = sbw_ref[...]
        for u in range(HG_UNROLL):
            c = n * HG_UNROLL + u
            s_fw = chunk(c, bfw_ref, kfw_ref, s_fw, second, lower, C // 2 - 1, C - 1)
            s_bw = chunk(n_chunks - 1 - c, bbw_ref, kbw_ref, s_bw, second, upper, C // 2, 0)
        sfw_ref[...] = s_fw
        sbw_ref[...] = s_bw
        return carry

    trips = n_chunks // HG_UNROLL
    lax.fori_loop(0, trips // 2, functools.partial(body, second=False), 0)
    lax.fori_loop(trips // 2, trips, functools.partial(body, second=True), 0)


def _hgrn(hq, b_fw, k_fw, b_bw, k_bw, hv, hg, gain):
    B, L, W = hq.shape
    spec = pl.BlockSpec((1, L, HG_PAIR), lambda b, h: (b, 0, h))
    return pl.pallas_call(
        _hgrn_kernel,
        grid=(B, W // HG_PAIR),
        in_specs=[spec] * 7 + [pl.BlockSpec((1, HEAD_DIM), lambda b, h: (0, 0))],
        out_specs=spec,
        out_shape=jax.ShapeDtypeStruct((B, L, W), BF16),
        scratch_shapes=[pltpu.VMEM((L, HG_PAIR), F32),
                        pltpu.VMEM((HEAD_DIM, HG_PAIR), F32), pltpu.VMEM((HEAD_DIM, HG_PAIR), F32)],
        compiler_params=_params("arbitrary", "arbitrary"),
        name="hgrn2",
    )(hq, b_fw, k_fw, b_bw, k_bw, hv, hg, gain)


DA_TQ = 2048
DA_BQ = 128
DA_BK = DA_BQ + 2 * DA_RADIUS
DA_BLOCKS = DA_TQ // DA_BQ
DA_MERGE_ROWS = 256


def _dilated_kernel(q0_ref, q1_ref, q2_ref, k0_ref, k1_ref, k2_ref, v0_ref, v1_ref, v2_ref,
                    o_ref, og_ref, l_ref, m_ref, bias_ref, *, slopes):
    h = pl.program_id(1)
    t = pl.program_id(2)
    bq, bk = DA_BQ, DA_BK
    groups = tuple(zip(DA_DILATIONS, (q0_ref, q1_ref, q2_ref), (k0_ref, k1_ref, k2_ref),
                       (v0_ref, v1_ref, v2_ref)))

    @pl.when(t == 0)
    def _():
        row = lax.broadcasted_iota(jnp.int32, (bq, bk), 0)
        col = lax.broadcasted_iota(jnp.int32, (bq, bk), 1)
        for g, d in enumerate(DA_DILATIONS):
            slope = jnp.float32(slopes[g * DA_HEADS_PER_GROUP])
            for i in range(1, DA_HEADS_PER_GROUP):
                slope = jnp.where(h == i, jnp.float32(slopes[g * DA_HEADS_PER_GROUP + i]), slope)
            slope = slope * float(d)
            for variant, offset in enumerate((-DA_RADIUS, 0, -2 * DA_RADIUS)):
                dist = jnp.abs(col - row + offset)
                bias_ref[g, variant] = jnp.where(dist <= DA_RADIUS, -slope * dist.astype(F32),
                                                 NEG_INF)

    ones = jnp.ones((bk, HEAD_DIM), BF16)

    def block(idx, g, d, q_ref, k_ref, v_ref):
        Ld = k_ref.shape[2]
        per_class = DA_TQ // d // bq
        r = 0 if d == 1 else idx // per_class
        i = idx if d == 1 else idx % per_class
        n0 = t * (DA_TQ // d) + i * bq
        k0 = pl.multiple_of(jnp.clip(n0 - DA_RADIUS, 0, Ld - bk), DA_RADIUS)
        variant = jnp.where(n0 == 0, 1, jnp.where(n0 == Ld - bq, 2, 0))
        q = q_ref[0, r, pl.ds(pl.multiple_of(i * bq, bq), bq), :]
        k = k_ref[0, r, pl.ds(k0, bk), :]
        v = v_ref[0, r, pl.ds(k0, bk), :]
        s = _dot_nt(q, k) + bias_ref[g, variant]
        m = jnp.max(s, axis=-1, keepdims=True)
        p = jnp.exp(s - m).astype(BF16)
        ol = _dot(p, jnp.concatenate([v, ones], axis=1))
        if d == 1:
            dst = pl.ds(pl.multiple_of(i * bq, bq), bq)
        else:
            dst = pl.ds(i * (bq * d) + r, bq, stride=d)
        og_ref[g, dst, :] = ol[:, :HEAD_DIM]
        l_ref[g, dst, :] = ol[:, HEAD_DIM:]
        m_ref[g, dst, :] = jnp.broadcast_to(m, (bq, HEAD_DIM))

    def body(idx, carry):
        for g, (d, q_ref, k_ref, v_ref) in enumerate(groups):
            block(idx, g, d, q_ref, k_ref, v_ref)
        return carry

    lax.fori_loop(0, DA_BLOCKS, body, 0, unroll=4)

    for c in range(DA_TQ // DA_MERGE_ROWS):
        rows = slice(c * DA_MERGE_ROWS, (c + 1) * DA_MERGE_ROWS)
        ms = [m_ref[g, rows, :] for g in range(len(groups))]
        m = jnp.maximum(jnp.maximum(ms[0], ms[1]), ms[2])
        es = [jnp.exp(x - m) for x in ms]
        num = es[0] * og_ref[0, rows, :]
        den = es[0] * l_ref[0, rows, :]
        for g in range(1, len(groups)):
            num = num + es[g] * og_ref[g, rows, :]
            den = den + es[g] * l_ref[g, rows, :]
        o_ref[0, rows, :] = (num / den).astype(BF16)


def _dilated(qs, ks, vs, slopes):
    B, _, L, W = qs[0].shape
    nh = W // HEAD_DIM
    assert L % DA_TQ == 0 and all(L // d >= DA_BK and DA_TQ // d >= DA_BQ for d in DA_DILATIONS)

    def q_spec(d):
        return pl.BlockSpec((1, d, DA_TQ // d, HEAD_DIM), lambda b, h, t: (b, 0, t, h))

    def kv_spec(d):
        return pl.BlockSpec((1, d, L // d, HEAD_DIM), lambda b, h, t: (b, 0, 0, h))

    n_groups = len(DA_DILATIONS)
    return pl.pallas_call(
        functools.partial(_dilated_kernel, slopes=tuple(slopes)),
        grid=(B, nh, L // DA_TQ),
        in_specs=([q_spec(d) for d in DA_DILATIONS] + [kv_spec(d) for d in DA_DILATIONS] * 2),
        out_specs=pl.BlockSpec((1, DA_TQ, HEAD_DIM), lambda b, h, t: (b, t, h)),
        out_shape=jax.ShapeDtypeStruct((B, L, W), BF16),
        scratch_shapes=[pltpu.VMEM((n_groups, DA_TQ, HEAD_DIM), F32),
                        pltpu.VMEM((n_groups, DA_TQ, HEAD_DIM), F32),
                        pltpu.VMEM((n_groups, DA_TQ, HEAD_DIM), F32),
                        pltpu.VMEM((n_groups, 3, DA_BQ, DA_BK), F32)],
        compiler_params=_params("arbitrary", "arbitrary", "arbitrary"),
        name="dilated",
    )(*qs, *ks, *vs)


def _mix_kernel(x_ref, ohg_ref, oda_ref, mq_ref, mk_ref, mv_ref, gates_ref,
                whg_ref, wda_ref, wmem_ref, wout_ref, out_ref):
    D = x_ref.shape[-1]

    def body(c, rows):
        mem_parts = []
        ones = jnp.ones((mk_ref.shape[1], HEAD_DIM), BF16)
        for h in range(MEM_HEADS):
            sl = slice(h * HEAD_DIM, (h + 1) * HEAD_DIM)
            s = _dot_nt(mq_ref[rows, sl], mk_ref[0, :, sl])
            p = jnp.exp(s - jnp.max(s, axis=-1, keepdims=True)).astype(BF16)
            ol = _dot(p, jnp.concatenate([mv_ref[0, :, sl], ones], axis=1))
            mem_parts.append((ol[:, :HEAD_DIM] / ol[:, HEAD_DIM:]).astype(BF16))
        o_mem = jnp.concatenate(mem_parts, axis=-1)

        merged = (gates_ref[rows, 0:D].astype(F32) * _dot(ohg_ref[rows, :], whg_ref[...])
                  + gates_ref[rows, D:2 * D].astype(F32) * _dot(oda_ref[rows, :], wda_ref[...])
                  + gates_ref[rows, 2 * D:3 * D].astype(F32) * _dot(o_mem, wmem_ref[...]))
        out_ref[rows, :] = x_ref[rows, :] + _dot(merged.astype(BF16), wout_ref[...])

    _for_row_chunks(x_ref.shape[0], body)


def _mix(x2, o_hg, o_da, mem_q, mem_k, mem_v, gates, w_hg, w_da, w_mem, w_out, L):
    T, D = x2.shape
    tm = WIDE_TOKEN_TILE
    nt = L // tm
    M = mem_k.shape[1]
    const = lambda i: (0, 0)

    def tok(width):
        return pl.BlockSpec((tm, width), lambda i: (i, 0))

    mem_spec = pl.BlockSpec((1, M, mem_k.shape[2]), lambda i: (i // nt, 0, 0))
    return pl.pallas_call(
        _mix_kernel,
        grid=(T // tm,),
        in_specs=[tok(D), tok(o_hg.shape[1]), tok(o_da.shape[1]), tok(mem_q.shape[1]),
                  mem_spec, mem_spec, tok(gates.shape[1]),
                  _resident(w_hg.shape, const), _resident(w_da.shape, const),
                  _resident(w_mem.shape, const), _resident(w_out.shape, const)],
        out_specs=tok(D),
        out_shape=jax.ShapeDtypeStruct((T, D), F32),
        compiler_params=_params("arbitrary"),
        name="mix",
    )(x2, o_hg, o_da, mem_q, mem_k, mem_v, gates, w_hg, w_da, w_mem, w_out)


FFN_TILE = 256


def _ffn_kernel(x_ref, gain_ref, wa_ref, wb_ref, wo_ref, out_ref, h_ref, g_ref):
    d_ff = wa_ref.shape[1]

    def body(c, rows):
        h_ref[...] = _rms_rows(x_ref[rows, :], gain_ref[...]).astype(BF16)
        for j in range(d_ff // FFN_TILE):
            cols = slice(j * FFN_TILE, (j + 1) * FFN_TILE)
            a = _dot(h_ref[...], wa_ref[:, cols])
            b = _dot(h_ref[...], wb_ref[:, cols])
            g_ref[:, cols] = (_silu(a) * b).astype(BF16)
        out_ref[rows, :] = x_ref[rows, :] + _dot(g_ref[...], wo_ref[...])

    _for_row_chunks(x_ref.shape[0], body)


def _ffn(x2, gain, w_in, w_out):
    T, D = x2.shape
    d_ff = w_out.shape[0]
    tm = WIDE_TOKEN_TILE
    return pl.pallas_call(
        _ffn_kernel,
        grid=(T // tm,),
        in_specs=[
            pl.BlockSpec((tm, D), lambda i: (i, 0)),
            pl.BlockSpec((1, D), lambda i: (0, 0)),
            _resident((D, d_ff), lambda i: (0, 0)),
            _resident((D, d_ff), lambda i: (0, 1)),
            _resident((d_ff, D), lambda i: (0, 0)),
        ],
        out_specs=pl.BlockSpec((tm, D), lambda i: (i, 0)),
        out_shape=jax.ShapeDtypeStruct((T, D), F32),
        scratch_shapes=[pltpu.VMEM((ROW_CHUNK, D), BF16), pltpu.VMEM((ROW_CHUNK, d_ff), BF16)],
        compiler_params=_params("arbitrary"),
        name="ffn",
    )(x2, gain, w_in, w_in, w_out)


def kernel(x, mem, norm_mix_gain, norm_mem_gain, w_in, lb_logits_fw, lb_logits_bw, hg_norm_gain,
           da_q_gain, da_k_gain, w_mem_kv, mem_q_gain, mem_k_gain, w_proj_hg, w_proj_da,
           w_proj_mem, w_out, norm_ffn_gain, w_ffn_in, w_ffn_out):
    B, L, D = x.shape
    depth = w_in.shape[0]
    assert depth == 1, "lower-bound tables are evaluated for a single layer"
    T = B * L
    slopes = _alibi_slopes(DA_HEADS)
    hg_cols = 5 * w_proj_hg.shape[1]
    x2 = x.reshape(T, D)
    for l in range(depth):
        mem_k, mem_v = _mem_kv(mem, norm_mem_gain[l][None], w_mem_kv[l].astype(BF16),
                               mem_k_gain[l][None])
        h, hq, lf_fw, k_fw, lf_bw, k_bw, hv, hg = _proj_hgrn(
            x2, norm_mix_gain[l][None], w_in[l, :, :hg_cols].astype(BF16),
            lb_logits_fw.astype(F32), lb_logits_bw.astype(F32))
        (q0, q1, q2, k0, k1, k2, v0, v1, v2, mem_q, gates) = _proj_attn(
            h, w_in[l, :, hg_cols:].astype(BF16), da_q_gain[l][None], da_k_gain[l][None],
            mem_q_gain[l][None], B, L)

        def seq(t):
            return t.reshape(B, L, t.shape[-1])

        o_hg = _hgrn(seq(hq), seq(lf_fw), seq(k_fw), seq(lf_bw), seq(k_bw), seq(hv), seq(hg),
                     hg_norm_gain[l][None])
        o_da = _dilated((q0, q1, q2), (k0, k1, k2), (v0, v1, v2), slopes)

        x2 = _mix(x2, o_hg.reshape(T, -1), o_da.reshape(T, -1), mem_q, mem_k, mem_v, gates,
                  w_proj_hg[l].astype(BF16), w_proj_da[l].astype(BF16),
                  w_proj_mem[l].astype(BF16), w_out[l].astype(BF16), L)
        x2 = _ffn(x2, norm_ffn_gain[l][None], w_ffn_in[l].astype(BF16), w_ffn_out[l].astype(BF16))
    return x2.reshape(B, L, D)
```

```python
import functools

import jax
import jax.numpy as jnp
from jax import lax
from jax.experimental import pallas as pl
from jax.experimental.pallas import tpu as pltpu

RMS_EPS = 1e-6
NEG_INF = -1e30
HEAD_DIM = 128
HG_CHUNK = 128
HG_UNROLL = 16
DA_CONFIGS = ((128, 1), (512, 4), (2048, 16))
DA_DILATIONS = tuple(d for _, d in DA_CONFIGS)
DA_RADIUS = 64
DA_HEADS_PER_GROUP = 4
DA_HEADS = DA_HEADS_PER_GROUP * len(DA_CONFIGS)
MEM_HEADS = 4
GROUP_WIDTH = DA_HEADS_PER_GROUP * HEAD_DIM
ATTN_SCALE = HEAD_DIM ** -0.5

BF16 = jnp.bfloat16
F32 = jnp.float32

VMEM_LIMIT_BYTES = 56 * 1024 * 1024

TOKEN_TILE = 512
WIDE_TOKEN_TILE = 1024
ROW_CHUNK = 256


def _alibi_slopes(n):
    return [float(2.0 ** (-8.0 * (i + 1) / n)) for i in range(n)]


def _sigmoid(x):
    return 1.0 / (1.0 + jnp.exp(-x))


def _silu(x):
    return x * _sigmoid(x)


def _rms_rows(x, gain):
    ms = jnp.mean(x * x, axis=-1, keepdims=True)
    return x * lax.rsqrt(ms + RMS_EPS) * gain


def _head_norm(x, gain, scale):
    parts = []
    for h in range(x.shape[-1] // HEAD_DIM):
        parts.append(_rms_rows(x[:, h * HEAD_DIM:(h + 1) * HEAD_DIM], gain) * scale)
    return jnp.concatenate(parts, axis=-1)


def _dot(a, b):
    return jnp.dot(a, b, preferred_element_type=F32)


def _dot_nt(a, b):
    return lax.dot_general(a, b, (((1,), (1,)), ((), ())), preferred_element_type=F32)


def _dot_tn(a, b):
    return lax.dot_general(a, b, (((0,), (0,)), ((), ())), preferred_element_type=F32)


def _params(*semantics):
    return pltpu.CompilerParams(dimension_semantics=semantics,
                                vmem_limit_bytes=VMEM_LIMIT_BYTES)


def _resident(shape, index_map):
    return pl.BlockSpec(shape, index_map, pipeline_mode=pl.Buffered(1))


def _for_row_chunks(n_rows, body):
    def step(c, carry):
        body(c, pl.ds(pl.multiple_of(c * ROW_CHUNK, ROW_CHUNK), ROW_CHUNK))
        return carry
    lax.fori_loop(0, n_rows // ROW_CHUNK, step, 0, unroll=2)


def _mem_kv_kernel(mem_ref, gain_ref, w_ref, kgain_ref, k_ref, v_ref):
    mem_n = _rms_rows(mem_ref[0], gain_ref[...]).astype(BF16)
    kv = _dot(mem_n, w_ref[...])
    width = k_ref.shape[-1]
    k_ref[0] = _head_norm(kv[:, :width], kgain_ref[...], 1.0).astype(BF16)
    v_ref[0] = kv[:, width:].astype(BF16)


def _mem_kv(mem, gain, w_kv, k_gain):
    B, M, D = mem.shape
    width = w_kv.shape[1] // 2
    return pl.pallas_call(
        _mem_kv_kernel,
        grid=(B,),
        in_specs=[
            pl.BlockSpec((1, M, D), lambda b: (b, 0, 0)),
            pl.BlockSpec((1, D), lambda b: (0, 0)),
            pl.BlockSpec((D, 2 * width), lambda b: (0, 0)),
            pl.BlockSpec((1, HEAD_DIM), lambda b: (0, 0)),
        ],
        out_specs=[
            pl.BlockSpec((1, M, width), lambda b: (b, 0, 0)),
            pl.BlockSpec((1, M, width), lambda b: (b, 0, 0)),
        ],
        out_shape=[jax.ShapeDtypeStruct((B, M, width), BF16)] * 2,
        compiler_params=_params("arbitrary"),
        name="mem_kv",
    )(mem, gain, w_kv, k_gain)


def _lower_bound(logits):
    e = jnp.exp(logits - jnp.max(logits, axis=0, keepdims=True))
    return e[0:1] / jnp.sum(e, axis=0, keepdims=True)


def _proj_hgrn_kernel(x_ref, gain_ref, w_ref, lbfw_ref, lbbw_ref,
                      h_ref, hq_ref, lffw_ref, kfw_ref, lfbw_ref, kbw_ref, hv_ref, hg_ref):
    W = hq_ref.shape[1]

    def section(rows, s):
        return _dot(h_ref[rows, :], w_ref[:, s * W:(s + 1) * W])

    row = lax.broadcasted_iota(jnp.int32, (ROW_CHUNK, ROW_CHUNK), 0)
    col = lax.broadcasted_iota(jnp.int32, (ROW_CHUNK, ROW_CHUNK), 1)
    same_chunk = (row // HG_CHUNK) == (col // HG_CHUNK)
    tri_fw = jnp.where(same_chunk & (row >= col), 1.0, 0.0).astype(BF16)
    tri_bw = jnp.where(same_chunk & (row <= col), 1.0, 0.0).astype(BF16)

    def forget(s, lb_ref, tri, rows, b_ref, k_ref):
        lb = _lower_bound(lb_ref[...])
        f = lb + (1.0 - lb) * _sigmoid(section(rows, s))
        k_ref[rows, :] = (1.0 - f).astype(BF16)
        lf = jnp.log(f)
        lf_hi = lf.astype(BF16)
        lf_lo = (lf - lf_hi.astype(F32)).astype(BF16)
        bb = _dot(tri, jnp.concatenate([lf_hi, lf_lo], axis=1))
        b_ref[rows, :] = bb[:, :W] + bb[:, W:]

    def body(c, rows):
        h_ref[rows, :] = _rms_rows(x_ref[rows, :], gain_ref[...]).astype(BF16)
        hq_ref[rows, :] = (_silu(section(rows, 0)) * ATTN_SCALE).astype(BF16)
        forget(1, lbfw_ref, tri_fw, rows, lffw_ref, kfw_ref)
        forget(2, lbbw_ref, tri_bw, rows, lfbw_ref, kbw_ref)
        hv_ref[rows, :] = section(rows, 3).astype(BF16)
        hg_ref[rows, :] = _silu(section(rows, 4)).astype(BF16)

    _for_row_chunks(x_ref.shape[0], body)


def _proj_hgrn(x2, gain, w_hg, lb_fw, lb_bw):
    T, D = x2.shape
    W = w_hg.shape[1] // 5
    tm = TOKEN_TILE
    const = lambda i: (0, 0)
    tok = lambda width: pl.BlockSpec((tm, width), lambda i: (i, 0))
    outs = [(D, BF16), (W, BF16), (W, F32), (W, BF16), (W, F32), (W, BF16), (W, BF16), (W, BF16)]
    return pl.pallas_call(
        _proj_hgrn_kernel,
        grid=(T // tm,),
        in_specs=[tok(D), pl.BlockSpec((1, D), const), _resident(w_hg.shape, const),
                  pl.BlockSpec(lb_fw.shape, const), pl.BlockSpec(lb_bw.shape, const)],
        out_specs=[tok(w) for w, _ in outs],
        out_shape=[jax.ShapeDtypeStruct((T, w), dt) for w, dt in outs],
        compiler_params=_params("arbitrary"),
        name="proj_hgrn",
    )(x2, gain, w_hg, lb_fw, lb_bw)


def _proj_attn_kernel(h_ref, w_ref, qg_ref, kg_ref, mg_ref,
                      q0_ref, q1_ref, q2_ref, k0_ref, k1_ref, k2_ref, v0_ref, v1_ref, v2_ref,
                      mq_ref, gates_ref, scr_ref):
    nh = GROUP_WIDTH // HEAD_DIM
    gates_col0 = w_ref.shape[1] - gates_ref.shape[1]
    D = gates_ref.shape[1] // 3

    def q_norm(t):
        return _rms_rows(t, qg_ref[...]) * ATTN_SCALE

    def k_norm(t):
        return _rms_rows(t, kg_ref[...])

    def body(c, rows):
        def project(col0, width):
            return _dot(h_ref[rows, :], w_ref[:, col0:col0 + width])

        def class_major(tile, out_ref, fn):
            d = out_ref.shape[1]
            n = ROW_CHUNK // d
            acc = project(tile * GROUP_WIDTH, GROUP_WIDTH)
            for hh in range(nh):
                cols = slice(hh * HEAD_DIM, (hh + 1) * HEAD_DIM)
                vals = fn(acc[:, cols])
                if d == 1:
                    out_ref[0, 0, rows, cols] = vals.astype(BF16)
                    continue
                scr_ref[tile, hh] = vals
                dst = pl.ds(pl.multiple_of(c * n, n), n)
                for r in range(d):
                    out_ref[0, r, dst, cols] = scr_ref[tile, hh, pl.ds(r, n, stride=d), :].astype(BF16)

        for g, (q_ref, k_ref, v_ref) in enumerate(((q0_ref, k0_ref, v0_ref), (q1_ref, k1_ref, v1_ref),
                                                   (q2_ref, k2_ref, v2_ref))):
            class_major(g, q_ref, q_norm)
            class_major(3 + g, k_ref, k_norm)
            class_major(6 + g, v_ref, lambda t: t)
        mq = project(9 * GROUP_WIDTH, GROUP_WIDTH)
        mq_ref[rows, :] = _head_norm(mq, mg_ref[...], ATTN_SCALE).astype(BF16)
        gates_ref[rows, :] = _sigmoid(project(gates_col0, 3 * D)).astype(BF16)

    _for_row_chunks(h_ref.shape[0], body)


def _proj_attn(h, w_at, q_gain, k_gain, mq_gain, B, L):
    T, D = h.shape
    tm = TOKEN_TILE
    nt = L // tm
    n_groups = len(DA_DILATIONS)
    gates_width = w_at.shape[1] - (3 * n_groups + 1) * GROUP_WIDTH
    const = lambda i: (0, 0)

    def cls(d):
        return (pl.BlockSpec((1, d, tm // d, GROUP_WIDTH), lambda i: (i // nt, 0, i % nt, 0)),
                jax.ShapeDtypeStruct((B, d, L // d, GROUP_WIDTH), BF16))

    outs = [cls(d) for _ in range(3) for d in DA_DILATIONS]
    for width in (GROUP_WIDTH, gates_width):
        outs.append((pl.BlockSpec((tm, width), lambda i: (i, 0)),
                     jax.ShapeDtypeStruct((T, width), BF16)))
    gain_spec = pl.BlockSpec((1, HEAD_DIM), const)
    return pl.pallas_call(
        _proj_attn_kernel,
        grid=(T // tm,),
        in_specs=[pl.BlockSpec((tm, D), lambda i: (i, 0)), _resident(w_at.shape, const),
                  gain_spec, gain_spec, gain_spec],
        out_specs=[o[0] for o in outs],
        out_shape=[o[1] for o in outs],
        scratch_shapes=[pltpu.VMEM((3 * n_groups, GROUP_WIDTH // HEAD_DIM, ROW_CHUNK, HEAD_DIM), F32)],
        compiler_params=_params("arbitrary"),
        name="proj_attn",
    )(h, w_at, q_gain, k_gain, mq_gain)


HG_PAIR = 2 * HEAD_DIM


def _head_diag(m):
    return jnp.concatenate([m[:HEAD_DIM, :HEAD_DIM], m[HEAD_DIM:, HEAD_DIM:]], axis=1)


def _hgrn_kernel(q_ref, bfw_ref, kfw_ref, bbw_ref, kbw_ref, v_ref, g_ref, gain_ref,
                 o_ref, acc_ref, sfw_ref, sbw_ref):
    L = q_ref.shape[1]
    C = HG_CHUNK
    n_chunks = L // C
    row = lax.broadcasted_iota(jnp.int32, (C, HG_PAIR), 0)
    col = lax.broadcasted_iota(jnp.int32, (C, HG_PAIR), 1)
    first_head = col < HEAD_DIM
    src = jnp.where(first_head, col, col - HEAD_DIM)
    lower = row >= src
    upper = row <= src

    sfw_ref[...] = jnp.zeros_like(sfw_ref)
    sbw_ref[...] = jnp.zeros_like(sbw_ref)

    def emit(rows, o, second):
        if not second:
            acc_ref[rows, :] = o
            return
        o = o + acc_ref[rows, :]
        o_ref[0, rows, :] = (_head_norm(o, gain_ref[...], 1.0)
                             * g_ref[0, rows, :].astype(F32)).astype(BF16)

    def chunk(c, b_ref, k_ref, s, second, causal, mid_row, end_row):
        rows = pl.ds(pl.multiple_of(c * C, C), C)
        b = b_ref[0, rows, :]
        b_mid = b[mid_row:mid_row + 1, :]
        b_end = b[end_row:end_row + 1, :]
        q_in = q_ref[0, rows, :].astype(F32) * jnp.exp(b - b_mid)
        k_in = k_ref[0, rows, :].astype(F32) * jnp.exp(b_mid - b)
        q_st = (q_in * jnp.exp(b_mid)).astype(BF16)
        k_st = (k_in * jnp.exp(b_end - b_mid)).astype(BF16)
        v = v_ref[0, rows, :]
        k_bf = k_in.astype(BF16)
        zero = jnp.zeros_like(k_bf)
        k_diag = jnp.concatenate([jnp.where(first_head, k_bf, zero),
                                  jnp.where(first_head, zero, k_bf)], axis=0)
        a = _dot_nt(q_in.astype(BF16), k_diag)
        a = jnp.where(causal, a, 0.0).astype(BF16)
        lhs = jnp.concatenate(
            [jnp.concatenate([q_st[:, :HEAD_DIM], a[:, :HEAD_DIM]], axis=1),
             jnp.concatenate([q_st[:, HEAD_DIM:], a[:, HEAD_DIM:]], axis=1)], axis=0)
        rhs = jnp.concatenate([s.astype(BF16), v], axis=0)
        emit(rows, _head_diag(_dot(lhs, rhs)), second)
        d = jnp.exp(b_end)
        d_col = jnp.concatenate(
            [jnp.broadcast_to(d[:, :HEAD_DIM], (HEAD_DIM, HEAD_DIM)).T,
             jnp.broadcast_to(d[:, HEAD_DIM:], (HEAD_DIM, HEAD_DIM)).T], axis=1)
        return s * d_col + _head_diag(_dot_tn(k_st, v))

    def body(n, carry, second):
        s_fw = sfw_ref[...]
        s_bw = sbw_ref[...]
        for u in range(HG_UNROLL):
            c = n * HG_UNROLL + u
            s_fw = chunk(c, bfw_ref, kfw_ref, s_fw, second, lower, C // 2 - 1, C - 1)
            s_bw = chunk(n_chunks - 1 - c, bbw_ref, kbw_ref, s_bw, second, upper, C // 2, 0)
        sfw_ref[...] = s_fw
        sbw_ref[...] = s_bw
        return carry

    trips = n_chunks // HG_UNROLL
    lax.fori_loop(0, trips // 2, functools.partial(body, second=False), 0)
    lax.fori_loop(trips // 2, trips, functools.partial(body, second=True), 0)


def _hgrn(hq, b_fw, k_fw, b_bw, k_bw, hv, hg, gain):
    B, L, W = hq.shape
    spec = pl.BlockSpec((1, L, HG_PAIR), lambda b, h: (b, 0, h))
    return pl.pallas_call(
        _hgrn_kernel,
        grid=(B, W // HG_PAIR),
        in_specs=[spec] * 7 + [pl.BlockSpec((1, HEAD_DIM), lambda b, h: (0, 0))],
        out_specs=spec,
        out_shape=jax.ShapeDtypeStruct((B, L, W), BF16),
        scratch_shapes=[pltpu.VMEM((L, HG_PAIR), F32),
                        pltpu.VMEM((HEAD_DIM, HG_PAIR), F32), pltpu.VMEM((HEAD_DIM, HG_PAIR), F32)],
        compiler_params=_params("arbitrary", "arbitrary"),
        name="hgrn2",
    )(hq, b_fw, k_fw, b_bw, k_bw, hv, hg, gain)


DA_TQ = 2048
DA_BQ = 128
DA_BK = DA_BQ + 2 * DA_RADIUS
DA_BLOCKS = DA_TQ // DA_BQ
DA_MERGE_ROWS = 256
DA_PAIR = 1


def _dilated_kernel(q0_ref, q1_ref, q2_ref, k0_ref, k1_ref, k2_ref, v0_ref, v1_ref, v2_ref,
                    o_ref, og_ref, lse_ref, bias_ref, *, slopes):
    pair = pl.program_id(1)
    t = pl.program_id(2)
    bq, bk = DA_BQ, DA_BK
    groups = tuple(zip(DA_DILATIONS, (q0_ref, q1_ref, q2_ref), (k0_ref, k1_ref, k2_ref),
                       (v0_ref, v1_ref, v2_ref)))

    @pl.when(t == 0)
    def _():
        row = lax.broadcasted_iota(jnp.int32, (bq, bk), 0)
        col = lax.broadcasted_iota(jnp.int32, (bq, bk), 1)
        n_pairs = DA_HEADS_PER_GROUP // DA_PAIR
        for g, d in enumerate(DA_DILATIONS):
            for hh in range(DA_PAIR):
                base = g * DA_HEADS_PER_GROUP + hh
                slope = jnp.float32(slopes[base])
                for p in range(1, n_pairs):
                    slope = jnp.where(pair == p, jnp.float32(slopes[base + p * DA_PAIR]), slope)
                slope = slope * float(d)
                for variant, offset in enumerate((-DA_RADIUS, 0, -2 * DA_RADIUS)):
                    dist = jnp.abs(col - row + offset)
                    bias_ref[g, variant, :, hh * bk:(hh + 1) * bk] = jnp.where(
                        dist <= DA_RADIUS, -slope * dist.astype(F32), NEG_INF)

    ones = jnp.ones((bk, HEAD_DIM), BF16)
    lane_head = lax.broadcasted_iota(jnp.int32, (bk, DA_PAIR * HEAD_DIM), 1) // HEAD_DIM

    def block(idx, g, d, q_ref, k_ref, v_ref):
        Ld = k_ref.shape[2]
        per_class = DA_TQ // d // bq
        r = 0 if d == 1 else idx // per_class
        i = idx if d == 1 else idx % per_class
        n0 = t * (DA_TQ // d) + i * bq
        k0 = pl.multiple_of(jnp.clip(n0 - DA_RADIUS, 0, Ld - bk), DA_RADIUS)
        variant = jnp.where(n0 == 0, 1, jnp.where(n0 == Ld - bq, 2, 0))
        q = q_ref[0, r, pl.ds(pl.multiple_of(i * bq, bq), bq), :]
        k = k_ref[0, r, pl.ds(k0, bk), :]
        v = v_ref[0, r, pl.ds(k0, bk), :]
        if DA_PAIR == 1:
            k_diag = k
        else:
            zero = jnp.zeros_like(k)
            k_diag = jnp.concatenate([jnp.where(lane_head == hh, k, zero)
                                      for hh in range(DA_PAIR)], axis=0)
        s2 = _dot_nt(q, k_diag) + bias_ref[g, variant]
        if d == 1:
            dst = pl.ds(pl.multiple_of(i * bq, bq), bq)
        else:
            dst = pl.ds(i * (bq * d) + r, bq, stride=d)
        for hh in range(DA_PAIR):
            s = s2[:, hh * bk:(hh + 1) * bk]
            m = jnp.max(s, axis=-1, keepdims=True)
            p = jnp.exp(s - m).astype(BF16)
            ol = _dot(p, jnp.concatenate([v[:, hh * HEAD_DIM:(hh + 1) * HEAD_DIM], ones], axis=1))
            l = ol[:, HEAD_DIM:]
            og_ref[g, hh, dst, :] = ol[:, :HEAD_DIM] / l
            lse_ref[g, hh, dst, :] = m + jnp.log(l)

    def body(idx, carry):
        for g, (d, q_ref, k_ref, v_ref) in enumerate(groups):
            block(idx, g, d, q_ref, k_ref, v_ref)
        return carry

    lax.fori_loop(0, DA_BLOCKS, body, 0, unroll=4)

    for hh in range(DA_PAIR):
        for c in range(DA_TQ // DA_MERGE_ROWS):
            rows = slice(c * DA_MERGE_ROWS, (c + 1) * DA_MERGE_ROWS)
            ls = [lse_ref[g, hh, rows, :] for g in range(len(groups))]
            m = jnp.maximum(jnp.maximum(ls[0], ls[1]), ls[2])
            es = [jnp.exp(x - m) for x in ls]
            inv = 1.0 / (es[0] + es[1] + es[2])
            acc = (es[0] * inv) * og_ref[0, hh, rows, :]
            for g in range(1, len(groups)):
                acc = acc + (es[g] * inv) * og_ref[g, hh, rows, :]
            o_ref[0, rows, hh * HEAD_DIM:(hh + 1) * HEAD_DIM] = acc.astype(BF16)


def _dilated(qs, ks, vs, slopes):
    B, _, L, W = qs[0].shape
    width = DA_PAIR * HEAD_DIM
    assert L % DA_TQ == 0 and all(L // d >= DA_BK and DA_TQ // d >= DA_BQ for d in DA_DILATIONS)

    def q_spec(d):
        return pl.BlockSpec((1, d, DA_TQ // d, width), lambda b, h, t: (b, 0, t, h))

    def kv_spec(d):
        return pl.BlockSpec((1, d, L // d, width), lambda b, h, t: (b, 0, 0, h))

    n_groups = len(DA_DILATIONS)
    return pl.pallas_call(
        functools.partial(_dilated_kernel, slopes=tuple(slopes)),
        grid=(B, W // width, L // DA_TQ),
        in_specs=([q_spec(d) for d in DA_DILATIONS] + [kv_spec(d) for d in DA_DILATIONS] * 2),
        out_specs=pl.BlockSpec((1, DA_TQ, width), lambda b, h, t: (b, t, h)),
        out_shape=jax.ShapeDtypeStruct((B, L, W), BF16),
        scratch_shapes=[pltpu.VMEM((n_groups, DA_PAIR, DA_TQ, HEAD_DIM), F32),
                        pltpu.VMEM((n_groups, DA_PAIR, DA_TQ, HEAD_DIM), F32),
                        pltpu.VMEM((n_groups, 3, DA_BQ, DA_PAIR * DA_BK), F32)],
        compiler_params=_params("arbitrary", "arbitrary", "arbitrary"),
        name="dilated",
    )(*qs, *ks, *vs)


def _mix_kernel(x_ref, ohg_ref, oda_ref, mq_ref, mk_ref, mv_ref, gates_ref,
                whg_ref, wda_ref, wmem_ref, wout_ref, out_ref):
    D = x_ref.shape[-1]

    def body(c, rows):
        mem_parts = []
        ones = jnp.ones((mk_ref.shape[1], HEAD_DIM), BF16)
        for h in range(MEM_HEADS):
            sl = slice(h * HEAD_DIM, (h + 1) * HEAD_DIM)
            s = _dot_nt(mq_ref[rows, sl], mk_ref[0, :, sl])
            p = jnp.exp(s - jnp.max(s, axis=-1, keepdims=True)).astype(BF16)
            ol = _dot(p, jnp.concatenate([mv_ref[0, :, sl], ones], axis=1))
            mem_parts.append((ol[:, :HEAD_DIM] / ol[:, HEAD_DIM:]).astype(BF16))
        o_mem = jnp.concatenate(mem_parts, axis=-1)

        merged = (gates_ref[rows, 0:D].astype(F32) * _dot(ohg_ref[rows, :], whg_ref[...])
                  + gates_ref[rows, D:2 * D].astype(F32) * _dot(oda_ref[rows, :], wda_ref[...])
                  + gates_ref[rows, 2 * D:3 * D].astype(F32) * _dot(o_mem, wmem_ref[...]))
        out_ref[rows, :] = x_ref[rows, :] + _dot(merged.astype(BF16), wout_ref[...])

    _for_row_chunks(x_ref.shape[0], body)


def _mix(x2, o_hg, o_da, mem_q, mem_k, mem_v, gates, w_hg, w_da, w_mem, w_out, L):
    T, D = x2.shape
    tm = WIDE_TOKEN_TILE
    nt = L // tm
    M = mem_k.shape[1]
    const = lambda i: (0, 0)

    def tok(width):
        return pl.BlockSpec((tm, width), lambda i: (i, 0))

    mem_spec = pl.BlockSpec((1, M, mem_k.shape[2]), lambda i: (i // nt, 0, 0))
    return pl.pallas_call(
        _mix_kernel,
        grid=(T // tm,),
        in_specs=[tok(D), tok(o_hg.shape[1]), tok(o_da.shape[1]), tok(mem_q.shape[1]),
                  mem_spec, mem_spec, tok(gates.shape[1]),
                  _resident(w_hg.shape, const), _resident(w_da.shape, const),
                  _resident(w_mem.shape, const), _resident(w_out.shape, const)],
        out_specs=tok(D),
        out_shape=jax.ShapeDtypeStruct((T, D), F32),
        compiler_params=_params("arbitrary"),
        name="mix",
    )(x2, o_hg, o_da, mem_q, mem_k, mem_v, gates, w_hg, w_da, w_mem, w_out)


FFN_TILE = 256


def _ffn_kernel(x_ref, gain_ref, wa_ref, wb_ref, wo_ref, out_ref, h_ref, g_ref):
    d_ff = wa_ref.shape[1]

    def body(c, rows):
        h_ref[...] = _rms_rows(x_ref[rows, :], gain_ref[...]).astype(BF16)
        for j in range(d_ff // FFN_TILE):
            cols = slice(j * FFN_TILE, (j + 1) * FFN_TILE)
            a = _dot(h_ref[...], wa_ref[:, cols])
            b = _dot(h_ref[...], wb_ref[:, cols])
            g_ref[:, cols] = (_silu(a) * b).astype(BF16)
        out_ref[rows, :] = x_ref[rows, :] + _dot(g_ref[...], wo_ref[...])

    _for_row_chunks(x_ref.shape[0], body)


def _ffn(x2, gain, w_in, w_out):
    T, D = x2.shape
    d_ff = w_out.shape[0]
    tm = WIDE_TOKEN_TILE
    return pl.pallas_call(
        _ffn_kernel,
        grid=(T // tm,),
        in_specs=[
            pl.BlockSpec((tm, D), lambda i: (i, 0)),
            pl.BlockSpec((1, D), lambda i: (0, 0)),
            _resident((D, d_ff), lambda i: (0, 0)),
            _resident((D, d_ff), lambda i: (0, 1)),
            _resident((d_ff, D), lambda i: (0, 0)),
        ],
        out_specs=pl.BlockSpec((tm, D), lambda i: (i, 0)),
        out_shape=jax.ShapeDtypeStruct((T, D), F32),
        scratch_shapes=[pltpu.VMEM((ROW_CHUNK, D), BF16), pltpu.VMEM((ROW_CHUNK, d_ff), BF16)],
        compiler_params=_params("arbitrary"),
        name="ffn",
    )(x2, gain, w_in, w_in, w_out)


def kernel(x, mem, norm_mix_gain, norm_mem_gain, w_in, lb_logits_fw, lb_logits_bw, hg_norm_gain,
           da_q_gain, da_k_gain, w_mem_kv, mem_q_gain, mem_k_gain, w_proj_hg, w_proj_da,
           w_proj_mem, w_out, norm_ffn_gain, w_ffn_in, w_ffn_out):
    B, L, D = x.shape
    depth = w_in.shape[0]
    assert depth == 1, "lower-bound tables are evaluated for a single layer"
    T = B * L
    slopes = _alibi_slopes(DA_HEADS)
    hg_cols = 5 * w_proj_hg.shape[1]
    x2 = x.reshape(T, D)
    for l in range(depth):
        mem_k, mem_v = _mem_kv(mem, norm_mem_gain[l][None], w_mem_kv[l].astype(BF16),
                               mem_k_gain[l][None])
        h, hq, lf_fw, k_fw, lf_bw, k_bw, hv, hg = _proj_hgrn(
            x2, norm_mix_gain[l][None], w_in[l, :, :hg_cols].astype(BF16),
            lb_logits_fw.astype(F32), lb_logits_bw.astype(F32))
        (q0, q1, q2, k0, k1, k2, v0, v1, v2, mem_q, gates) = _proj_attn(
            h, w_in[l, :, hg_cols:].astype(BF16), da_q_gain[l][None], da_k_gain[l][None],
            mem_q_gain[l][None], B, L)

        def seq(t):
            return t.reshape(B, L, t.shape[-1])

        o_hg = _hgrn(seq(hq), seq(lf_fw), seq(k_fw), seq(lf_bw), seq(k_bw), seq(hv), seq(hg),
                     hg_norm_gain[l][None])
        o_da = _dilated((q0, q1, q2), (k0, k1, k2), (v0, v1, v2), slopes)

        x2 = _mix(x2, o_hg.reshape(T, -1), o_da.reshape(T, -1), mem_q, mem_k, mem_v, gates,
                  w_proj_hg[l].astype(BF16), w_proj_da[l].astype(BF16),
                  w_proj_mem[l].astype(BF16), w_out[l].astype(BF16), L)
        x2 = _ffn(x2, norm_ffn_gain[l][None], w_ffn_in[l].astype(BF16), w_ffn_out[l].astype(BF16))
    return x2.reshape(B, L, D)
```

```python
import functools

import jax
import jax.numpy as jnp
from jax import lax
from jax.experimental import pallas as pl
from jax.experimental.pallas import tpu as pltpu

RMS_EPS = 1e-6
NEG_INF = -1e30
HEAD_DIM = 128
HG_CHUNK = 128
HG_UNROLL = 16
DA_CONFIGS = ((128, 1), (512, 4), (2048, 16))
DA_DILATIONS = tuple(d for _, d in DA_CONFIGS)
DA_RADIUS = 64
DA_HEADS_PER_GROUP = 4
DA_HEADS = DA_HEADS_PER_GROUP * len(DA_CONFIGS)
MEM_HEADS = 4
GROUP_WIDTH = DA_HEADS_PER_GROUP * HEAD_DIM
ATTN_SCALE = HEAD_DIM ** -0.5

BF16 = jnp.bfloat16
F32 = jnp.float32

VMEM_LIMIT_BYTES = 56 * 1024 * 1024

TOKEN_TILE = 512
WIDE_TOKEN_TILE = 1024
ROW_CHUNK = 256


def _alibi_slopes(n):
    return [float(2.0 ** (-8.0 * (i + 1) / n)) for i in range(n)]


def _sigmoid(x):
    return 1.0 / (1.0 + jnp.exp(-x))


def _silu(x):
    return x * _sigmoid(x)


def _rms_rows(x, gain):
    ms = jnp.mean(x * x, axis=-1, keepdims=True)
    return x * lax.rsqrt(ms + RMS_EPS) * gain


def _head_norm(x, gain, scale):
    parts = []
    for h in range(x.shape[-1] // HEAD_DIM):
        parts.append(_rms_rows(x[:, h * HEAD_DIM:(h + 1) * HEAD_DIM], gain) * scale)
    return jnp.concatenate(parts, axis=-1)


def _dot(a, b):
    return jnp.dot(a, b, preferred_element_type=F32)


def _dot_nt(a, b):
    return lax.dot_general(a, b, (((1,), (1,)), ((), ())), preferred_element_type=F32)


def _dot_tn(a, b):
    return lax.dot_general(a, b, (((0,), (0,)), ((), ())), preferred_element_type=F32)


def _params(*semantics):
    return pltpu.CompilerParams(dimension_semantics=semantics,
                                vmem_limit_bytes=VMEM_LIMIT_BYTES)


def _resident(shape, index_map):
    return pl.BlockSpec(shape, index_map, pipeline_mode=pl.Buffered(1))


def _for_row_chunks(n_rows, body):
    def step(c, carry):
        body(c, pl.ds(pl.multiple_of(c * ROW_CHUNK, ROW_CHUNK), ROW_CHUNK))
        return carry
    lax.fori_loop(0, n_rows // ROW_CHUNK, step, 0, unroll=True)


def _mem_kv_kernel(mem_ref, gain_ref, w_ref, kgain_ref, k_ref, v_ref):
    mem_n = _rms_rows(mem_ref[0], gain_ref[...]).astype(BF16)
    kv = _dot(mem_n, w_ref[...])
    width = k_ref.shape[-1]
    k_ref[0] = _head_norm(kv[:, :width], kgain_ref[...], 1.0).astype(BF16)
    v_ref[0] = kv[:, width:].astype(BF16)


def _mem_kv(mem, gain, w_kv, k_gain):
    B, M, D = mem.shape
    width = w_kv.shape[1] // 2
    return pl.pallas_call(
        _mem_kv_kernel,
        grid=(B,),
        in_specs=[
            pl.BlockSpec((1, M, D), lambda b: (b, 0, 0)),
            pl.BlockSpec((1, D), lambda b: (0, 0)),
            pl.BlockSpec((D, 2 * width), lambda b: (0, 0)),
            pl.BlockSpec((1, HEAD_DIM), lambda b: (0, 0)),
        ],
        out_specs=[
            pl.BlockSpec((1, M, width), lambda b: (b, 0, 0)),
            pl.BlockSpec((1, M, width), lambda b: (b, 0, 0)),
        ],
        out_shape=[jax.ShapeDtypeStruct((B, M, width), BF16)] * 2,
        compiler_params=_params("arbitrary"),
        name="mem_kv",
    )(mem, gain, w_kv, k_gain)


def _lower_bound(logits):
    e = jnp.exp(logits - jnp.max(logits, axis=0, keepdims=True))
    return e[0:1] / jnp.sum(e, axis=0, keepdims=True)


def _proj_hgrn_kernel(x_ref, gain_ref, w_ref, lbfw_ref, lbbw_ref,
                      h_ref, hq_ref, lffw_ref, kfw_ref, lfbw_ref, kbw_ref, hv_ref, hg_ref):
    W = hq_ref.shape[1]

    def section(rows, s):
        return _dot(h_ref[rows, :], w_ref[:, s * W:(s + 1) * W])

    row = lax.broadcasted_iota(jnp.int32, (ROW_CHUNK, ROW_CHUNK), 0)
    col = lax.broadcasted_iota(jnp.int32, (ROW_CHUNK, ROW_CHUNK), 1)
    same_chunk = (row // HG_CHUNK) == (col // HG_CHUNK)
    tri_fw = jnp.where(same_chunk & (row >= col), 1.0, 0.0).astype(BF16)
    tri_bw = jnp.where(same_chunk & (row <= col), 1.0, 0.0).astype(BF16)

    def forget(s, lb_ref, tri, rows, b_ref, k_ref):
        lb = _lower_bound(lb_ref[...])
        f = lb + (1.0 - lb) * _sigmoid(section(rows, s))
        k_ref[rows, :] = (1.0 - f).astype(BF16)
        lf = jnp.log(f)
        lf_hi = lf.astype(BF16)
        lf_lo = (lf - lf_hi.astype(F32)).astype(BF16)
        bb = _dot(tri, jnp.concatenate([lf_hi, lf_lo], axis=1))
        b_ref[rows, :] = bb[:, :W] + bb[:, W:]

    def body(c, rows):
        h_ref[rows, :] = _rms_rows(x_ref[rows, :], gain_ref[...]).astype(BF16)
        hq_ref[rows, :] = (_silu(section(rows, 0)) * ATTN_SCALE).astype(BF16)
        forget(1, lbfw_ref, tri_fw, rows, lffw_ref, kfw_ref)
        forget(2, lbbw_ref, tri_bw, rows, lfbw_ref, kbw_ref)
        hv_ref[rows, :] = section(rows, 3).astype(BF16)
        hg_ref[rows, :] = _silu(section(rows, 4)).astype(BF16)

    _for_row_chunks(x_ref.shape[0], body)


def _proj_hgrn(x2, gain, w_hg, lb_fw, lb_bw):
    T, D = x2.shape
    W = w_hg.shape[1] // 5
    tm = TOKEN_TILE
    const = lambda i: (0, 0)
    tok = lambda width: pl.BlockSpec((tm, width), lambda i: (i, 0))
    outs = [(D, BF16), (W, BF16), (W, F32), (W, BF16), (W, F32), (W, BF16), (W, BF16), (W, BF16)]
    return pl.pallas_call(
        _proj_hgrn_kernel,
        grid=(T // tm,),
        in_specs=[tok(D), pl.BlockSpec((1, D), const), _resident(w_hg.shape, const),
                  pl.BlockSpec(lb_fw.shape, const), pl.BlockSpec(lb_bw.shape, const)],
        out_specs=[tok(w) for w, _ in outs],
        out_shape=[jax.ShapeDtypeStruct((T, w), dt) for w, dt in outs],
        compiler_params=_params("arbitrary"),
        name="proj_hgrn",
    )(x2, gain, w_hg, lb_fw, lb_bw)


def _proj_attn_kernel(h_ref, w_ref, qg_ref, kg_ref, mg_ref,
                      q0_ref, q1_ref, q2_ref, k0_ref, k1_ref, k2_ref, v0_ref, v1_ref, v2_ref,
                      mq_ref, gates_ref, scr_ref):
    nh = GROUP_WIDTH // HEAD_DIM
    gates_col0 = w_ref.shape[1] - gates_ref.shape[1]
    D = gates_ref.shape[1] // 3

    def q_norm(t):
        return _rms_rows(t, qg_ref[...]) * ATTN_SCALE

    def k_norm(t):
        return _rms_rows(t, kg_ref[...])

    def body(c, rows):
        def project(col0, width):
            return _dot(h_ref[rows, :], w_ref[:, col0:col0 + width])

        def class_major(tile, out_ref, fn):
            d = out_ref.shape[1]
            n = ROW_CHUNK // d
            acc = project(tile * GROUP_WIDTH, GROUP_WIDTH)
            for hh in range(nh):
                cols = slice(hh * HEAD_DIM, (hh + 1) * HEAD_DIM)
                vals = fn(acc[:, cols])
                if d == 1:
                    out_ref[0, 0, rows, cols] = vals.astype(BF16)
                    continue
                scr_ref[tile, hh] = vals
                dst = pl.ds(pl.multiple_of(c * n, n), n)
                for r in range(d):
                    out_ref[0, r, dst, cols] = scr_ref[tile, hh, pl.ds(r, n, stride=d), :].astype(BF16)

        for g, (q_ref, k_ref, v_ref) in enumerate(((q0_ref, k0_ref, v0_ref), (q1_ref, k1_ref, v1_ref),
                                                   (q2_ref, k2_ref, v2_ref))):
            class_major(g, q_ref, q_norm)
            class_major(3 + g, k_ref, k_norm)
            class_major(6 + g, v_ref, lambda t: t)
        mq = project(9 * GROUP_WIDTH, GROUP_WIDTH)
        mq_ref[rows, :] = _head_norm(mq, mg_ref[...], ATTN_SCALE).astype(BF16)
        gates_ref[rows, :] = _sigmoid(project(gates_col0, 3 * D)).astype(BF16)

    _for_row_chunks(h_ref.shape[0], body)


def _proj_attn(h, w_at, q_gain, k_gain, mq_gain, B, L):
    T, D = h.shape
    tm = TOKEN_TILE
    nt = L // tm
    n_groups = len(DA_DILATIONS)
    gates_width = w_at.shape[1] - (3 * n_groups + 1) * GROUP_WIDTH
    const = lambda i: (0, 0)

    def cls(d):
        return (pl.BlockSpec((1, d, tm // d, GROUP_WIDTH), lambda i: (i // nt, 0, i % nt, 0)),
                jax.ShapeDtypeStruct((B, d, L // d, GROUP_WIDTH), BF16))

    outs = [cls(d) for _ in range(3) for d in DA_DILATIONS]
    for width in (GROUP_WIDTH, gates_width):
        outs.append((pl.BlockSpec((tm, width), lambda i: (i, 0)),
                     jax.ShapeDtypeStruct((T, width), BF16)))
    gain_spec = pl.BlockSpec((1, HEAD_DIM), const)
    return pl.pallas_call(
        _proj_attn_kernel,
        grid=(T // tm,),
        in_specs=[pl.BlockSpec((tm, D), lambda i: (i, 0)), _resident(w_at.shape, const),
                  gain_spec, gain_spec, gain_spec],
        out_specs=[o[0] for o in outs],
        out_shape=[o[1] for o in outs],
        scratch_shapes=[pltpu.VMEM((3 * n_groups, GROUP_WIDTH // HEAD_DIM, ROW_CHUNK, HEAD_DIM), F32)],
        compiler_params=_params("arbitrary"),
        name="proj_attn",
    )(h, w_at, q_gain, k_gain, mq_gain)


HG_PAIR = 2 * HEAD_DIM


def _head_diag(m):
    return jnp.concatenate([m[:HEAD_DIM, :HEAD_DIM], m[HEAD_DIM:, HEAD_DIM:]], axis=1)


def _hgrn_kernel(q_ref, bfw_ref, kfw_ref, bbw_ref, kbw_ref, v_ref, g_ref, gain_ref,
                 o_ref, acc_ref, sfw_ref, sbw_ref):
    L = q_ref.shape[1]
    C = HG_CHUNK
    n_chunks = L // C
    row = lax.broadcasted_iota(jnp.int32, (C, HG_PAIR), 0)
    col = lax.broadcasted_iota(jnp.int32, (C, HG_PAIR), 1)
    first_head = col < HEAD_DIM
    src = jnp.where(first_head, col, col - HEAD_DIM)
    lower = row >= src
    upper = row <= src

    sfw_ref[...] = jnp.zeros_like(sfw_ref)
    sbw_ref[...] = jnp.zeros_like(sbw_ref)

    def emit(rows, o, second):
        if not second:
            acc_ref[rows, :] = o
            return
        o = o + acc_ref[rows, :]
        o_ref[0, rows, :] = (_head_norm(o, gain_ref[...], 1.0)
                             * g_ref[0, rows, :].astype(F32)).astype(BF16)

    def chunk(c, b_ref, k_ref, s, second, causal, mid_row, end_row):
        rows = pl.ds(pl.multiple_of(c * C, C), C)
        b = b_ref[0, rows, :]
        b_mid = b[mid_row:mid_row + 1, :]
        b_end = b[end_row:end_row + 1, :]
        q_in = q_ref[0, rows, :].astype(F32) * jnp.exp(b - b_mid)
        k_in = k_ref[0, rows, :].astype(F32) * jnp.exp(b_mid - b)
        q_st = (q_in * jnp.exp(b_mid)).astype(BF16)
        k_st = (k_in * jnp.exp(b_end - b_mid)).astype(BF16)
        v = v_ref[0, rows, :]
        k_bf = k_in.astype(BF16)
        zero = jnp.zeros_like(k_bf)
        k_diag = jnp.concatenate([jnp.where(first_head, k_bf, zero),
                                  jnp.where(first_head, zero, k_bf)], axis=0)
        a = _dot_nt(q_in.astype(BF16), k_diag)
        a = jnp.where(causal, a, 0.0).astype(BF16)
        lhs = jnp.concatenate(
            [jnp.concatenate([q_st[:, :HEAD_DIM], a[:, :HEAD_DIM]], axis=1),
             jnp.concatenate([q_st[:, HEAD_DIM:], a[:, HEAD_DIM:]], axis=1)], axis=0)
        rhs = jnp.concatenate([s.astype(BF16), v], axis=0)
        emit(rows, _head_diag(_dot(lhs, rhs)), second)
        d = jnp.exp(b_end)
        d_col = jnp.concatenate(
            [jnp.broadcast_to(d[:, :HEAD_DIM], (HEAD_DIM, HEAD_DIM)).T,
             jnp.broadcast_to(d[:, HEAD_DIM:], (HEAD_DIM, HEAD_DIM)).T], axis=1)
        return s * d_col + _head_diag(_dot_tn(k_st, v))

    def body(n, carry, second):
        s_fw = sfw_ref[...]
        s_bw = sbw_ref[...]
        for u in range(HG_UNROLL):
            c = n * HG_UNROLL + u
            s_fw = chunk(c, bfw_ref, kfw_ref, s_fw, second, lower, C // 2 - 1, C - 1)
            s_bw = chunk(n_chunks - 1 - c, bbw_ref, kbw_ref, s_bw, second, upper, C // 2, 0)
        sfw_ref[...] = s_fw
        sbw_ref[...] = s_bw
        return carry

    trips = n_chunks // HG_UNROLL
    lax.fori_loop(0, trips // 2, functools.partial(body, second=False), 0)
    lax.fori_loop(trips // 2, trips, functools.partial(body, second=True), 0)


def _hgrn(hq, b_fw, k_fw, b_bw, k_bw, hv, hg, gain):
    B, L, W = hq.shape
    spec = pl.BlockSpec((1, L, HG_PAIR), lambda b, h: (b, 0, h))
    return pl.pallas_call(
        _hgrn_kernel,
        grid=(B, W // HG_PAIR),
        in_specs=[spec] * 7 + [pl.BlockSpec((1, HEAD_DIM), lambda b, h: (0, 0))],
        out_specs=spec,
        out_shape=jax.ShapeDtypeStruct((B, L, W), BF16),
        scratch_shapes=[pltpu.VMEM((L, HG_PAIR), F32),
                        pltpu.VMEM((HEAD_DIM, HG_PAIR), F32), pltpu.VMEM((HEAD_DIM, HG_PAIR), F32)],
        compiler_params=_params("arbitrary", "arbitrary"),
        name="hgrn2",
    )(hq, b_fw, k_fw, b_bw, k_bw, hv, hg, gain)


DA_TQ = 2048
DA_BQ = 128
DA_BK = DA_BQ + 2 * DA_RADIUS
DA_BLOCKS = DA_TQ // DA_BQ
DA_MERGE_ROWS = 256
DA_PAIR = 1


def _dilated_kernel(q0_ref, q1_ref, q2_ref, k0_ref, k1_ref, k2_ref, v0_ref, v1_ref, v2_ref,
                    o_ref, og_ref, lse_ref, bias_ref, *, slopes):
    pair = pl.program_id(1)
    t = pl.program_id(2)
    bq, bk = DA_BQ, DA_BK
    groups = tuple(zip(DA_DILATIONS, (q0_ref, q1_ref, q2_ref), (k0_ref, k1_ref, k2_ref),
                       (v0_ref, v1_ref, v2_ref)))

    @pl.when(t == 0)
    def _():
        row = lax.broadcasted_iota(jnp.int32, (bq, bk), 0)
        col = lax.broadcasted_iota(jnp.int32, (bq, bk), 1)
        n_pairs = DA_HEADS_PER_GROUP // DA_PAIR
        for g, d in enumerate(DA_DILATIONS):
            for hh in range(DA_PAIR):
                base = g * DA_HEADS_PER_GROUP + hh
                slope = jnp.float32(slopes[base])
                for p in range(1, n_pairs):
                    slope = jnp.where(pair == p, jnp.float32(slopes[base + p * DA_PAIR]), slope)
                slope = slope * float(d)
                for variant, offset in enumerate((-DA_RADIUS, 0, -2 * DA_RADIUS)):
                    dist = jnp.abs(col - row + offset)
                    bias_ref[g, variant, :, hh * bk:(hh + 1) * bk] = jnp.where(
                        dist <= DA_RADIUS, -slope * dist.astype(F32), NEG_INF)

    ones = jnp.ones((bk, HEAD_DIM), BF16)
    lane_head = lax.broadcasted_iota(jnp.int32, (bk, DA_PAIR * HEAD_DIM), 1) // HEAD_DIM

    def block(idx, g, d, q_ref, k_ref, v_ref):
        Ld = k_ref.shape[2]
        per_class = DA_TQ // d // bq
        r = 0 if d == 1 else idx // per_class
        i = idx if d == 1 else idx % per_class
        n0 = t * (DA_TQ // d) + i * bq
        k0 = pl.multiple_of(jnp.clip(n0 - DA_RADIUS, 0, Ld - bk), DA_RADIUS)
        variant = jnp.where(n0 == 0, 1, jnp.where(n0 == Ld - bq, 2, 0))
        q = q_ref[0, r, pl.ds(pl.multiple_of(i * bq, bq), bq), :]
        k = k_ref[0, r, pl.ds(k0, bk), :]
        v = v_ref[0, r, pl.ds(k0, bk), :]
        if DA_PAIR == 1:
            k_diag = k
        else:
            zero = jnp.zeros_like(k)
            k_diag = jnp.concatenate([jnp.where(lane_head == hh, k, zero)
                                      for hh in range(DA_PAIR)], axis=0)
        s2 = _dot_nt(q, k_diag) + bias_ref[g, variant]
        if d == 1:
            dst = pl.ds(pl.multiple_of(i * bq, bq), bq)
        else:
            dst = pl.ds(i * (bq * d) + r, bq, stride=d)
        for hh in range(DA_PAIR):
            s = s2[:, hh * bk:(hh + 1) * bk]
            m = jnp.max(s, axis=-1, keepdims=True)
            p = jnp.exp(s - m).astype(BF16)
            ol = _dot(p, jnp.concatenate([v[:, hh * HEAD_DIM:(hh + 1) * HEAD_DIM], ones], axis=1))
            l = ol[:, HEAD_DIM:]
            og_ref[g, hh, dst, :] = ol[:, :HEAD_DIM] / l
            lse_ref[g, hh, dst, :] = m + jnp.log(l)

    def body(idx, carry):
        for g, (d, q_ref, k_ref, v_ref) in enumerate(groups):
            block(idx, g, d, q_ref, k_ref, v_ref)
        return carry

    lax.fori_loop(0, DA_BLOCKS, body, 0, unroll=True)

    for hh in range(DA_PAIR):
        for c in range(DA_TQ // DA_MERGE_ROWS):
            rows = slice(c * DA_MERGE_ROWS, (c + 1) * DA_MERGE_ROWS)
            ls = [lse_ref[g, hh, rows, :] for g in range(len(groups))]
            m = jnp.maximum(jnp.maximum(ls[0], ls[1]), ls[2])
            es = [jnp.exp(x - m) for x in ls]
            inv = 1.0 / (es[0] + es[1] + es[2])
            acc = (es[0] * inv) * og_ref[0, hh, rows, :]
            for g in range(1, len(groups)):
                acc = acc + (es[g] * inv) * og_ref[g, hh, rows, :]
            o_ref[0, rows, hh * HEAD_DIM:(hh + 1) * HEAD_DIM] = acc.astype(BF16)


def _dilated(qs, ks, vs, slopes):
    B, _, L, W = qs[0].shape
    width = DA_PAIR * HEAD_DIM
    assert L % DA_TQ == 0 and all(L // d >= DA_BK and DA_TQ // d >= DA_BQ for d in DA_DILATIONS)

    def q_spec(d):
        return pl.BlockSpec((1, d, DA_TQ // d, width), lambda b, h, t: (b, 0, t, h))

    def kv_spec(d):
        return pl.BlockSpec((1, d, L // d, width), lambda b, h, t: (b, 0, 0, h))

    n_groups = len(DA_DILATIONS)
    return pl.pallas_call(
        functools.partial(_dilated_kernel, slopes=tuple(slopes)),
        grid=(B, W // width, L // DA_TQ),
        in_specs=([q_spec(d) for d in DA_DILATIONS] + [kv_spec(d) for d in DA_DILATIONS] * 2),
        out_specs=pl.BlockSpec((1, DA_TQ, width), lambda b, h, t: (b, t, h)),
        out_shape=jax.ShapeDtypeStruct((B, L, W), BF16),
        scratch_shapes=[pltpu.VMEM((n_groups, DA_PAIR, DA_TQ, HEAD_DIM), F32),
                        pltpu.VMEM((n_groups, DA_PAIR, DA_TQ, HEAD_DIM), F32),
                        pltpu.VMEM((n_groups, 3, DA_BQ, DA_PAIR * DA_BK), F32)],
        compiler_params=_params("arbitrary", "arbitrary", "arbitrary"),
        name="dilated",
    )(*qs, *ks, *vs)


def _mix_kernel(x_ref, ohg_ref, oda_ref, mq_ref, mk_ref, mv_ref, gates_ref,
                whg_ref, wda_ref, wmem_ref, wout_ref, out_ref):
    D = x_ref.shape[-1]

    def body(c, rows):
        mem_parts = []
        ones = jnp.ones((mk_ref.shape[1], HEAD_DIM), BF16)
        for h in range(MEM_HEADS):
            sl = slice(h * HEAD_DIM, (h + 1) * HEAD_DIM)
            s = _dot_nt(mq_ref[rows, sl], mk_ref[0, :, sl])
            p = jnp.exp(s - jnp.max(s, axis=-1, keepdims=True)).astype(BF16)
            ol = _dot(p, jnp.concatenate([mv_ref[0, :, sl], ones], axis=1))
            mem_parts.append((ol[:, :HEAD_DIM] / ol[:, HEAD_DIM:]).astype(BF16))
        o_mem = jnp.concatenate(mem_parts, axis=-1)

        merged = (gates_ref[rows, 0:D].astype(F32) * _dot(ohg_ref[rows, :], whg_ref[...])
                  + gates_ref[rows, D:2 * D].astype(F32) * _dot(oda_ref[rows, :], wda_ref[...])
                  + gates_ref[rows, 2 * D:3 * D].astype(F32) * _dot(o_mem, wmem_ref[...]))
        out_ref[rows, :] = x_ref[rows, :] + _dot(merged.astype(BF16), wout_ref[...])

    _for_row_chunks(x_ref.shape[0], body)


def _mix(x2, o_hg, o_da, mem_q, mem_k, mem_v, gates, w_hg, w_da, w_mem, w_out, L):
    T, D = x2.shape
    tm = WIDE_TOKEN_TILE
    nt = L // tm
    M = mem_k.shape[1]
    const = lambda i: (0, 0)

    def tok(width):
        return pl.BlockSpec((tm, width), lambda i: (i, 0))

    mem_spec = pl.BlockSpec((1, M, mem_k.shape[2]), lambda i: (i // nt, 0, 0))
    return pl.pallas_call(
        _mix_kernel,
        grid=(T // tm,),
        in_specs=[tok(D), tok(o_hg.shape[1]), tok(o_da.shape[1]), tok(mem_q.shape[1]),
                  mem_spec, mem_spec, tok(gates.shape[1]),
                  _resident(w_hg.shape, const), _resident(w_da.shape, const),
                  _resident(w_mem.shape, const), _resident(w_out.shape, const)],
        out_specs=tok(D),
        out_shape=jax.ShapeDtypeStruct((T, D), F32),
        compiler_params=_params("arbitrary"),
        name="mix",
    )(x2, o_hg, o_da, mem_q, mem_k, mem_v, gates, w_hg, w_da, w_mem, w_out)


FFN_TILE = 256


def _ffn_kernel(x_ref, gain_ref, wa_ref, wb_ref, wo_ref, out_ref, h_ref, g_ref):
    d_ff = wa_ref.shape[1]

    def body(c, rows):
        h_ref[...] = _rms_rows(x_ref[rows, :], gain_ref[...]).astype(BF16)
        for j in range(d_ff // FFN_TILE):
            cols = slice(j * FFN_TILE, (j + 1) * FFN_TILE)
            a = _dot(h_ref[...], wa_ref[:, cols])
            b = _dot(h_ref[...], wb_ref[:, cols])
            g_ref[:, cols] = (_silu(a) * b).astype(BF16)
        out_ref[rows, :] = x_ref[rows, :] + _dot(g_ref[...], wo_ref[...])

    _for_row_chunks(x_ref.shape[0], body)


def _ffn(x2, gain, w_in, w_out):
    T, D = x2.shape
    d_ff = w_out.shape[0]
    tm = WIDE_TOKEN_TILE
    return pl.pallas_call(
        _ffn_kernel,
        grid=(T // tm,),
        in_specs=[
            pl.BlockSpec((tm, D), lambda i: (i, 0)),
            pl.BlockSpec((1, D), lambda i: (0, 0)),
            _resident((D, d_ff), lambda i: (0, 0)),
            _resident((D, d_ff), lambda i: (0, 1)),
            _resident((d_ff, D), lambda i: (0, 0)),
        ],
        out_specs=pl.BlockSpec((tm, D), lambda i: (i, 0)),
        out_shape=jax.ShapeDtypeStruct((T, D), F32),
        scratch_shapes=[pltpu.VMEM((ROW_CHUNK, D), BF16), pltpu.VMEM((ROW_CHUNK, d_ff), BF16)],
        compiler_params=_params("arbitrary"),
        name="ffn",
    )(x2, gain, w_in, w_in, w_out)


def kernel(x, mem, norm_mix_gain, norm_mem_gain, w_in, lb_logits_fw, lb_logits_bw, hg_norm_gain,
           da_q_gain, da_k_gain, w_mem_kv, mem_q_gain, mem_k_gain, w_proj_hg, w_proj_da,
           w_proj_mem, w_out, norm_ffn_gain, w_ffn_in, w_ffn_out):
    B, L, D = x.shape
    depth = w_in.shape[0]
    assert depth == 1, "lower-bound tables are evaluated for a single layer"
    T = B * L
    slopes = _alibi_slopes(DA_HEADS)
    hg_cols = 5 * w_proj_hg.shape[1]
    x2 = x.reshape(T, D)
    for l in range(depth):
        mem_k, mem_v = _mem_kv(mem, norm_mem_gain[l][None], w_mem_kv[l].astype(BF16),
                               mem_k_gain[l][None])
        h, hq, lf_fw, k_fw, lf_bw, k_bw, hv, hg = _proj_hgrn(
            x2, norm_mix_gain[l][None], w_in[l, :, :hg_cols].astype(BF16),
            lb_logits_fw.astype(F32), lb_logits_bw.astype(F32))
        (q0, q1, q2, k0, k1, k2, v0, v1, v2, mem_q, gates) = _proj_attn(
            h, w_in[l, :, hg_cols:].astype(BF16), da_q_gain[l][None], da_k_gain[l][None],
            mem_q_gain[l][None], B, L)

        def seq(t):
            return t.reshape(B, L, t.shape[-1])

        o_hg = _hgrn(seq(hq), seq(lf_fw), seq(k_fw), seq(lf_bw), seq(k_bw), seq(hv), seq(hg),
                     hg_norm_gain[l][None])
        o_da = _dilated((q0, q1, q2), (k0, k1, k2), (v0, v1, v2), slopes)

        x2 = _mix(x2, o_hg.reshape(T, -1), o_da.reshape(T, -1), mem_q, mem_k, mem_v, gates,
                  w_proj_hg[l].astype(BF16), w_proj_da[l].astype(BF16),
                  w_proj_mem[l].astype(BF16), w_out[l].astype(BF16), L)
        x2 = _ffn(x2, norm_ffn_gain[l][None], w_ffn_in[l].astype(BF16), w_ffn_out[l].astype(BF16))
    return x2.reshape(B, L, D)
```

```python
import functools

import jax
import jax.numpy as jnp
from jax import lax
from jax.experimental import pallas as pl
from jax.experimental.pallas import tpu as pltpu

RMS_EPS = 1e-6
NEG_INF = -1e30
HEAD_DIM = 128
HG_CHUNK = 128
HG_UNROLL = 16
DA_CONFIGS = ((128, 1), (512, 4), (2048, 16))
DA_DILATIONS = tuple(d for _, d in DA_CONFIGS)
DA_RADIUS = 64
DA_HEADS_PER_GROUP = 4
DA_HEADS = DA_HEADS_PER_GROUP * len(DA_CONFIGS)
MEM_HEADS = 4
GROUP_WIDTH = DA_HEADS_PER_GROUP * HEAD_DIM
ATTN_SCALE = HEAD_DIM ** -0.5

BF16 = jnp.bfloat16
F32 = jnp.float32

VMEM_LIMIT_BYTES = 56 * 1024 * 1024

TOKEN_TILE = 512
WIDE_TOKEN_TILE = 1024
ROW_CHUNK = 256


def _alibi_slopes(n):
    return [float(2.0 ** (-8.0 * (i + 1) / n)) for i in range(n)]


def _sigmoid(x):
    return 1.0 / (1.0 + jnp.exp(-x))


def _silu(x):
    return x * _sigmoid(x)


def _rms_rows(x, gain):
    ms = jnp.mean(x * x, axis=-1, keepdims=True)
    return x * lax.rsqrt(ms + RMS_EPS) * gain


def _head_norm(x, gain, scale):
    parts = []
    for h in range(x.shape[-1] // HEAD_DIM):
        parts.append(_rms_rows(x[:, h * HEAD_DIM:(h + 1) * HEAD_DIM], gain) * scale)
    return jnp.concatenate(parts, axis=-1)


def _dot(a, b):
    return jnp.dot(a, b, preferred_element_type=F32)


def _dot_nt(a, b):
    return lax.dot_general(a, b, (((1,), (1,)), ((), ())), preferred_element_type=F32)


def _dot_tn(a, b):
    return lax.dot_general(a, b, (((0,), (0,)), ((), ())), preferred_element_type=F32)


def _params(*semantics):
    return pltpu.CompilerParams(dimension_semantics=semantics,
                                vmem_limit_bytes=VMEM_LIMIT_BYTES)


def _resident(shape, index_map):
    return pl.BlockSpec(shape, index_map, pipeline_mode=pl.Buffered(1))


def _for_row_chunks(n_rows, body):
    def step(c, carry):
        body(c, pl.ds(pl.multiple_of(c * ROW_CHUNK, ROW_CHUNK), ROW_CHUNK))
        return carry
    lax.fori_loop(0, n_rows // ROW_CHUNK, step, 0, unroll=True)


def _mem_kv_kernel(mem_ref, gain_ref, w_ref, kgain_ref, k_ref, v_ref):
    mem_n = _rms_rows(mem_ref[0], gain_ref[...]).astype(BF16)
    kv = _dot(mem_n, w_ref[...])
    width = k_ref.shape[-1]
    k_ref[0] = _head_norm(kv[:, :width], kgain_ref[...], 1.0).astype(BF16)
    v_ref[0] = kv[:, width:].astype(BF16)


def _mem_kv(mem, gain, w_kv, k_gain):
    B, M, D = mem.shape
    width = w_kv.shape[1] // 2
    return pl.pallas_call(
        _mem_kv_kernel,
        grid=(B,),
        in_specs=[
            pl.BlockSpec((1, M, D), lambda b: (b, 0, 0)),
            pl.BlockSpec((1, D), lambda b: (0, 0)),
            pl.BlockSpec((D, 2 * width), lambda b: (0, 0)),
            pl.BlockSpec((1, HEAD_DIM), lambda b: (0, 0)),
        ],
        out_specs=[
            pl.BlockSpec((1, M, width), lambda b: (b, 0, 0)),
            pl.BlockSpec((1, M, width), lambda b: (b, 0, 0)),
        ],
        out_shape=[jax.ShapeDtypeStruct((B, M, width), BF16)] * 2,
        compiler_params=_params("arbitrary"),
        name="mem_kv",
    )(mem, gain, w_kv, k_gain)


def _lower_bound(logits):
    e = jnp.exp(logits - jnp.max(logits, axis=0, keepdims=True))
    return e[0:1] / jnp.sum(e, axis=0, keepdims=True)


def _proj_hgrn_kernel(x_ref, gain_ref, w_ref, lbfw_ref, lbbw_ref,
                      h_ref, hq_ref, lffw_ref, kfw_ref, lfbw_ref, kbw_ref, hv_ref, hg_ref):
    W = hq_ref.shape[1]

    def section(rows, s):
        return _dot(h_ref[rows, :], w_ref[:, s * W:(s + 1) * W])

    row = lax.broadcasted_iota(jnp.int32, (ROW_CHUNK, ROW_CHUNK), 0)
    col = lax.broadcasted_iota(jnp.int32, (ROW_CHUNK, ROW_CHUNK), 1)
    same_chunk = (row // HG_CHUNK) == (col // HG_CHUNK)
    tri_fw = jnp.where(same_chunk & (row >= col), 1.0, 0.0).astype(BF16)
    tri_bw = jnp.where(same_chunk & (row <= col), 1.0, 0.0).astype(BF16)

    def forget(s, lb_ref, tri, rows, b_ref, k_ref):
        lb = _lower_bound(lb_ref[...])
        f = lb + (1.0 - lb) * _sigmoid(section(rows, s))
        k_ref[rows, :] = (1.0 - f).astype(BF16)
        lf = jnp.log(f)
        lf_hi = lf.astype(BF16)
        lf_lo = (lf - lf_hi.astype(F32)).astype(BF16)
        bb = _dot(tri, jnp.concatenate([lf_hi, lf_lo], axis=1))
        b_ref[rows, :] = bb[:, :W] + bb[:, W:]

    def body(c, rows):
        h_ref[rows, :] = _rms_rows(x_ref[rows, :], gain_ref[...]).astype(BF16)
        hq_ref[rows, :] = (_silu(section(rows, 0)) * ATTN_SCALE).astype(BF16)
        forget(1, lbfw_ref, tri_fw, rows, lffw_ref, kfw_ref)
        forget(2, lbbw_ref, tri_bw, rows, lfbw_ref, kbw_ref)
        hv_ref[rows, :] = section(rows, 3).astype(BF16)
        hg_ref[rows, :] = _silu(section(rows, 4)).astype(BF16)

    _for_row_chunks(x_ref.shape[0], body)


def _proj_hgrn(x2, gain, w_hg, lb_fw, lb_bw):
    T, D = x2.shape
    W = w_hg.shape[1] // 5
    tm = TOKEN_TILE
    const = lambda i: (0, 0)
    tok = lambda width: pl.BlockSpec((tm, width), lambda i: (i, 0))
    outs = [(D, BF16), (W, BF16), (W, F32), (W, BF16), (W, F32), (W, BF16), (W, BF16), (W, BF16)]
    return pl.pallas_call(
        _proj_hgrn_kernel,
        grid=(T // tm,),
        in_specs=[tok(D), pl.BlockSpec((1, D), const), _resident(w_hg.shape, const),
                  pl.BlockSpec(lb_fw.shape, const), pl.BlockSpec(lb_bw.shape, const)],
        out_specs=[tok(w) for w, _ in outs],
        out_shape=[jax.ShapeDtypeStruct((T, w), dt) for w, dt in outs],
        compiler_params=_params("arbitrary"),
        name="proj_hgrn",
    )(x2, gain, w_hg, lb_fw, lb_bw)


def _proj_attn_kernel(h_ref, w_ref, qg_ref, kg_ref, mg_ref,
                      q0_ref, q1_ref, q2_ref, k0_ref, k1_ref, k2_ref, v0_ref, v1_ref, v2_ref,
                      mq_ref, gates_ref, scr_ref):
    nh = GROUP_WIDTH // HEAD_DIM
    gates_col0 = w_ref.shape[1] - gates_ref.shape[1]
    D = gates_ref.shape[1] // 3

    def q_norm(t):
        return _rms_rows(t, qg_ref[...]) * ATTN_SCALE

    def k_norm(t):
        return _rms_rows(t, kg_ref[...])

    def body(c, rows):
        def project(col0, width):
            return _dot(h_ref[rows, :], w_ref[:, col0:col0 + width])

        def class_major(tile, out_ref, fn):
            d = out_ref.shape[1]
            n = ROW_CHUNK // d
            acc = project(tile * GROUP_WIDTH, GROUP_WIDTH)
            for hh in range(nh):
                cols = slice(hh * HEAD_DIM, (hh + 1) * HEAD_DIM)
                vals = fn(acc[:, cols])
                if d == 1:
                    out_ref[0, 0, rows, cols] = vals.astype(BF16)
                    continue
                scr_ref[tile, hh] = vals
                dst = pl.ds(pl.multiple_of(c * n, n), n)
                for r in range(d):
                    out_ref[0, r, dst, cols] = scr_ref[tile, hh, pl.ds(r, n, stride=d), :].astype(BF16)

        for g, (q_ref, k_ref, v_ref) in enumerate(((q0_ref, k0_ref, v0_ref), (q1_ref, k1_ref, v1_ref),
                                                   (q2_ref, k2_ref, v2_ref))):
            class_major(g, q_ref, q_norm)
            class_major(3 + g, k_ref, k_norm)
            class_major(6 + g, v_ref, lambda t: t)
        mq = project(9 * GROUP_WIDTH, GROUP_WIDTH)
        mq_ref[rows, :] = _head_norm(mq, mg_ref[...], ATTN_SCALE).astype(BF16)
        gates_ref[rows, :] = _sigmoid(project(gates_col0, 3 * D)).astype(BF16)

    _for_row_chunks(h_ref.shape[0], body)


def _proj_attn(h, w_at, q_gain, k_gain, mq_gain, B, L):
    T, D = h.shape
    tm = TOKEN_TILE
    nt = L // tm
    n_groups = len(DA_DILATIONS)
    gates_width = w_at.shape[1] - (3 * n_groups + 1) * GROUP_WIDTH
    const = lambda i: (0, 0)

    def cls(d):
        return (pl.BlockSpec((1, d, tm // d, GROUP_WIDTH), lambda i: (i // nt, 0, i % nt, 0)),
                jax.ShapeDtypeStruct((B, d, L // d, GROUP_WIDTH), BF16))

    outs = [cls(d) for _ in range(3) for d in DA_DILATIONS]
    for width in (GROUP_WIDTH, gates_width):
        outs.append((pl.BlockSpec((tm, width), lambda i: (i, 0)),
                     jax.ShapeDtypeStruct((T, width), BF16)))
    gain_spec = pl.BlockSpec((1, HEAD_DIM), const)
    return pl.pallas_call(
        _proj_attn_kernel,
        grid=(T // tm,),
        in_specs=[pl.BlockSpec((tm, D), lambda i: (i, 0)), _resident(w_at.shape, const),
                  gain_spec, gain_spec, gain_spec],
        out_specs=[o[0] for o in outs],
        out_shape=[o[1] for o in outs],
        scratch_shapes=[pltpu.VMEM((3 * n_groups, GROUP_WIDTH // HEAD_DIM, ROW_CHUNK, HEAD_DIM), F32)],
        compiler_params=_params("arbitrary"),
        name="proj_attn",
    )(h, w_at, q_gain, k_gain, mq_gain)


HG_PAIR = 2 * HEAD_DIM


def _head_diag(m):
    return jnp.concatenate([m[:HEAD_DIM, :HEAD_DIM], m[HEAD_DIM:, HEAD_DIM:]], axis=1)


def _hgrn_kernel(q_ref, bfw_ref, kfw_ref, bbw_ref, kbw_ref, v_ref, g_ref, gain_ref,
                 o_ref, acc_ref, sfw_ref, sbw_ref):
    L = q_ref.shape[1]
    C = HG_CHUNK
    n_chunks = L // C
    row = lax.broadcasted_iota(jnp.int32, (C, HG_PAIR), 0)
    col = lax.broadcasted_iota(jnp.int32, (C, HG_PAIR), 1)
    first_head = col < HEAD_DIM
    src = jnp.where(first_head, col, col - HEAD_DIM)
    lower = row >= src
    upper = row <= src

    sfw_ref[...] = jnp.zeros_like(sfw_ref)
    sbw_ref[...] = jnp.zeros_like(sbw_ref)

    def emit(rows, o, second):
        if not second:
            acc_ref[rows, :] = o
            return
        o = o + acc_ref[rows, :]
        o_ref[0, rows, :] = (_head_norm(o, gain_ref[...], 1.0)
                             * g_ref[0, rows, :].astype(F32)).astype(BF16)

    def chunk(c, b_ref, k_ref, s, second, causal, mid_row, end_row):
        rows = pl.ds(pl.multiple_of(c * C, C), C)
        b = b_ref[0, rows, :]
        b_mid = b[mid_row:mid_row + 1, :]
        b_end = b[end_row:end_row + 1, :]
        q_in = q_ref[0, rows, :].astype(F32) * jnp.exp(b - b_mid)
        k_in = k_ref[0, rows, :].astype(F32) * jnp.exp(b_mid - b)
        q_st = (q_in * jnp.exp(b_mid)).astype(BF16)
        k_st = (k_in * jnp.exp(b_end - b_mid)).astype(BF16)
        v = v_ref[0, rows, :]
        k_bf = k_in.astype(BF16)
        zero = jnp.zeros_like(k_bf)
        k_diag = jnp.concatenate([jnp.where(first_head, k_bf, zero),
                                  jnp.where(first_head, zero, k_bf)], axis=0)
        a = _dot_nt(q_in.astype(BF16), k_diag)
        a = jnp.where(causal, a, 0.0).astype(BF16)
        lhs = jnp.concatenate(
            [jnp.concatenate([q_st[:, :HEAD_DIM], a[:, :HEAD_DIM]], axis=1),
             jnp.concatenate([q_st[:, HEAD_DIM:], a[:, HEAD_DIM:]], axis=1)], axis=0)
        rhs = jnp.concatenate([s.astype(BF16), v], axis=0)
        emit(rows, _head_diag(_dot(lhs, rhs)), second)
        d = jnp.exp(b_end)
        d_col = jnp.concatenate(
            [jnp.broadcast_to(d[:, :HEAD_DIM], (HEAD_DIM, HEAD_DIM)).T,
             jnp.broadcast_to(d[:, HEAD_DIM:], (HEAD_DIM, HEAD_DIM)).T], axis=1)
        return s * d_col + _head_diag(_dot_tn(k_st, v))

    def body(n, carry, second):
        s_fw = sfw_ref[...]
        s_bw = sbw_ref[...]
        for u in range(HG_UNROLL):
            c = n * HG_UNROLL + u
            s_fw = chunk(c, bfw_ref, kfw_ref, s_fw, second, lower, C // 2 - 1, C - 1)
            s_bw = chunk(n_chunks - 1 - c, bbw_ref, kbw_ref, s_bw, second, upper, C // 2, 0)
        sfw_ref[...] = s_fw
        sbw_ref[...] = s_bw
        return carry

    trips = n_chunks // HG_UNROLL
    lax.fori_loop(0, trips // 2, functools.partial(body, second=False), 0)
    lax.fori_loop(trips // 2, trips, functools.partial(body, second=True), 0)


def _hgrn(hq, b_fw, k_fw, b_bw, k_bw, hv, hg, gain):
    B, L, W = hq.shape
    spec = pl.BlockSpec((1, L, HG_PAIR), lambda b, h: (b, 0, h))
    return pl.pallas_call(
        _hgrn_kernel,
        grid=(B, W // HG_PAIR),
        in_specs=[spec] * 7 + [pl.BlockSpec((1, HEAD_DIM), lambda b, h: (0, 0))],
        out_specs=spec,
        out_shape=jax.ShapeDtypeStruct((B, L, W), BF16),
        scratch_shapes=[pltpu.VMEM((L, HG_PAIR), F32),
                        pltpu.VMEM((HEAD_DIM, HG_PAIR), F32), pltpu.VMEM((HEAD_DIM, HG_PAIR), F32)],
        compiler_params=_params("arbitrary", "arbitrary"),
        name="hgrn2",
    )(hq, b_fw, k_fw, b_bw, k_bw, hv, hg, gain)


DA_TQ = 4096
DA_BQ = 128
DA_BK = DA_BQ + 2 * DA_RADIUS
DA_BLOCKS = DA_TQ // DA_BQ
DA_MERGE_ROWS = 256


def _dilated_kernel(q0_ref, q1_ref, q2_ref, k0_ref, k1_ref, k2_ref, v0_ref, v1_ref, v2_ref,
                    o_ref, og_ref, lse_ref, bias_ref, *, slopes):
    h = pl.program_id(1)
    t = pl.program_id(2)
    bq, bk = DA_BQ, DA_BK
    groups = tuple(zip(DA_DILATIONS, (q0_ref, q1_ref, q2_ref), (k0_ref, k1_ref, k2_ref),
                       (v0_ref, v1_ref, v2_ref)))

    @pl.when(t == 0)
    def _():
        row = lax.broadcasted_iota(jnp.int32, (bq, bk), 0)
        col = lax.broadcasted_iota(jnp.int32, (bq, bk), 1)
        for g, d in enumerate(DA_DILATIONS):
            slope = jnp.float32(slopes[g * DA_HEADS_PER_GROUP])
            for i in range(1, DA_HEADS_PER_GROUP):
                slope = jnp.where(h == i, jnp.float32(slopes[g * DA_HEADS_PER_GROUP + i]), slope)
            slope = slope * float(d)
            for variant, offset in enumerate((-DA_RADIUS, 0, -2 * DA_RADIUS)):
                dist = jnp.abs(col - row + offset)
                bias_ref[g, variant] = jnp.where(dist <= DA_RADIUS, -slope * dist.astype(F32),
                                                 NEG_INF)

    ones = jnp.ones((bk, HEAD_DIM), BF16)

    def block(idx, g, d, q_ref, k_ref, v_ref):
        Ld = k_ref.shape[2]
        per_class = DA_TQ // d // bq
        r = 0 if d == 1 else idx // per_class
        i = idx if d == 1 else idx % per_class
        n0 = t * (DA_TQ // d) + i * bq
        k0 = pl.multiple_of(jnp.clip(n0 - DA_RADIUS, 0, Ld - bk), DA_RADIUS)
        variant = jnp.where(n0 == 0, 1, jnp.where(n0 == Ld - bq, 2, 0))
        q = q_ref[0, r, pl.ds(pl.multiple_of(i * bq, bq), bq), :]
        k = k_ref[0, r, pl.ds(k0, bk), :]
        v = v_ref[0, r, pl.ds(k0, bk), :]
        s = _dot_nt(q, k) + bias_ref[g, variant]
        m = jnp.max(s, axis=-1, keepdims=True)
        p = jnp.exp(s - m).astype(BF16)
        ol = _dot(p, jnp.concatenate([v, ones], axis=1))
        l = ol[:, HEAD_DIM:]
        if d == 1:
            dst = pl.ds(pl.multiple_of(i * bq, bq), bq)
        else:
            dst = pl.ds(i * (bq * d) + r, bq, stride=d)
        og_ref[g, dst, :] = ol[:, :HEAD_DIM] / l
        lse_ref[g, dst, :] = m + jnp.log(l)

    def body(idx, carry):
        for g, (d, q_ref, k_ref, v_ref) in enumerate(groups):
            block(idx, g, d, q_ref, k_ref, v_ref)
        return carry

    lax.fori_loop(0, DA_BLOCKS, body, 0, unroll=True)

    for c in range(DA_TQ // DA_MERGE_ROWS):
        rows = slice(c * DA_MERGE_ROWS, (c + 1) * DA_MERGE_ROWS)
        ls = [lse_ref[g, rows, :] for g in range(len(groups))]
        m = jnp.maximum(jnp.maximum(ls[0], ls[1]), ls[2])
        es = [jnp.exp(x - m) for x in ls]
        inv = 1.0 / (es[0] + es[1] + es[2])
        acc = (es[0] * inv) * og_ref[0, rows, :]
        for g in range(1, len(groups)):
            acc = acc + (es[g] * inv) * og_ref[g, rows, :]
        o_ref[0, rows, :] = acc.astype(BF16)


def _dilated(qs, ks, vs, slopes):
    B, _, L, W = qs[0].shape
    assert L % DA_TQ == 0 and all(L // d >= DA_BK and DA_TQ // d >= DA_BQ for d in DA_DILATIONS)

    def q_spec(d):
        return pl.BlockSpec((1, d, DA_TQ // d, HEAD_DIM), lambda b, h, t: (b, 0, t, h))

    def kv_spec(d):
        return pl.BlockSpec((1, d, L // d, HEAD_DIM), lambda b, h, t: (b, 0, 0, h))

    n_groups = len(DA_DILATIONS)
    return pl.pallas_call(
        functools.partial(_dilated_kernel, slopes=tuple(slopes)),
        grid=(B, W // HEAD_DIM, L // DA_TQ),
        in_specs=([q_spec(d) for d in DA_DILATIONS] + [kv_spec(d) for d in DA_DILATIONS] * 2),
        out_specs=pl.BlockSpec((1, DA_TQ, HEAD_DIM), lambda b, h, t: (b, t, h)),
        out_shape=jax.ShapeDtypeStruct((B, L, W), BF16),
        scratch_shapes=[pltpu.VMEM((n_groups, DA_TQ, HEAD_DIM), F32),
                        pltpu.VMEM((n_groups, DA_TQ, HEAD_DIM), F32),
                        pltpu.VMEM((n_groups, 3, DA_BQ, DA_BK), F32)],
        compiler_params=_params("arbitrary", "arbitrary", "arbitrary"),
        name="dilated",
    )(*qs, *ks, *vs)


def _mix_kernel(x_ref, ohg_ref, oda_ref, mq_ref, mk_ref, mv_ref, gates_ref,
                whg_ref, wda_ref, wmem_ref, wout_ref, out_ref):
    D = x_ref.shape[-1]

    def body(c, rows):
        mem_parts = []
        ones = jnp.ones((mk_ref.shape[1], HEAD_DIM), BF16)
        for h in range(MEM_HEADS):
            sl = slice(h * HEAD_DIM, (h + 1) * HEAD_DIM)
            s = _dot_nt(mq_ref[rows, sl], mk_ref[0, :, sl])
            p = jnp.exp(s - jnp.max(s, axis=-1, keepdims=True)).astype(BF16)
            ol = _dot(p, jnp.concatenate([mv_ref[0, :, sl], ones], axis=1))
            mem_parts.append((ol[:, :HEAD_DIM] / ol[:, HEAD_DIM:]).astype(BF16))
        o_mem = jnp.concatenate(mem_parts, axis=-1)

        merged = (gates_ref[rows, 0:D].astype(F32) * _dot(ohg_ref[rows, :], whg_ref[...])
                  + gates_ref[rows, D:2 * D].astype(F32) * _dot(oda_ref[rows, :], wda_ref[...])
                  + gates_ref[rows, 2 * D:3 * D].astype(F32) * _dot(o_mem, wmem_ref[...]))
        out_ref[rows, :] = x_ref[rows, :] + _dot(merged.astype(BF16), wout_ref[...])

    _for_row_chunks(x_ref.shape[0], body)


def _mix(x2, o_hg, o_da, mem_q, mem_k, mem_v, gates, w_hg, w_da, w_mem, w_out, L):
    T, D = x2.shape
    tm = WIDE_TOKEN_TILE
    nt = L // tm
    M = mem_k.shape[1]
    const = lambda i: (0, 0)

    def tok(width):
        return pl.BlockSpec((tm, width), lambda i: (i, 0))

    mem_spec = pl.BlockSpec((1, M, mem_k.shape[2]), lambda i: (i // nt, 0, 0))
    return pl.pallas_call(
        _mix_kernel,
        grid=(T // tm,),
        in_specs=[tok(D), tok(o_hg.shape[1]), tok(o_da.shape[1]), tok(mem_q.shape[1]),
                  mem_spec, mem_spec, tok(gates.shape[1]),
                  _resident(w_hg.shape, const), _resident(w_da.shape, const),
                  _resident(w_mem.shape, const), _resident(w_out.shape, const)],
        out_specs=tok(D),
        out_shape=jax.ShapeDtypeStruct((T, D), F32),
        compiler_params=_params("arbitrary"),
        name="mix",
    )(x2, o_hg, o_da, mem_q, mem_k, mem_v, gates, w_hg, w_da, w_mem, w_out)


FFN_TILE = 256


def _ffn_kernel(x_ref, gain_ref, wa_ref, wb_ref, wo_ref, out_ref, h_ref, g_ref):
    d_ff = wa_ref.shape[1]

    def body(c, rows):
        h_ref[...] = _rms_rows(x_ref[rows, :], gain_ref[...]).astype(BF16)
        for j in range(d_ff // FFN_TILE):
            cols = slice(j * FFN_TILE, (j + 1) * FFN_TILE)
            a = _dot(h_ref[...], wa_ref[:, cols])
            b = _dot(h_ref[...], wb_ref[:, cols])
            g_ref[:, cols] = (_silu(a) * b).astype(BF16)
        out_ref[rows, :] = x_ref[rows, :] + _dot(g_ref[...], wo_ref[...])

    _for_row_chunks(x_ref.shape[0], body)


def _ffn(x2, gain, w_in, w_out):
    T, D = x2.shape
    d_ff = w_out.shape[0]
    tm = WIDE_TOKEN_TILE
    return pl.pallas_call(
        _ffn_kernel,
        grid=(T // tm,),
        in_specs=[
            pl.BlockSpec((tm, D), lambda i: (i, 0)),
            pl.BlockSpec((1, D), lambda i: (0, 0)),
            _resident((D, d_ff), lambda i: (0, 0)),
            _resident((D, d_ff), lambda i: (0, 1)),
            _resident((d_ff, D), lambda i: (0, 0)),
        ],
        out_specs=pl.BlockSpec((tm, D), lambda i: (i, 0)),
        out_shape=jax.ShapeDtypeStruct((T, D), F32),
        scratch_shapes=[pltpu.VMEM((ROW_CHUNK, D), BF16), pltpu.VMEM((ROW_CHUNK, d_ff), BF16)],
        compiler_params=_params("arbitrary"),
        name="ffn",
    )(x2, gain, w_in, w_in, w_out)


def kernel(x, mem, norm_mix_gain, norm_mem_gain, w_in, lb_logits_fw, lb_logits_bw, hg_norm_gain,
           da_q_gain, da_k_gain, w_mem_kv, mem_q_gain, mem_k_gain, w_proj_hg, w_proj_da,
           w_proj_mem, w_out, norm_ffn_gain, w_ffn_in, w_ffn_out):
    B, L, D = x.shape
    depth = w_in.shape[0]
    assert depth == 1, "lower-bound tables are evaluated for a single layer"
    T = B * L
    slopes = _alibi_slopes(DA_HEADS)
    hg_cols = 5 * w_proj_hg.shape[1]
    x2 = x.reshape(T, D)
    for l in range(depth):
        mem_k, mem_v = _mem_kv(mem, norm_mem_gain[l][None], w_mem_kv[l].astype(BF16),
                               mem_k_gain[l][None])
        h, hq, lf_fw, k_fw, lf_bw, k_bw, hv, hg = _proj_hgrn(
            x2, norm_mix_gain[l][None], w_in[l, :, :hg_cols].astype(BF16),
            lb_logits_fw.astype(F32), lb_logits_bw.astype(F32))
        (q0, q1, q2, k0, k1, k2, v0, v1, v2, mem_q, gates) = _proj_attn(
            h, w_in[l, :, hg_cols:].astype(BF16), da_q_gain[l][None], da_k_gain[l][None],
            mem_q_gain[l][None], B, L)

        def seq(t):
            return t.reshape(B, L, t.shape[-1])

        o_hg = _hgrn(seq(hq), seq(lf_fw), seq(k_fw), seq(lf_bw), seq(k_bw), seq(hv), seq(hg),
                     hg_norm_gain[l][None])
        o_da = _dilated((q0, q1, q2), (k0, k1, k2), (v0, v1, v2), slopes)

        x2 = _mix(x2, o_hg.reshape(T, -1), o_da.reshape(T, -1), mem_q, mem_k, mem_v, gates,
                  w_proj_hg[l].astype(BF16), w_proj_da[l].astype(BF16),
                  w_proj_mem[l].astype(BF16), w_out[l].astype(BF16), L)
        x2 = _ffn(x2, norm_ffn_gain[l][None], w_ffn_in[l].astype(BF16), w_ffn_out[l].astype(BF16))
    return x2.reshape(B, L, D)
```

```python
import functools

import jax
import jax.numpy as jnp
from jax import lax
from jax.experimental import pallas as pl
from jax.experimental.pallas import tpu as pltpu

RMS_EPS = 1e-6
NEG_INF = -1e30
HEAD_DIM = 128
HG_CHUNK = 128
HG_UNROLL = 16
DA_CONFIGS = ((128, 1), (512, 4), (2048, 16))
DA_DILATIONS = tuple(d for _, d in DA_CONFIGS)
DA_RADIUS = 64
DA_HEADS_PER_GROUP = 4
DA_HEADS = DA_HEADS_PER_GROUP * len(DA_CONFIGS)
MEM_HEADS = 4
GROUP_WIDTH = DA_HEADS_PER_GROUP * HEAD_DIM
ATTN_SCALE = HEAD_DIM ** -0.5

BF16 = jnp.bfloat16
F32 = jnp.float32

VMEM_LIMIT_BYTES = 56 * 1024 * 1024

TOKEN_TILE = 512
WIDE_TOKEN_TILE = 1024
ROW_CHUNK = 256


def _alibi_slopes(n):
    return [float(2.0 ** (-8.0 * (i + 1) / n)) for i in range(n)]


def _sigmoid(x):
    return 1.0 / (1.0 + jnp.exp(-x))


def _silu(x):
    return x * _sigmoid(x)


def _rms_rows(x, gain):
    ms = jnp.mean(x * x, axis=-1, keepdims=True)
    return x * lax.rsqrt(ms + RMS_EPS) * gain


def _head_norm(x, gain, scale):
    parts = []
    for h in range(x.shape[-1] // HEAD_DIM):
        parts.append(_rms_rows(x[:, h * HEAD_DIM:(h + 1) * HEAD_DIM], gain) * scale)
    return jnp.concatenate(parts, axis=-1)


def _dot(a, b):
    return jnp.dot(a, b, preferred_element_type=F32)


def _dot_nt(a, b):
    return lax.dot_general(a, b, (((1,), (1,)), ((), ())), preferred_element_type=F32)


def _dot_tn(a, b):
    return lax.dot_general(a, b, (((0,), (0,)), ((), ())), preferred_element_type=F32)


def _params(*semantics):
    return pltpu.CompilerParams(dimension_semantics=semantics,
                                vmem_limit_bytes=VMEM_LIMIT_BYTES)


def _resident(shape, index_map):
    return pl.BlockSpec(shape, index_map, pipeline_mode=pl.Buffered(1))


def _for_row_chunks(n_rows, body):
    def step(c, carry):
        body(c, pl.ds(pl.multiple_of(c * ROW_CHUNK, ROW_CHUNK), ROW_CHUNK))
        return carry
    lax.fori_loop(0, n_rows // ROW_CHUNK, step, 0, unroll=True)


def _mem_kv_kernel(mem_ref, gain_ref, w_ref, kgain_ref, k_ref, v_ref):
    mem_n = _rms_rows(mem_ref[0], gain_ref[...]).astype(BF16)
    kv = _dot(mem_n, w_ref[...])
    width = k_ref.shape[-1]
    k_ref[0] = _head_norm(kv[:, :width], kgain_ref[...], 1.0).astype(BF16)
    v_ref[0] = kv[:, width:].astype(BF16)


def _mem_kv(mem, gain, w_kv, k_gain):
    B, M, D = mem.shape
    width = w_kv.shape[1] // 2
    return pl.pallas_call(
        _mem_kv_kernel,
        grid=(B,),
        in_specs=[
            pl.BlockSpec((1, M, D), lambda b: (b, 0, 0)),
            pl.BlockSpec((1, D), lambda b: (0, 0)),
            pl.BlockSpec((D, 2 * width), lambda b: (0, 0)),
            pl.BlockSpec((1, HEAD_DIM), lambda b: (0, 0)),
        ],
        out_specs=[
            pl.BlockSpec((1, M, width), lambda b: (b, 0, 0)),
            pl.BlockSpec((1, M, width), lambda b: (b, 0, 0)),
        ],
        out_shape=[jax.ShapeDtypeStruct((B, M, width), BF16)] * 2,
        compiler_params=_params("arbitrary"),
        name="mem_kv",
    )(mem, gain, w_kv, k_gain)


def _lower_bound(logits):
    e = jnp.exp(logits - jnp.max(logits, axis=0, keepdims=True))
    return e[0:1] / jnp.sum(e, axis=0, keepdims=True)


def _proj_hgrn_kernel(x_ref, gain_ref, w_ref, lbfw_ref, lbbw_ref,
                      h_ref, hq_ref, lffw_ref, kfw_ref, lfbw_ref, kbw_ref, hv_ref, hg_ref):
    W = hq_ref.shape[1]

    def section(rows, s):
        return _dot(h_ref[rows, :], w_ref[:, s * W:(s + 1) * W])

    row = lax.broadcasted_iota(jnp.int32, (ROW_CHUNK, ROW_CHUNK), 0)
    col = lax.broadcasted_iota(jnp.int32, (ROW_CHUNK, ROW_CHUNK), 1)
    same_chunk = (row // HG_CHUNK) == (col // HG_CHUNK)
    tri_fw = jnp.where(same_chunk & (row >= col), 1.0, 0.0).astype(BF16)
    tri_bw = jnp.where(same_chunk & (row <= col), 1.0, 0.0).astype(BF16)

    def forget(s, lb_ref, tri, rows, b_ref, k_ref):
        lb = _lower_bound(lb_ref[...])
        f = lb + (1.0 - lb) * _sigmoid(section(rows, s))
        k_ref[rows, :] = (1.0 - f).astype(BF16)
        lf = jnp.log(f)
        lf_hi = lf.astype(BF16)
        lf_lo = (lf - lf_hi.astype(F32)).astype(BF16)
        bb = _dot(tri, jnp.concatenate([lf_hi, lf_lo], axis=1))
        b_ref[rows, :] = bb[:, :W] + bb[:, W:]

    def body(c, rows):
        h_ref[rows, :] = _rms_rows(x_ref[rows, :], gain_ref[...]).astype(BF16)
        hq_ref[rows, :] = (_silu(section(rows, 0)) * ATTN_SCALE).astype(BF16)
        forget(1, lbfw_ref, tri_fw, rows, lffw_ref, kfw_ref)
        forget(2, lbbw_ref, tri_bw, rows, lfbw_ref, kbw_ref)
        hv_ref[rows, :] = section(rows, 3).astype(BF16)
        hg_ref[rows, :] = _silu(section(rows, 4)).astype(BF16)

    _for_row_chunks(x_ref.shape[0], body)


def _proj_hgrn(x2, gain, w_hg, lb_fw, lb_bw):
    T, D = x2.shape
    W = w_hg.shape[1] // 5
    tm = TOKEN_TILE
    const = lambda i: (0, 0)
    tok = lambda width: pl.BlockSpec((tm, width), lambda i: (i, 0))
    outs = [(D, BF16), (W, BF16), (W, F32), (W, BF16), (W, F32), (W, BF16), (W, BF16), (W, BF16)]
    return pl.pallas_call(
        _proj_hgrn_kernel,
        grid=(T // tm,),
        in_specs=[tok(D), pl.BlockSpec((1, D), const), _resident(w_hg.shape, const),
                  pl.BlockSpec(lb_fw.shape, const), pl.BlockSpec(lb_bw.shape, const)],
        out_specs=[tok(w) for w, _ in outs],
        out_shape=[jax.ShapeDtypeStruct((T, w), dt) for w, dt in outs],
        compiler_params=_params("arbitrary"),
        name="proj_hgrn",
    )(x2, gain, w_hg, lb_fw, lb_bw)


def _proj_attn_kernel(h_ref, w_ref, qg_ref, kg_ref, mg_ref,
                      q0_ref, q1_ref, q2_ref, k0_ref, k1_ref, k2_ref, v0_ref, v1_ref, v2_ref,
                      mq_ref, gates_ref, scr_ref):
    nh = GROUP_WIDTH // HEAD_DIM
    gates_col0 = w_ref.shape[1] - gates_ref.shape[1]
    D = gates_ref.shape[1] // 3

    def q_norm(t):
        return _rms_rows(t, qg_ref[...]) * ATTN_SCALE

    def k_norm(t):
        return _rms_rows(t, kg_ref[...])

    def body(c, rows):
        def project(col0, width):
            return _dot(h_ref[rows, :], w_ref[:, col0:col0 + width])

        def class_major(tile, out_ref, fn):
            d = out_ref.shape[1]
            n = ROW_CHUNK // d
            acc = project(tile * GROUP_WIDTH, GROUP_WIDTH)
            for hh in range(nh):
                cols = slice(hh * HEAD_DIM, (hh + 1) * HEAD_DIM)
                vals = fn(acc[:, cols])
                if d == 1:
                    out_ref[0, 0, rows, cols] = vals.astype(BF16)
                    continue
                scr_ref[tile, hh] = vals
                dst = pl.ds(pl.multiple_of(c * n, n), n)
                for r in range(d):
                    out_ref[0, r, dst, cols] = scr_ref[tile, hh, pl.ds(r, n, stride=d), :].astype(BF16)

        for g, (q_ref, k_ref, v_ref) in enumerate(((q0_ref, k0_ref, v0_ref), (q1_ref, k1_ref, v1_ref),
                                                   (q2_ref, k2_ref, v2_ref))):
            class_major(g, q_ref, q_norm)
            class_major(3 + g, k_ref, k_norm)
            class_major(6 + g, v_ref, lambda t: t)
        mq = project(9 * GROUP_WIDTH, GROUP_WIDTH)
        mq_ref[rows, :] = _head_norm(mq, mg_ref[...], ATTN_SCALE).astype(BF16)
        gates_ref[rows, :] = _sigmoid(project(gates_col0, 3 * D)).astype(BF16)

    _for_row_chunks(h_ref.shape[0], body)


def _proj_attn(h, w_at, q_gain, k_gain, mq_gain, B, L):
    T, D = h.shape
    tm = TOKEN_TILE
    nt = L // tm
    n_groups = len(DA_DILATIONS)
    gates_width = w_at.shape[1] - (3 * n_groups + 1) * GROUP_WIDTH
    const = lambda i: (0, 0)

    def cls(d):
        return (pl.BlockSpec((1, d, tm // d, GROUP_WIDTH), lambda i: (i // nt, 0, i % nt, 0)),
                jax.ShapeDtypeStruct((B, d, L // d, GROUP_WIDTH), BF16))

    outs = [cls(d) for _ in range(3) for d in DA_DILATIONS]
    for width in (GROUP_WIDTH, gates_width):
        outs.append((pl.BlockSpec((tm, width), lambda i: (i, 0)),
                     jax.ShapeDtypeStruct((T, width), BF16)))
    gain_spec = pl.BlockSpec((1, HEAD_DIM), const)
    return pl.pallas_call(
        _proj_attn_kernel,
        grid=(T // tm,),
        in_specs=[pl.BlockSpec((tm, D), lambda i: (i, 0)), _resident(w_at.shape, const),
                  gain_spec, gain_spec, gain_spec],
        out_specs=[o[0] for o in outs],
        out_shape=[o[1] for o in outs],
        scratch_shapes=[pltpu.VMEM((3 * n_groups, GROUP_WIDTH // HEAD_DIM, ROW_CHUNK, HEAD_DIM), F32)],
        compiler_params=_params("arbitrary"),
        name="proj_attn",
    )(h, w_at, q_gain, k_gain, mq_gain)


HG_PAIR = 2 * HEAD_DIM


def _head_diag(m):
    return jnp.concatenate([m[:HEAD_DIM, :HEAD_DIM], m[HEAD_DIM:, HEAD_DIM:]], axis=1)


def _hgrn_kernel(q_ref, bfw_ref, kfw_ref, bbw_ref, kbw_ref, v_ref, g_ref, gain_ref,
                 o_ref, acc_ref, sfw_ref, sbw_ref):
    L = q_ref.shape[1]
    C = HG_CHUNK
    n_chunks = L // C
    row = lax.broadcasted_iota(jnp.int32, (C, HG_PAIR), 0)
    col = lax.broadcasted_iota(jnp.int32, (C, HG_PAIR), 1)
    first_head = col < HEAD_DIM
    src = jnp.where(first_head, col, col - HEAD_DIM)
    lower = row >= src
    upper = row <= src

    sfw_ref[...] = jnp.zeros_like(sfw_ref)
    sbw_ref[...] = jnp.zeros_like(sbw_ref)

    def emit(rows, o, second):
        if not second:
            acc_ref[rows, :] = o
            return
        o = o + acc_ref[rows, :]
        o_ref[0, rows, :] = (_head_norm(o, gain_ref[...], 1.0)
                             * g_ref[0, rows, :].astype(F32)).astype(BF16)

    def chunk(c, b_ref, k_ref, s, second, causal, mid_row, end_row):
        rows = pl.ds(pl.multiple_of(c * C, C), C)
        b = b_ref[0, rows, :]
        b_mid = b[mid_row:mid_row + 1, :]
        b_end = b[end_row:end_row + 1, :]
        q_in = q_ref[0, rows, :].astype(F32) * jnp.exp(b - b_mid)
        k_in = k_ref[0, rows, :].astype(F32) * jnp.exp(b_mid - b)
        q_st = (q_in * jnp.exp(b_mid)).astype(BF16)
        k_st = (k_in * jnp.exp(b_end - b_mid)).astype(BF16)
        v = v_ref[0, rows, :]
        k_bf = k_in.astype(BF16)
        zero = jnp.zeros_like(k_bf)
        k_diag = jnp.concatenate([jnp.where(first_head, k_bf, zero),
                                  jnp.where(first_head, zero, k_bf)], axis=0)
        a = _dot_nt(q_in.astype(BF16), k_diag)
        a = jnp.where(causal, a, 0.0).astype(BF16)
        lhs = jnp.concatenate(
            [jnp.concatenate([q_st[:, :HEAD_DIM], a[:, :HEAD_DIM]], axis=1),
             jnp.concatenate([q_st[:, HEAD_DIM:], a[:, HEAD_DIM:]], axis=1)], axis=0)
        rhs = jnp.concatenate([s.astype(BF16), v], axis=0)
        emit(rows, _head_diag(_dot(lhs, rhs)), second)
        d = jnp.exp(b_end)
        d_col = jnp.concatenate(
            [jnp.broadcast_to(d[:, :HEAD_DIM], (HEAD_DIM, HEAD_DIM)).T,
             jnp.broadcast_to(d[:, HEAD_DIM:], (HEAD_DIM, HEAD_DIM)).T], axis=1)
        return s * d_col + _head_diag(_dot_tn(k_st, v))

    def body(n, carry, second):
        s_fw = sfw_ref[...]
        s_bw = sbw_ref[...]
        for u in range(HG_UNROLL):
            c = n * HG_UNROLL + u
            s_fw = chunk(c, bfw_ref, kfw_ref, s_fw, second, lower, C // 2 - 1, C - 1)
            s_bw = chunk(n_chunks - 1 - c, bbw_ref, kbw_ref, s_bw, second, upper, C // 2, 0)
        sfw_ref[...] = s_fw
        sbw_ref[...] = s_bw
        return carry

    trips = n_chunks // HG_UNROLL
    lax.fori_loop(0, trips // 2, functools.partial(body, second=False), 0)
    lax.fori_loop(trips // 2, trips, functools.partial(body, second=True), 0)


def _hgrn(hq, b_fw, k_fw, b_bw, k_bw, hv, hg, gain):
    B, L, W = hq.shape
    spec = pl.BlockSpec((1, L, HG_PAIR), lambda b, h: (b, 0, h))
    return pl.pallas_call(
        _hgrn_kernel,
        grid=(B, W // HG_PAIR),
        in_specs=[spec] * 7 + [pl.BlockSpec((1, HEAD_DIM), lambda b, h: (0, 0))],
        out_specs=spec,
        out_shape=jax.ShapeDtypeStruct((B, L, W), BF16),
        scratch_shapes=[pltpu.VMEM((L, HG_PAIR), F32),
                        pltpu.VMEM((HEAD_DIM, HG_PAIR), F32), pltpu.VMEM((HEAD_DIM, HG_PAIR), F32)],
        compiler_params=_params("arbitrary", "arbitrary"),
        name="hgrn2",
    )(hq, b_fw, k_fw, b_bw, k_bw, hv, hg, gain)


DA_TQ = 4096
DA_BQ = 128
DA_BK = DA_BQ + 2 * DA_RADIUS
DA_BLOCKS = DA_TQ // DA_BQ
DA_MERGE_ROWS = 256


def _dilated_kernel(q0_ref, q1_ref, q2_ref, k0_ref, k1_ref, k2_ref, v0_ref, v1_ref, v2_ref,
                    o_ref, og_ref, lse_ref, bias_ref, *, slopes):
    h = pl.program_id(1)
    t = pl.program_id(2)
    bq, bk = DA_BQ, DA_BK
    groups = tuple(zip(DA_DILATIONS, (q0_ref, q1_ref, q2_ref), (k0_ref, k1_ref, k2_ref),
                       (v0_ref, v1_ref, v2_ref)))

    @pl.when(t == 0)
    def _():
        row = lax.broadcasted_iota(jnp.int32, (bq, bk), 0)
        col = lax.broadcasted_iota(jnp.int32, (bq, bk), 1)
        for g, d in enumerate(DA_DILATIONS):
            slope = jnp.float32(slopes[g * DA_HEADS_PER_GROUP])
            for i in range(1, DA_HEADS_PER_GROUP):
                slope = jnp.where(h == i, jnp.float32(slopes[g * DA_HEADS_PER_GROUP + i]), slope)
            slope = slope * float(d)
            for variant, offset in enumerate((-DA_RADIUS, 0, -2 * DA_RADIUS)):
                dist = jnp.abs(col - row + offset)
                bias_ref[g, variant] = jnp.where(dist <= DA_RADIUS, -slope * dist.astype(F32),
                                                 NEG_INF)

    ones = jnp.ones((bk, HEAD_DIM), BF16)

    def block(idx, g, d, q_ref, k_ref, v_ref):
        Ld = k_ref.shape[2]
        per_class = DA_TQ // d // bq
        r = 0 if d == 1 else idx // per_class
        i = idx if d == 1 else idx % per_class
        n0 = t * (DA_TQ // d) + i * bq
        k0 = pl.multiple_of(jnp.clip(n0 - DA_RADIUS, 0, Ld - bk), DA_RADIUS)
        variant = jnp.where(n0 == 0, 1, jnp.where(n0 == Ld - bq, 2, 0))
        q = q_ref[0, r, pl.ds(pl.multiple_of(i * bq, bq), bq), :]
        k = k_ref[0, r, pl.ds(k0, bk), :]
        v = v_ref[0, r, pl.ds(k0, bk), :]
        bias = bias_ref[g, variant]
        s = jnp.where(bias == NEG_INF, NEG_INF, _dot_nt(q, k) + bias)
        m = jnp.max(s, axis=-1, keepdims=True)
        p = jnp.exp(s - m).astype(BF16)
        ol = _dot(p, jnp.concatenate([v, ones], axis=1))
        l = ol[:, HEAD_DIM:]
        if d == 1:
            dst = pl.ds(pl.multiple_of(i * bq, bq), bq)
        else:
            dst = pl.ds(i * (bq * d) + r, bq, stride=d)
        og_ref[g, dst, :] = ol[:, :HEAD_DIM] / l
        lse_ref[g, dst, :] = m + jnp.log(l)

    def body(idx, carry):
        for g, (d, q_ref, k_ref, v_ref) in enumerate(groups):
            block(idx, g, d, q_ref, k_ref, v_ref)
        return carry

    lax.fori_loop(0, DA_BLOCKS, body, 0, unroll=True)

    for c in range(DA_TQ // DA_MERGE_ROWS):
        rows = slice(c * DA_MERGE_ROWS, (c + 1) * DA_MERGE_ROWS)
        ls = [lse_ref[g, rows, :] for g in range(len(groups))]
        m = jnp.maximum(jnp.maximum(ls[0], ls[1]), ls[2])
        es = [jnp.exp(x - m) for x in ls]
        inv = 1.0 / (es[0] + es[1] + es[2])
        acc = (es[0] * inv) * og_ref[0, rows, :]
        for g in range(1, len(groups)):
            acc = acc + (es[g] * inv) * og_ref[g, rows, :]
        o_ref[0, rows, :] = acc.astype(BF16)


def _dilated(qs, ks, vs, slopes):
    B, _, L, W = qs[0].shape
    assert L % DA_TQ == 0 and all(L // d >= DA_BK and DA_TQ // d >= DA_BQ for d in DA_DILATIONS)

    def q_spec(d):
        return pl.BlockSpec((1, d, DA_TQ // d, HEAD_DIM), lambda b, h, t: (b, 0, t, h))

    def kv_spec(d):
        return pl.BlockSpec((1, d, L // d, HEAD_DIM), lambda b, h, t: (b, 0, 0, h))

    n_groups = len(DA_DILATIONS)
    return pl.pallas_call(
        functools.partial(_dilated_kernel, slopes=tuple(slopes)),
        grid=(B, W // HEAD_DIM, L // DA_TQ),
        in_specs=([q_spec(d) for d in DA_DILATIONS] + [kv_spec(d) for d in DA_DILATIONS] * 2),
        out_specs=pl.BlockSpec((1, DA_TQ, HEAD_DIM), lambda b, h, t: (b, t, h)),
        out_shape=jax.ShapeDtypeStruct((B, L, W), BF16),
        scratch_shapes=[pltpu.VMEM((n_groups, DA_TQ, HEAD_DIM), F32),
                        pltpu.VMEM((n_groups, DA_TQ, HEAD_DIM), F32),
                        pltpu.VMEM((n_groups, 3, DA_BQ, DA_BK), F32)],
        compiler_params=_params("arbitrary", "arbitrary", "arbitrary"),
        name="dilated",
    )(*qs, *ks, *vs)


def _mix_kernel(x_ref, ohg_ref, oda_ref, mq_ref, mk_ref, mv_ref, gates_ref,
                whg_ref, wda_ref, wmem_ref, wout_ref, out_ref):
    D = x_ref.shape[-1]

    def body(c, rows):
        mem_parts = []
        ones = jnp.ones((mk_ref.shape[1], HEAD_DIM), BF16)
        for h in range(MEM_HEADS):
            sl = slice(h * HEAD_DIM, (h + 1) * HEAD_DIM)
            s = _dot_nt(mq_ref[rows, sl], mk_ref[0, :, sl])
            p = jnp.exp(s - jnp.max(s, axis=-1, keepdims=True)).astype(BF16)
            ol = _dot(p, jnp.concatenate([mv_ref[0, :, sl], ones], axis=1))
            mem_parts.append((ol[:, :HEAD_DIM] / ol[:, HEAD_DIM:]).astype(BF16))
        o_mem = jnp.concatenate(mem_parts, axis=-1)

        merged = (gates_ref[rows, 0:D].astype(F32) * _dot(ohg_ref[rows, :], whg_ref[...])
                  + gates_ref[rows, D:2 * D].astype(F32) * _dot(oda_ref[rows, :], wda_ref[...])
                  + gates_ref[rows, 2 * D:3 * D].astype(F32) * _dot(o_mem, wmem_ref[...]))
        out_ref[rows, :] = x_ref[rows, :] + _dot(merged.astype(BF16), wout_ref[...])

    _for_row_chunks(x_ref.shape[0], body)


def _mix(x2, o_hg, o_da, mem_q, mem_k, mem_v, gates, w_hg, w_da, w_mem, w_out, L):
    T, D = x2.shape
    tm = WIDE_TOKEN_TILE
    nt = L // tm
    M = mem_k.shape[1]
    const = lambda i: (0, 0)

    def tok(width):
        return pl.BlockSpec((tm, width), lambda i: (i, 0))

    mem_spec = pl.BlockSpec((1, M, mem_k.shape[2]), lambda i: (i // nt, 0, 0))
    return pl.pallas_call(
        _mix_kernel,
        grid=(T // tm,),
        in_specs=[tok(D), tok(o_hg.shape[1]), tok(o_da.shape[1]), tok(mem_q.shape[1]),
                  mem_spec, mem_spec, tok(gates.shape[1]),
                  _resident(w_hg.shape, const), _resident(w_da.shape, const),
                  _resident(w_mem.shape, const), _resident(w_out.shape, const)],
        out_specs=tok(D),
        out_shape=jax.ShapeDtypeStruct((T, D), F32),
        compiler_params=_params("arbitrary"),
        name="mix",
    )(x2, o_hg, o_da, mem_q, mem_k, mem_v, gates, w_hg, w_da, w_mem, w_out)


FFN_TILE = 256


def _ffn_kernel(x_ref, gain_ref, wa_ref, wb_ref, wo_ref, out_ref, h_ref, g_ref):
    d_ff = wa_ref.shape[1]

    def body(c, rows):
        h_ref[...] = _rms_rows(x_ref[rows, :], gain_ref[...]).astype(BF16)
        for j in range(d_ff // FFN_TILE):
            cols = slice(j * FFN_TILE, (j + 1) * FFN_TILE)
            a = _dot(h_ref[...], wa_ref[:, cols])
            b = _dot(h_ref[...], wb_ref[:, cols])
            g_ref[:, cols] = (_silu(a) * b).astype(BF16)
        out_ref[rows, :] = x_ref[rows, :] + _dot(g_ref[...], wo_ref[...])

    _for_row_chunks(x_ref.shape[0], body)


def _ffn(x2, gain, w_in, w_out):
    T, D = x2.shape
    d_ff = w_out.shape[0]
    tm = WIDE_TOKEN_TILE
    return pl.pallas_call(
        _ffn_kernel,
        grid=(T // tm,),
        in_specs=[
            pl.BlockSpec((tm, D), lambda i: (i, 0)),
            pl.BlockSpec((1, D), lambda i: (0, 0)),
            _resident((D, d_ff), lambda i: (0, 0)),
            _resident((D, d_ff), lambda i: (0, 1)),
            _resident((d_ff, D), lambda i: (0, 0)),
        ],
        out_specs=pl.BlockSpec((tm, D), lambda i: (i, 0)),
        out_shape=jax.ShapeDtypeStruct((T, D), F32),
        scratch_shapes=[pltpu.VMEM((ROW_CHUNK, D), BF16), pltpu.VMEM((ROW_CHUNK, d_ff), BF16)],
        compiler_params=_params("arbitrary"),
        name="ffn",
    )(x2, gain, w_in, w_in, w_out)


def kernel(x, mem, norm_mix_gain, norm_mem_gain, w_in, lb_logits_fw, lb_logits_bw, hg_norm_gain,
           da_q_gain, da_k_gain, w_mem_kv, mem_q_gain, mem_k_gain, w_proj_hg, w_proj_da,
           w_proj_mem, w_out, norm_ffn_gain, w_ffn_in, w_ffn_out):
    B, L, D = x.shape
    depth = w_in.shape[0]
    assert depth == 1, "lower-bound tables are evaluated for a single layer"
    T = B * L
    slopes = _alibi_slopes(DA_HEADS)
    hg_cols = 5 * w_proj_hg.shape[1]
    x2 = x.reshape(T, D)
    for l in range(depth):
        mem_k, mem_v = _mem_kv(mem, norm_mem_gain[l][None], w_mem_kv[l].astype(BF16),
                               mem_k_gain[l][None])
        h, hq, lf_fw, k_fw, lf_bw, k_bw, hv, hg = _proj_hgrn(
            x2, norm_mix_gain[l][None], w_in[l, :, :hg_cols].astype(BF16),
            lb_logits_fw.astype(F32), lb_logits_bw.astype(F32))
        (q0, q1, q2, k0, k1, k2, v0, v1, v2, mem_q, gates) = _proj_attn(
            h, w_in[l, :, hg_cols:].astype(BF16), da_q_gain[l][None], da_k_gain[l][None],
            mem_q_gain[l][None], B, L)

        def seq(t):
            return t.reshape(B, L, t.shape[-1])

        o_hg = _hgrn(seq(hq), seq(lf_fw), seq(k_fw), seq(lf_bw), seq(k_bw), seq(hv), seq(hg),
                     hg_norm_gain[l][None])
        o_da = _dilated((q0, q1, q2), (k0, k1, k2), (v0, v1, v2), slopes)

        x2 = _mix(x2, o_hg.reshape(T, -1), o_da.reshape(T, -1), mem_q, mem_k, mem_v, gates,
                  w_proj_hg[l].astype(BF16), w_proj_da[l].astype(BF16),
                  w_proj_mem[l].astype(BF16), w_out[l].astype(BF16), L)
        x2 = _ffn(x2, norm_ffn_gain[l][None], w_ffn_in[l].astype(BF16), w_ffn_out[l].astype(BF16))
    return x2.reshape(B, L, D)
```

```python
import functools

import jax
import jax.numpy as jnp
from jax import lax
from jax.experimental import pallas as pl
from jax.experimental.pallas import tpu as pltpu

RMS_EPS = 1e-6
NEG_INF = -1e30
HEAD_DIM = 128
HG_CHUNK = 128
HG_UNROLL = 16
DA_CONFIGS = ((128, 1), (512, 4), (2048, 16))
DA_DILATIONS = tuple(d for _, d in DA_CONFIGS)
DA_RADIUS = 64
DA_HEADS_PER_GROUP = 4
DA_HEADS = DA_HEADS_PER_GROUP * len(DA_CONFIGS)
MEM_HEADS = 4
GROUP_WIDTH = DA_HEADS_PER_GROUP * HEAD_DIM
ATTN_SCALE = HEAD_DIM ** -0.5
LOG2_E = 1.4426950408889634

BF16 = jnp.bfloat16
F32 = jnp.float32

VMEM_LIMIT_BYTES = 56 * 1024 * 1024

TOKEN_TILE = 512
WIDE_TOKEN_TILE = 1024
ROW_CHUNK = 256


def _alibi_slopes(n):
    return [float(2.0 ** (-8.0 * (i + 1) / n)) for i in range(n)]


def _sigmoid(x):
    return 1.0 / (1.0 + jnp.exp(-x))


def _silu(x):
    return x * _sigmoid(x)


def _rms_rows(x, gain):
    ms = jnp.mean(x * x, axis=-1, keepdims=True)
    return x * lax.rsqrt(ms + RMS_EPS) * gain


def _head_norm(x, gain, scale):
    parts = []
    for h in range(x.shape[-1] // HEAD_DIM):
        parts.append(_rms_rows(x[:, h * HEAD_DIM:(h + 1) * HEAD_DIM], gain) * scale)
    return jnp.concatenate(parts, axis=-1)


def _dot(a, b):
    return jnp.dot(a, b, preferred_element_type=F32)


def _dot_nt(a, b):
    return lax.dot_general(a, b, (((1,), (1,)), ((), ())), preferred_element_type=F32)


def _dot_tn(a, b):
    return lax.dot_general(a, b, (((0,), (0,)), ((), ())), preferred_element_type=F32)


def _params(*semantics):
    return pltpu.CompilerParams(dimension_semantics=semantics,
                                vmem_limit_bytes=VMEM_LIMIT_BYTES)


def _resident(shape, index_map):
    return pl.BlockSpec(shape, index_map, pipeline_mode=pl.Buffered(1))


def _for_row_chunks(n_rows, body):
    def step(c, carry):
        body(c, pl.ds(pl.multiple_of(c * ROW_CHUNK, ROW_CHUNK), ROW_CHUNK))
        return carry
    lax.fori_loop(0, n_rows // ROW_CHUNK, step, 0, unroll=True)


def _mem_kv_kernel(mem_ref, gain_ref, w_ref, kgain_ref, k_ref, v_ref):
    mem_n = _rms_rows(mem_ref[0], gain_ref[...]).astype(BF16)
    kv = _dot(mem_n, w_ref[...])
    width = k_ref.shape[-1]
    k_ref[0] = _head_norm(kv[:, :width], kgain_ref[...], 1.0).astype(BF16)
    v_ref[0] = kv[:, width:].astype(BF16)


def _mem_kv(mem, gain, w_kv, k_gain):
    B, M, D = mem.shape
    width = w_kv.shape[1] // 2
    return pl.pallas_call(
        _mem_kv_kernel,
        grid=(B,),
        in_specs=[
            pl.BlockSpec((1, M, D), lambda b: (b, 0, 0)),
            pl.BlockSpec((1, D), lambda b: (0, 0)),
            pl.BlockSpec((D, 2 * width), lambda b: (0, 0)),
            pl.BlockSpec((1, HEAD_DIM), lambda b: (0, 0)),
        ],
        out_specs=[
            pl.BlockSpec((1, M, width), lambda b: (b, 0, 0)),
            pl.BlockSpec((1, M, width), lambda b: (b, 0, 0)),
        ],
        out_shape=[jax.ShapeDtypeStruct((B, M, width), BF16)] * 2,
        compiler_params=_params("arbitrary"),
        name="mem_kv",
    )(mem, gain, w_kv, k_gain)


def _lower_bound(logits):
    e = jnp.exp(logits - jnp.max(logits, axis=0, keepdims=True))
    return e[0:1] / jnp.sum(e, axis=0, keepdims=True)


def _proj_hgrn_kernel(x_ref, gain_ref, w_ref, lbfw_ref, lbbw_ref,
                      h_ref, hq_ref, lffw_ref, kfw_ref, lfbw_ref, kbw_ref, hv_ref, hg_ref):
    W = hq_ref.shape[1]

    def section(rows, s):
        return _dot(h_ref[rows, :], w_ref[:, s * W:(s + 1) * W])

    row = lax.broadcasted_iota(jnp.int32, (ROW_CHUNK, ROW_CHUNK), 0)
    col = lax.broadcasted_iota(jnp.int32, (ROW_CHUNK, ROW_CHUNK), 1)
    same_chunk = (row // HG_CHUNK) == (col // HG_CHUNK)
    tri_fw = jnp.where(same_chunk & (row >= col), 1.0, 0.0).astype(BF16)
    tri_bw = jnp.where(same_chunk & (row <= col), 1.0, 0.0).astype(BF16)

    def forget(s, lb_ref, tri, rows, b_ref, k_ref):
        lb = _lower_bound(lb_ref[...])
        f = lb + (1.0 - lb) * _sigmoid(section(rows, s))
        k_ref[rows, :] = (1.0 - f).astype(BF16)
        lf = jnp.log(f)
        lf_hi = lf.astype(BF16)
        lf_lo = (lf - lf_hi.astype(F32)).astype(BF16)
        bb = _dot(tri, jnp.concatenate([lf_hi, lf_lo], axis=1))
        b_ref[rows, :] = bb[:, :W] + bb[:, W:]

    def body(c, rows):
        h_ref[rows, :] = _rms_rows(x_ref[rows, :], gain_ref[...]).astype(BF16)
        hq_ref[rows, :] = (_silu(section(rows, 0)) * ATTN_SCALE).astype(BF16)
        forget(1, lbfw_ref, tri_fw, rows, lffw_ref, kfw_ref)
        forget(2, lbbw_ref, tri_bw, rows, lfbw_ref, kbw_ref)
        hv_ref[rows, :] = section(rows, 3).astype(BF16)
        hg_ref[rows, :] = _silu(section(rows, 4)).astype(BF16)

    _for_row_chunks(x_ref.shape[0], body)


def _proj_hgrn(x2, gain, w_hg, lb_fw, lb_bw):
    T, D = x2.shape
    W = w_hg.shape[1] // 5
    tm = TOKEN_TILE
    const = lambda i: (0, 0)
    tok = lambda width: pl.BlockSpec((tm, width), lambda i: (i, 0))
    outs = [(D, BF16), (W, BF16), (W, F32), (W, BF16), (W, F32), (W, BF16), (W, BF16), (W, BF16)]
    return pl.pallas_call(
        _proj_hgrn_kernel,
        grid=(T // tm,),
        in_specs=[tok(D), pl.BlockSpec((1, D), const), _resident(w_hg.shape, const),
                  pl.BlockSpec(lb_fw.shape, const), pl.BlockSpec(lb_bw.shape, const)],
        out_specs=[tok(w) for w, _ in outs],
        out_shape=[jax.ShapeDtypeStruct((T, w), dt) for w, dt in outs],
        compiler_params=_params("arbitrary"),
        name="proj_hgrn",
    )(x2, gain, w_hg, lb_fw, lb_bw)


def _proj_attn_kernel(h_ref, w_ref, qg_ref, kg_ref, mg_ref,
                      q0_ref, q1_ref, q2_ref, k0_ref, k1_ref, k2_ref, v0_ref, v1_ref, v2_ref,
                      mq_ref, gates_ref, scr_ref):
    nh = GROUP_WIDTH // HEAD_DIM
    gates_col0 = w_ref.shape[1] - gates_ref.shape[1]
    D = gates_ref.shape[1] // 3

    def q_norm(t):
        return _rms_rows(t, qg_ref[...]) * (ATTN_SCALE * LOG2_E)

    def k_norm(t):
        return _rms_rows(t, kg_ref[...])

    def body(c, rows):
        def project(col0, width):
            return _dot(h_ref[rows, :], w_ref[:, col0:col0 + width])

        def class_major(tile, out_ref, fn):
            d = out_ref.shape[1]
            n = ROW_CHUNK // d
            acc = project(tile * GROUP_WIDTH, GROUP_WIDTH)
            for hh in range(nh):
                cols = slice(hh * HEAD_DIM, (hh + 1) * HEAD_DIM)
                vals = fn(acc[:, cols])
                if d == 1:
                    out_ref[0, 0, rows, cols] = vals.astype(BF16)
                    continue
                scr_ref[tile, hh] = vals
                dst = pl.ds(pl.multiple_of(c * n, n), n)
                for r in range(d):
                    out_ref[0, r, dst, cols] = scr_ref[tile, hh, pl.ds(r, n, stride=d), :].astype(BF16)

        for g, (q_ref, k_ref, v_ref) in enumerate(((q0_ref, k0_ref, v0_ref), (q1_ref, k1_ref, v1_ref),
                                                   (q2_ref, k2_ref, v2_ref))):
            class_major(g, q_ref, q_norm)
            class_major(3 + g, k_ref, k_norm)
            class_major(6 + g, v_ref, lambda t: t)
        mq = project(9 * GROUP_WIDTH, GROUP_WIDTH)
        mq_ref[rows, :] = _head_norm(mq, mg_ref[...], ATTN_SCALE).astype(BF16)
        gates_ref[rows, :] = _sigmoid(project(gates_col0, 3 * D)).astype(BF16)

    _for_row_chunks(h_ref.shape[0], body)


def _proj_attn(h, w_at, q_gain, k_gain, mq_gain, B, L):
    T, D = h.shape
    tm = TOKEN_TILE
    nt = L // tm
    n_groups = len(DA_DILATIONS)
    gates_width = w_at.shape[1] - (3 * n_groups + 1) * GROUP_WIDTH
    const = lambda i: (0, 0)

    def cls(d):
        return (pl.BlockSpec((1, d, tm // d, GROUP_WIDTH), lambda i: (i // nt, 0, i % nt, 0)),
                jax.ShapeDtypeStruct((B, d, L // d, GROUP_WIDTH), BF16))

    outs = [cls(d) for _ in range(3) for d in DA_DILATIONS]
    for width in (GROUP_WIDTH, gates_width):
        outs.append((pl.BlockSpec((tm, width), lambda i: (i, 0)),
                     jax.ShapeDtypeStruct((T, width), BF16)))
    gain_spec = pl.BlockSpec((1, HEAD_DIM), const)
    return pl.pallas_call(
        _proj_attn_kernel,
        grid=(T // tm,),
        in_specs=[pl.BlockSpec((tm, D), lambda i: (i, 0)), _resident(w_at.shape, const),
                  gain_spec, gain_spec, gain_spec],
        out_specs=[o[0] for o in outs],
        out_shape=[o[1] for o in outs],
        scratch_shapes=[pltpu.VMEM((3 * n_groups, GROUP_WIDTH // HEAD_DIM, ROW_CHUNK, HEAD_DIM), F32)],
        compiler_params=_params("arbitrary"),
        name="proj_attn",
    )(h, w_at, q_gain, k_gain, mq_gain)


HG_PAIR = 2 * HEAD_DIM


def _head_diag(m):
    return jnp.concatenate([m[:HEAD_DIM, :HEAD_DIM], m[HEAD_DIM:, HEAD_DIM:]], axis=1)


def _hgrn_kernel(q_ref, bfw_ref, kfw_ref, bbw_ref, kbw_ref, v_ref, g_ref, gain_ref,
                 o_ref, acc_ref, sfw_ref, sbw_ref):
    L = q_ref.shape[1]
    C = HG_CHUNK
    n_chunks = L // C
    row = lax.broadcasted_iota(jnp.int32, (C, HG_PAIR), 0)
    col = lax.broadcasted_iota(jnp.int32, (C, HG_PAIR), 1)
    first_head = col < HEAD_DIM
    src = jnp.where(first_head, col, col - HEAD_DIM)
    lower = row >= src
    upper = row <= src

    sfw_ref[...] = jnp.zeros_like(sfw_ref)
    sbw_ref[...] = jnp.zeros_like(sbw_ref)

    def emit(rows, o, second):
        if not second:
            acc_ref[rows, :] = o
            return
        o = o + acc_ref[rows, :]
        o_ref[0, rows, :] = (_head_norm(o, gain_ref[...], 1.0)
                             * g_ref[0, rows, :].astype(F32)).astype(BF16)

    def chunk(c, b_ref, k_ref, s, second, causal, mid_row, end_row):
        rows = pl.ds(pl.multiple_of(c * C, C), C)
        b = b_ref[0, rows, :]
        b_mid = b[mid_row:mid_row + 1, :]
        b_end = b[end_row:end_row + 1, :]
        q_in = q_ref[0, rows, :].astype(F32) * jnp.exp(b - b_mid)
        k_in = k_ref[0, rows, :].astype(F32) * jnp.exp(b_mid - b)
        q_st = (q_in * jnp.exp(b_mid)).astype(BF16)
        k_st = (k_in * jnp.exp(b_end - b_mid)).astype(BF16)
        v = v_ref[0, rows, :]
        k_bf = k_in.astype(BF16)
        zero = jnp.zeros_like(k_bf)
        k_diag = jnp.concatenate([jnp.where(first_head, k_bf, zero),
                                  jnp.where(first_head, zero, k_bf)], axis=0)
        a = _dot_nt(q_in.astype(BF16), k_diag)
        a = jnp.where(causal, a, 0.0).astype(BF16)
        lhs = jnp.concatenate(
            [jnp.concatenate([q_st[:, :HEAD_DIM], a[:, :HEAD_DIM]], axis=1),
             jnp.concatenate([q_st[:, HEAD_DIM:], a[:, HEAD_DIM:]], axis=1)], axis=0)
        rhs = jnp.concatenate([s.astype(BF16), v], axis=0)
        emit(rows, _head_diag(_dot(lhs, rhs)), second)
        d = jnp.exp(b_end)
        d_col = jnp.concatenate(
            [jnp.broadcast_to(d[:, :HEAD_DIM], (HEAD_DIM, HEAD_DIM)).T,
             jnp.broadcast_to(d[:, HEAD_DIM:], (HEAD_DIM, HEAD_DIM)).T], axis=1)
        return s * d_col + _head_diag(_dot_tn(k_st, v))

    def body(n, carry, second):
        s_fw = sfw_ref[...]
        s_bw = sbw_ref[...]
        for u in range(HG_UNROLL):
            c = n * HG_UNROLL + u
            s_fw = chunk(c, bfw_ref, kfw_ref, s_fw, second, lower, C // 2 - 1, C - 1)
            s_bw = chunk(n_chunks - 1 - c, bbw_ref, kbw_ref, s_bw, second, upper, C // 2, 0)
        sfw_ref[...] = s_fw
        sbw_ref[...] = s_bw
        return carry

    trips = n_chunks // HG_UNROLL
    lax.fori_loop(0, trips // 2, functools.partial(body, second=False), 0)
    lax.fori_loop(trips // 2, trips, functools.partial(body, second=True), 0)


def _hgrn(hq, b_fw, k_fw, b_bw, k_bw, hv, hg, gain):
    B, L, W = hq.shape
    spec = pl.BlockSpec((1, L, HG_PAIR), lambda b, h: (b, 0, h))
    return pl.pallas_call(
        _hgrn_kernel,
        grid=(B, W // HG_PAIR),
        in_specs=[spec] * 7 + [pl.BlockSpec((1, HEAD_DIM), lambda b, h: (0, 0))],
        out_specs=spec,
        out_shape=jax.ShapeDtypeStruct((B, L, W), BF16),
        scratch_shapes=[pltpu.VMEM((L, HG_PAIR), F32),
                        pltpu.VMEM((HEAD_DIM, HG_PAIR), F32), pltpu.VMEM((HEAD_DIM, HG_PAIR), F32)],
        compiler_params=_params("arbitrary", "arbitrary"),
        name="hgrn2",
    )(hq, b_fw, k_fw, b_bw, k_bw, hv, hg, gain)


DA_TQ = 4096
DA_BQ = 128
DA_BK = DA_BQ + 2 * DA_RADIUS
DA_BLOCKS = DA_TQ // DA_BQ
DA_MERGE_ROWS = 256


def _dilated_kernel(q0_ref, q1_ref, q2_ref, k0_ref, k1_ref, k2_ref, v0_ref, v1_ref, v2_ref,
                    o_ref, og_ref, lse_ref, bias_ref, *, slopes):
    h = pl.program_id(1)
    t = pl.program_id(2)
    bq, bk = DA_BQ, DA_BK
    groups = tuple(zip(DA_DILATIONS, (q0_ref, q1_ref, q2_ref), (k0_ref, k1_ref, k2_ref),
                       (v0_ref, v1_ref, v2_ref)))

    @pl.when(t == 0)
    def _():
        row = lax.broadcasted_iota(jnp.int32, (bq, bk), 0)
        col = lax.broadcasted_iota(jnp.int32, (bq, bk), 1)
        for g, d in enumerate(DA_DILATIONS):
            slope = jnp.float32(slopes[g * DA_HEADS_PER_GROUP])
            for i in range(1, DA_HEADS_PER_GROUP):
                slope = jnp.where(h == i, jnp.float32(slopes[g * DA_HEADS_PER_GROUP + i]), slope)
            slope = slope * (float(d) * LOG2_E)
            for variant, offset in enumerate((-DA_RADIUS, 0, -2 * DA_RADIUS)):
                dist = jnp.abs(col - row + offset)
                bias_ref[g, variant] = jnp.where(dist <= DA_RADIUS, -slope * dist.astype(F32),
                                                 NEG_INF)

    ones = jnp.ones((bk, HEAD_DIM), BF16)

    def block(idx, g, d, q_ref, k_ref, v_ref):
        Ld = k_ref.shape[2]
        per_class = DA_TQ // d // bq
        r = 0 if d == 1 else idx // per_class
        i = idx if d == 1 else idx % per_class
        n0 = t * (DA_TQ // d) + i * bq
        k0 = pl.multiple_of(jnp.clip(n0 - DA_RADIUS, 0, Ld - bk), DA_RADIUS)
        variant = jnp.where(n0 == 0, 1, jnp.where(n0 == Ld - bq, 2, 0))
        q = q_ref[0, r, pl.ds(pl.multiple_of(i * bq, bq), bq), :]
        k = k_ref[0, r, pl.ds(k0, bk), :]
        v = v_ref[0, r, pl.ds(k0, bk), :]
        bias = bias_ref[g, variant]
        s = jnp.where(bias == NEG_INF, NEG_INF, _dot_nt(q, k) + bias)
        m = jnp.max(s, axis=-1, keepdims=True)
        p = jnp.exp2(s - m).astype(BF16)
        ol = _dot(p, jnp.concatenate([v, ones], axis=1))
        l = ol[:, HEAD_DIM:]
        if d == 1:
            dst = pl.ds(pl.multiple_of(i * bq, bq), bq)
        else:
            dst = pl.ds(i * (bq * d) + r, bq, stride=d)
        og_ref[g, dst, :] = ol[:, :HEAD_DIM] / l
        lse_ref[g, dst, :] = m + jnp.log2(l)

    def body(idx, carry):
        for g, (d, q_ref, k_ref, v_ref) in enumerate(groups):
            block(idx, g, d, q_ref, k_ref, v_ref)
        return carry

    lax.fori_loop(0, DA_BLOCKS, body, 0, unroll=True)

    for c in range(DA_TQ // DA_MERGE_ROWS):
        rows = slice(c * DA_MERGE_ROWS, (c + 1) * DA_MERGE_ROWS)
        ls = [lse_ref[g, rows, :] for g in range(len(groups))]
        m = jnp.maximum(jnp.maximum(ls[0], ls[1]), ls[2])
        es = [jnp.exp2(x - m) for x in ls]
        inv = 1.0 / (es[0] + es[1] + es[2])
        acc = (es[0] * inv) * og_ref[0, rows, :]
        for g in range(1, len(groups)):
            acc = acc + (es[g] * inv) * og_ref[g, rows, :]
        o_ref[0, rows, :] = acc.astype(BF16)


def _dilated(qs, ks, vs, slopes):
    B, _, L, W = qs[0].shape
    assert L % DA_TQ == 0 and all(L // d >= DA_BK and DA_TQ // d >= DA_BQ for d in DA_DILATIONS)

    def q_spec(d):
        return pl.BlockSpec((1, d, DA_TQ // d, HEAD_DIM), lambda b, h, t: (b, 0, t, h))

    def kv_spec(d):
        return pl.BlockSpec((1, d, L // d, HEAD_DIM), lambda b, h, t: (b, 0, 0, h))

    n_groups = len(DA_DILATIONS)
    return pl.pallas_call(
        functools.partial(_dilated_kernel, slopes=tuple(slopes)),
        grid=(B, W // HEAD_DIM, L // DA_TQ),
        in_specs=([q_spec(d) for d in DA_DILATIONS] + [kv_spec(d) for d in DA_DILATIONS] * 2),
        out_specs=pl.BlockSpec((1, DA_TQ, HEAD_DIM), lambda b, h, t: (b, t, h)),
        out_shape=jax.ShapeDtypeStruct((B, L, W), BF16),
        scratch_shapes=[pltpu.VMEM((n_groups, DA_TQ, HEAD_DIM), F32),
                        pltpu.VMEM((n_groups, DA_TQ, HEAD_DIM), F32),
                        pltpu.VMEM((n_groups, 3, DA_BQ, DA_BK), F32)],
        compiler_params=_params("arbitrary", "arbitrary", "arbitrary"),
        name="dilated",
    )(*qs, *ks, *vs)


def _mix_kernel(x_ref, ohg_ref, oda_ref, mq_ref, mk_ref, mv_ref, gates_ref,
                whg_ref, wda_ref, wmem_ref, wout_ref, out_ref):
    D = x_ref.shape[-1]

    def body(c, rows):
        mem_parts = []
        ones = jnp.ones((mk_ref.shape[1], HEAD_DIM), BF16)
        for h in range(MEM_HEADS):
            sl = slice(h * HEAD_DIM, (h + 1) * HEAD_DIM)
            s = _dot_nt(mq_ref[rows, sl], mk_ref[0, :, sl])
            p = jnp.exp(s - jnp.max(s, axis=-1, keepdims=True)).astype(BF16)
            ol = _dot(p, jnp.concatenate([mv_ref[0, :, sl], ones], axis=1))
            mem_parts.append((ol[:, :HEAD_DIM] / ol[:, HEAD_DIM:]).astype(BF16))
        o_mem = jnp.concatenate(mem_parts, axis=-1)

        merged = (gates_ref[rows, 0:D].astype(F32) * _dot(ohg_ref[rows, :], whg_ref[...])
                  + gates_ref[rows, D:2 * D].astype(F32) * _dot(oda_ref[rows, :], wda_ref[...])
                  + gates_ref[rows, 2 * D:3 * D].astype(F32) * _dot(o_mem, wmem_ref[...]))
        out_ref[rows, :] = x_ref[rows, :] + _dot(merged.astype(BF16), wout_ref[...])

    _for_row_chunks(x_ref.shape[0], body)


def _mix(x2, o_hg, o_da, mem_q, mem_k, mem_v, gates, w_hg, w_da, w_mem, w_out, L):
    T, D = x2.shape
    tm = WIDE_TOKEN_TILE
    nt = L // tm
    M = mem_k.shape[1]
    const = lambda i: (0, 0)

    def tok(width):
        return pl.BlockSpec((tm, width), lambda i: (i, 0))

    mem_spec = pl.BlockSpec((1, M, mem_k.shape[2]), lambda i: (i // nt, 0, 0))
    return pl.pallas_call(
        _mix_kernel,
        grid=(T // tm,),
        in_specs=[tok(D), tok(o_hg.shape[1]), tok(o_da.shape[1]), tok(mem_q.shape[1]),
                  mem_spec, mem_spec, tok(gates.shape[1]),
                  _resident(w_hg.shape, const), _resident(w_da.shape, const),
                  _resident(w_mem.shape, const), _resident(w_out.shape, const)],
        out_specs=tok(D),
        out_shape=jax.ShapeDtypeStruct((T, D), F32),
        compiler_params=_params("arbitrary"),
        name="mix",
    )(x2, o_hg, o_da, mem_q, mem_k, mem_v, gates, w_hg, w_da, w_mem, w_out)


FFN_TILE = 256


def _ffn_kernel(x_ref, gain_ref, wa_ref, wb_ref, wo_ref, out_ref, h_ref, g_ref):
    d_ff = wa_ref.shape[1]

    def body(c, rows):
        h_ref[...] = _rms_rows(x_ref[rows, :], gain_ref[...]).astype(BF16)
        for j in range(d_ff // FFN_TILE):
            cols = slice(j * FFN_TILE, (j + 1) * FFN_TILE)
            a = _dot(h_ref[...], wa_ref[:, cols])
            b = _dot(h_ref[...], wb_ref[:, cols])
            g_ref[:, cols] = (_silu(a) * b).astype(BF16)
        out_ref[rows, :] = x_ref[rows, :] + _dot(g_ref[...], wo_ref[...])

    _for_row_chunks(x_ref.shape[0], body)


def _ffn(x2, gain, w_in, w_out):
    T, D = x2.shape
    d_ff = w_out.shape[0]
    tm = WIDE_TOKEN_TILE
    return pl.pallas_call(
        _ffn_kernel,
        grid=(T // tm,),
        in_specs=[
            pl.BlockSpec((tm, D), lambda i: (i, 0)),
            pl.BlockSpec((1, D), lambda i: (0, 0)),
            _resident((D, d_ff), lambda i: (0, 0)),
            _resident((D, d_ff), lambda i: (0, 1)),
            _resident((d_ff, D), lambda i: (0, 0)),
        ],
        out_specs=pl.BlockSpec((tm, D), lambda i: (i, 0)),
        out_shape=jax.ShapeDtypeStruct((T, D), F32),
        scratch_shapes=[pltpu.VMEM((ROW_CHUNK, D), BF16), pltpu.VMEM((ROW_CHUNK, d_ff), BF16)],
        compiler_params=_params("arbitrary"),
        name="ffn",
    )(x2, gain, w_in, w_in, w_out)


def kernel(x, mem, norm_mix_gain, norm_mem_gain, w_in, lb_logits_fw, lb_logits_bw, hg_norm_gain,
           da_q_gain, da_k_gain, w_mem_kv, mem_q_gain, mem_k_gain, w_proj_hg, w_proj_da,
           w_proj_mem, w_out, norm_ffn_gain, w_ffn_in, w_ffn_out):
    B, L, D = x.shape
    depth = w_in.shape[0]
    assert depth == 1, "lower-bound tables are evaluated for a single layer"
    T = B * L
    slopes = _alibi_slopes(DA_HEADS)
    hg_cols = 5 * w_proj_hg.shape[1]
    x2 = x.reshape(T, D)
    for l in range(depth):
        mem_k, mem_v = _mem_kv(mem, norm_mem_gain[l][None], w_mem_kv[l].astype(BF16),
                               mem_k_gain[l][None])
        h, hq, lf_fw, k_fw, lf_bw, k_bw, hv, hg = _proj_hgrn(
            x2, norm_mix_gain[l][None], w_in[l, :, :hg_cols].astype(BF16),
            lb_logits_fw.astype(F32), lb_logits_bw.astype(F32))
        (q0, q1, q2, k0, k1, k2, v0, v1, v2, mem_q, gates) = _proj_attn(
            h, w_in[l, :, hg_cols:].astype(BF16), da_q_gain[l][None], da_k_gain[l][None],
            mem_q_gain[l][None], B, L)

        def seq(t):
            return t.reshape(B, L, t.shape[-1])

        o_hg = _hgrn(seq(hq), seq(lf_fw), seq(k_fw), seq(lf_bw), seq(k_bw), seq(hv), seq(hg),
                     hg_norm_gain[l][None])
        o_da = _dilated((q0, q1, q2), (k0, k1, k2), (v0, v1, v2), slopes)

        x2 = _mix(x2, o_hg.reshape(T, -1), o_da.reshape(T, -1), mem_q, mem_k, mem_v, gates,
                  w_proj_hg[l].astype(BF16), w_proj_da[l].astype(BF16),
                  w_proj_mem[l].astype(BF16), w_out[l].astype(BF16), L)
        x2 = _ffn(x2, norm_ffn_gain[l][None], w_ffn_in[l].astype(BF16), w_ffn_out[l].astype(BF16))
    return x2.reshape(B, L, D)
```

```python
import functools

import jax
import jax.numpy as jnp
from jax import lax
from jax.experimental import pallas as pl
from jax.experimental.pallas import tpu as pltpu

RMS_EPS = 1e-6
NEG_INF = -1e30
HEAD_DIM = 128
HG_CHUNK = 128
HG_UNROLL = 16
DA_CONFIGS = ((128, 1), (512, 4), (2048, 16))
DA_DILATIONS = tuple(d for _, d in DA_CONFIGS)
DA_RADIUS = 64
DA_HEADS_PER_GROUP = 4
DA_HEADS = DA_HEADS_PER_GROUP * len(DA_CONFIGS)
MEM_HEADS = 4
GROUP_WIDTH = DA_HEADS_PER_GROUP * HEAD_DIM
ATTN_SCALE = HEAD_DIM ** -0.5
LOG2_E = 1.4426950408889634

BF16 = jnp.bfloat16
F32 = jnp.float32

VMEM_LIMIT_BYTES = 56 * 1024 * 1024

TOKEN_TILE = 512
WIDE_TOKEN_TILE = 1024
ROW_CHUNK = 256


def _alibi_slopes(n):
    return [float(2.0 ** (-8.0 * (i + 1) / n)) for i in range(n)]


def _sigmoid(x):
    return 1.0 / (1.0 + jnp.exp(-x))


def _silu(x):
    return x * _sigmoid(x)


def _rms_rows(x, gain):
    ms = jnp.mean(x * x, axis=-1, keepdims=True)
    return x * lax.rsqrt(ms + RMS_EPS) * gain


def _head_norm(x, gain, scale):
    parts = []
    for h in range(x.shape[-1] // HEAD_DIM):
        parts.append(_rms_rows(x[:, h * HEAD_DIM:(h + 1) * HEAD_DIM], gain) * scale)
    return jnp.concatenate(parts, axis=-1)


def _dot(a, b):
    return jnp.dot(a, b, preferred_element_type=F32)


def _dot_nt(a, b):
    return lax.dot_general(a, b, (((1,), (1,)), ((), ())), preferred_element_type=F32)


def _dot_tn(a, b):
    return lax.dot_general(a, b, (((0,), (0,)), ((), ())), preferred_element_type=F32)


def _params(*semantics):
    return pltpu.CompilerParams(dimension_semantics=semantics,
                                vmem_limit_bytes=VMEM_LIMIT_BYTES)


def _resident(shape, index_map):
    return pl.BlockSpec(shape, index_map, pipeline_mode=pl.Buffered(1))


def _for_row_chunks(n_rows, body):
    def step(c, carry):
        body(c, pl.ds(pl.multiple_of(c * ROW_CHUNK, ROW_CHUNK), ROW_CHUNK))
        return carry
    lax.fori_loop(0, n_rows // ROW_CHUNK, step, 0, unroll=True)


def _mem_kv_kernel(mem_ref, gain_ref, w_ref, kgain_ref, k_ref, v_ref):
    mem_n = _rms_rows(mem_ref[0], gain_ref[...]).astype(BF16)
    kv = _dot(mem_n, w_ref[...])
    width = k_ref.shape[-1]
    k_ref[0] = _head_norm(kv[:, :width], kgain_ref[...], 1.0).astype(BF16)
    v_ref[0] = kv[:, width:].astype(BF16)


def _mem_kv(mem, gain, w_kv, k_gain):
    B, M, D = mem.shape
    width = w_kv.shape[1] // 2
    return pl.pallas_call(
        _mem_kv_kernel,
        grid=(B,),
        in_specs=[
            pl.BlockSpec((1, M, D), lambda b: (b, 0, 0)),
            pl.BlockSpec((1, D), lambda b: (0, 0)),
            pl.BlockSpec((D, 2 * width), lambda b: (0, 0)),
            pl.BlockSpec((1, HEAD_DIM), lambda b: (0, 0)),
        ],
        out_specs=[
            pl.BlockSpec((1, M, width), lambda b: (b, 0, 0)),
            pl.BlockSpec((1, M, width), lambda b: (b, 0, 0)),
        ],
        out_shape=[jax.ShapeDtypeStruct((B, M, width), BF16)] * 2,
        compiler_params=_params("arbitrary"),
        name="mem_kv",
    )(mem, gain, w_kv, k_gain)


def _lower_bound(logits):
    e = jnp.exp(logits - jnp.max(logits, axis=0, keepdims=True))
    return e[0:1] / jnp.sum(e, axis=0, keepdims=True)


def _proj_hgrn_kernel(x_ref, gain_ref, w_ref, lbfw_ref, lbbw_ref,
                      h_ref, hq_ref, lffw_ref, kfw_ref, lfbw_ref, kbw_ref, hv_ref, hg_ref):
    W = hq_ref.shape[1]

    def section(rows, s):
        return _dot(h_ref[rows, :], w_ref[:, s * W:(s + 1) * W])

    row = lax.broadcasted_iota(jnp.int32, (ROW_CHUNK, ROW_CHUNK), 0)
    col = lax.broadcasted_iota(jnp.int32, (ROW_CHUNK, ROW_CHUNK), 1)
    same_chunk = (row // HG_CHUNK) == (col // HG_CHUNK)
    tri_fw = jnp.where(same_chunk & (row >= col), 1.0, 0.0).astype(BF16)
    tri_bw = jnp.where(same_chunk & (row <= col), 1.0, 0.0).astype(BF16)

    def forget(s, lb_ref, tri, rows, b_ref, k_ref):
        lb = _lower_bound(lb_ref[...])
        f = lb + (1.0 - lb) * _sigmoid(section(rows, s))
        k_ref[rows, :] = (1.0 - f).astype(BF16)
        lf = jnp.log(f)
        lf_hi = lf.astype(BF16)
        lf_lo = (lf - lf_hi.astype(F32)).astype(BF16)
        bb = _dot(tri, jnp.concatenate([lf_hi, lf_lo], axis=1))
        b_ref[rows, :] = bb[:, :W] + bb[:, W:]

    def body(c, rows):
        h_ref[rows, :] = _rms_rows(x_ref[rows, :], gain_ref[...]).astype(BF16)
        hq_ref[rows, :] = (_silu(section(rows, 0)) * ATTN_SCALE).astype(BF16)
        forget(1, lbfw_ref, tri_fw, rows, lffw_ref, kfw_ref)
        forget(2, lbbw_ref, tri_bw, rows, lfbw_ref, kbw_ref)
        hv_ref[rows, :] = section(rows, 3).astype(BF16)
        hg_ref[rows, :] = _silu(section(rows, 4)).astype(BF16)

    _for_row_chunks(x_ref.shape[0], body)


def _proj_hgrn(x2, gain, w_in, hg_cols, lb_fw, lb_bw):
    T, D = x2.shape
    W = hg_cols // 5
    tm = TOKEN_TILE
    const = lambda i: (0, 0)
    tok = lambda width: pl.BlockSpec((tm, width), lambda i: (i, 0))
    outs = [(D, BF16), (W, BF16), (W, F32), (W, BF16), (W, F32), (W, BF16), (W, BF16), (W, BF16)]
    return pl.pallas_call(
        _proj_hgrn_kernel,
        grid=(T // tm,),
        in_specs=[tok(D), pl.BlockSpec((1, D), const), _resident((D, hg_cols), const),
                  pl.BlockSpec(lb_fw.shape, const), pl.BlockSpec(lb_bw.shape, const)],
        out_specs=[tok(w) for w, _ in outs],
        out_shape=[jax.ShapeDtypeStruct((T, w), dt) for w, dt in outs],
        compiler_params=_params("arbitrary"),
        name="proj_hgrn",
    )(x2, gain, w_in, lb_fw, lb_bw)


def _proj_attn_kernel(h_ref, *refs):
    n_windows = len(refs) - 15
    w_refs = refs[:n_windows]
    (qg_ref, kg_ref, mg_ref, q0_ref, q1_ref, q2_ref, k0_ref, k1_ref, k2_ref,
     v0_ref, v1_ref, v2_ref, mq_ref, gates_ref, scr_ref) = refs[n_windows:]
    window = w_refs[0].shape[1]
    nh = GROUP_WIDTH // HEAD_DIM
    gates_col0 = n_windows * window - gates_ref.shape[1]
    D = gates_ref.shape[1] // 3

    def q_norm(t):
        return _rms_rows(t, qg_ref[...]) * (ATTN_SCALE * LOG2_E)

    def k_norm(t):
        return _rms_rows(t, kg_ref[...])

    def body(c, rows):
        def project(col0, width):
            w, off = divmod(col0, window)
            assert off + width <= window
            return _dot(h_ref[rows, :], w_refs[w][:, off:off + width])

        def class_major(tile, out_ref, fn):
            d = out_ref.shape[1]
            n = ROW_CHUNK // d
            acc = project(tile * GROUP_WIDTH, GROUP_WIDTH)
            for hh in range(nh):
                cols = slice(hh * HEAD_DIM, (hh + 1) * HEAD_DIM)
                vals = fn(acc[:, cols])
                if d == 1:
                    out_ref[0, 0, rows, cols] = vals.astype(BF16)
                    continue
                scr_ref[tile, hh] = vals
                dst = pl.ds(pl.multiple_of(c * n, n), n)
                for r in range(d):
                    out_ref[0, r, dst, cols] = scr_ref[tile, hh, pl.ds(r, n, stride=d), :].astype(BF16)

        for g, (q_ref, k_ref, v_ref) in enumerate(((q0_ref, k0_ref, v0_ref), (q1_ref, k1_ref, v1_ref),
                                                   (q2_ref, k2_ref, v2_ref))):
            class_major(g, q_ref, q_norm)
            class_major(3 + g, k_ref, k_norm)
            class_major(6 + g, v_ref, lambda t: t)
        mq = project(9 * GROUP_WIDTH, GROUP_WIDTH)
        mq_ref[rows, :] = _head_norm(mq, mg_ref[...], ATTN_SCALE).astype(BF16)
        for s in range(3):
            gate = project(gates_col0 + s * D, D)
            gates_ref[rows, s * D:(s + 1) * D] = _sigmoid(gate).astype(BF16)

    _for_row_chunks(h_ref.shape[0], body)


def _proj_attn(h, w_in, col0, q_gain, k_gain, mq_gain, B, L):
    T, D = h.shape
    tm = TOKEN_TILE
    nt = L // tm
    n_groups = len(DA_DILATIONS)
    gates_width = w_in.shape[1] - col0 - (3 * n_groups + 1) * GROUP_WIDTH
    window = D
    assert col0 % window == 0 and w_in.shape[1] % window == 0 and gates_width == 3 * window
    windows = [_resident((D, window), lambda i, k=k: (0, k))
               for k in range(col0 // window, w_in.shape[1] // window)]
    const = lambda i: (0, 0)

    def cls(d):
        return (pl.BlockSpec((1, d, tm // d, GROUP_WIDTH), lambda i: (i // nt, 0, i % nt, 0)),
                jax.ShapeDtypeStruct((B, d, L // d, GROUP_WIDTH), BF16))

    outs = [cls(d) for _ in range(3) for d in DA_DILATIONS]
    for width in (GROUP_WIDTH, gates_width):
        outs.append((pl.BlockSpec((tm, width), lambda i: (i, 0)),
                     jax.ShapeDtypeStruct((T, width), BF16)))
    gain_spec = pl.BlockSpec((1, HEAD_DIM), const)
    return pl.pallas_call(
        _proj_attn_kernel,
        grid=(T // tm,),
        in_specs=[pl.BlockSpec((tm, D), lambda i: (i, 0))] + windows
                 + [gain_spec, gain_spec, gain_spec],
        out_specs=[o[0] for o in outs],
        out_shape=[o[1] for o in outs],
        scratch_shapes=[pltpu.VMEM((3 * n_groups, GROUP_WIDTH // HEAD_DIM, ROW_CHUNK, HEAD_DIM), F32)],
        compiler_params=_params("arbitrary"),
        name="proj_attn",
    )(h, *([w_in] * len(windows)), q_gain, k_gain, mq_gain)


HG_PAIR = 2 * HEAD_DIM


def _head_diag(m):
    return jnp.concatenate([m[:HEAD_DIM, :HEAD_DIM], m[HEAD_DIM:, HEAD_DIM:]], axis=1)


def _hgrn_kernel(q_ref, bfw_ref, kfw_ref, bbw_ref, kbw_ref, v_ref, g_ref, gain_ref,
                 o_ref, acc_ref, sfw_ref, sbw_ref):
    L = q_ref.shape[1]
    C = HG_CHUNK
    n_chunks = L // C
    row = lax.broadcasted_iota(jnp.int32, (C, HG_PAIR), 0)
    col = lax.broadcasted_iota(jnp.int32, (C, HG_PAIR), 1)
    first_head = col < HEAD_DIM
    src = jnp.where(first_head, col, col - HEAD_DIM)
    lower = row >= src
    upper = row <= src

    sfw_ref[...] = jnp.zeros_like(sfw_ref)
    sbw_ref[...] = jnp.zeros_like(sbw_ref)

    def emit(rows, o, second):
        if not second:
            acc_ref[rows, :] = o
            return
        o = o + acc_ref[rows, :]
        o_ref[0, rows, :] = (_head_norm(o, gain_ref[...], 1.0)
                             * g_ref[0, rows, :].astype(F32)).astype(BF16)

    def chunk(c, b_ref, k_ref, s, second, causal, mid_row, end_row):
        rows = pl.ds(pl.multiple_of(c * C, C), C)
        b = b_ref[0, rows, :]
        b_mid = b[mid_row:mid_row + 1, :]
        b_end = b[end_row:end_row + 1, :]
        q_in = q_ref[0, rows, :].astype(F32) * jnp.exp(b - b_mid)
        k_in = k_ref[0, rows, :].astype(F32) * jnp.exp(b_mid - b)
        q_st = (q_in * jnp.exp(b_mid)).astype(BF16)
        k_st = (k_in * jnp.exp(b_end - b_mid)).astype(BF16)
        v = v_ref[0, rows, :]
        k_bf = k_in.astype(BF16)
        zero = jnp.zeros_like(k_bf)
        k_diag = jnp.concatenate([jnp.where(first_head, k_bf, zero),
                                  jnp.where(first_head, zero, k_bf)], axis=0)
        a = _dot_nt(q_in.astype(BF16), k_diag)
        a = jnp.where(causal, a, 0.0).astype(BF16)
        lhs = jnp.concatenate(
            [jnp.concatenate([q_st[:, :HEAD_DIM], a[:, :HEAD_DIM]], axis=1),
             jnp.concatenate([q_st[:, HEAD_DIM:], a[:, HEAD_DIM:]], axis=1)], axis=0)
        rhs = jnp.concatenate([s.astype(BF16), v], axis=0)
        emit(rows, _head_diag(_dot(lhs, rhs)), second)
        d = jnp.exp(b_end)
        d_col = jnp.concatenate(
            [jnp.broadcast_to(d[:, :HEAD_DIM], (HEAD_DIM, HEAD_DIM)).T,
             jnp.broadcast_to(d[:, HEAD_DIM:], (HEAD_DIM, HEAD_DIM)).T], axis=1)
        return s * d_col + _head_diag(_dot_tn(k_st, v))

    def body(n, carry, second):
        s_fw = sfw_ref[...]
        s_bw = sbw_ref[...]
        for u in range(HG_UNROLL):
            c = n * HG_UNROLL + u
            s_fw = chunk(c, bfw_ref, kfw_ref, s_fw, second, lower, C // 2 - 1, C - 1)
            s_bw = chunk(n_chunks - 1 - c, bbw_ref, kbw_ref, s_bw, second, upper, C // 2, 0)
        sfw_ref[...] = s_fw
        sbw_ref[...] = s_bw
        return carry

    trips = n_chunks // HG_UNROLL
    lax.fori_loop(0, trips // 2, functools.partial(body, second=False), 0)
    lax.fori_loop(trips // 2, trips, functools.partial(body, second=True), 0)


def _hgrn(hq, b_fw, k_fw, b_bw, k_bw, hv, hg, gain):
    B, L, W = hq.shape
    spec = pl.BlockSpec((1, L, HG_PAIR), lambda b, h: (b, 0, h))
    return pl.pallas_call(
        _hgrn_kernel,
        grid=(B, W // HG_PAIR),
        in_specs=[spec] * 7 + [pl.BlockSpec((1, HEAD_DIM), lambda b, h: (0, 0))],
        out_specs=spec,
        out_shape=jax.ShapeDtypeStruct((B, L, W), BF16),
        scratch_shapes=[pltpu.VMEM((L, HG_PAIR), F32),
                        pltpu.VMEM((HEAD_DIM, HG_PAIR), F32), pltpu.VMEM((HEAD_DIM, HG_PAIR), F32)],
        compiler_params=_params("arbitrary", "arbitrary"),
        name="hgrn2",
    )(hq, b_fw, k_fw, b_bw, k_bw, hv, hg, gain)


DA_TQ = 4096
DA_BQ = 128
DA_BK = DA_BQ + 2 * DA_RADIUS
DA_BLOCKS = DA_TQ // DA_BQ
DA_MERGE_ROWS = 256


def _dilated_kernel(q0_ref, q1_ref, q2_ref, k0_ref, k1_ref, k2_ref, v0_ref, v1_ref, v2_ref,
                    o_ref, og_ref, lse_ref, bias_ref, *, slopes):
    h = pl.program_id(1)
    t = pl.program_id(2)
    bq, bk = DA_BQ, DA_BK
    groups = tuple(zip(DA_DILATIONS, (q0_ref, q1_ref, q2_ref), (k0_ref, k1_ref, k2_ref),
                       (v0_ref, v1_ref, v2_ref)))

    @pl.when(t == 0)
    def _():
        row = lax.broadcasted_iota(jnp.int32, (bq, bk), 0)
        col = lax.broadcasted_iota(jnp.int32, (bq, bk), 1)
        for g, d in enumerate(DA_DILATIONS):
            slope = jnp.float32(slopes[g * DA_HEADS_PER_GROUP])
            for i in range(1, DA_HEADS_PER_GROUP):
                slope = jnp.where(h == i, jnp.float32(slopes[g * DA_HEADS_PER_GROUP + i]), slope)
            slope = slope * (float(d) * LOG2_E)
            for variant, offset in enumerate((-DA_RADIUS, 0, -2 * DA_RADIUS)):
                dist = jnp.abs(col - row + offset)
                bias_ref[g, variant] = jnp.where(dist <= DA_RADIUS, -slope * dist.astype(F32),
                                                 NEG_INF)

    ones = jnp.ones((bk, HEAD_DIM), BF16)

    def block(idx, g, d, q_ref, k_ref, v_ref):
        Ld = k_ref.shape[2]
        per_class = DA_TQ // d // bq
        r = 0 if d == 1 else idx // per_class
        i = idx if d == 1 else idx % per_class
        n0 = t * (DA_TQ // d) + i * bq
        k0 = pl.multiple_of(jnp.clip(n0 - DA_RADIUS, 0, Ld - bk), DA_RADIUS)
        variant = jnp.where(n0 == 0, 1, jnp.where(n0 == Ld - bq, 2, 0))
        q = q_ref[0, r, pl.ds(pl.multiple_of(i * bq, bq), bq), :]
        k = k_ref[0, r, pl.ds(k0, bk), :]
        v = v_ref[0, r, pl.ds(k0, bk), :]
        bias = bias_ref[g, variant]
        s = jnp.where(bias == NEG_INF, NEG_INF, _dot_nt(q, k) + bias)
        m = jnp.max(s, axis=-1, keepdims=True)
        p = jnp.exp2(s - m).astype(BF16)
        ol = _dot(p, jnp.concatenate([v, ones], axis=1))
        l = ol[:, HEAD_DIM:]
        if d == 1:
            dst = pl.ds(pl.multiple_of(i * bq, bq), bq)
        else:
            dst = pl.ds(i * (bq * d) + r, bq, stride=d)
        og_ref[g, dst, :] = ol[:, :HEAD_DIM] / l
        lse_ref[g, dst, :] = m + jnp.log2(l)

    def body(idx, carry):
        for g, (d, q_ref, k_ref, v_ref) in enumerate(groups):
            block(idx, g, d, q_ref, k_ref, v_ref)
        return carry

    lax.fori_loop(0, DA_BLOCKS, body, 0, unroll=True)

    for c in range(DA_TQ // DA_MERGE_ROWS):
        rows = slice(c * DA_MERGE_ROWS, (c + 1) * DA_MERGE_ROWS)
        ls = [lse_ref[g, rows, :] for g in range(len(groups))]
        m = jnp.maximum(jnp.maximum(ls[0], ls[1]), ls[2])
        es = [jnp.exp2(x - m) for x in ls]
        inv = 1.0 / (es[0] + es[1] + es[2])
        acc = (es[0] * inv) * og_ref[0, rows, :]
        for g in range(1, len(groups)):
            acc = acc + (es[g] * inv) * og_ref[g, rows, :]
        o_ref[0, rows, :] = acc.astype(BF16)


def _dilated(qs, ks, vs, slopes):
    B, _, L, W = qs[0].shape
    assert L % DA_TQ == 0 and all(L // d >= DA_BK and DA_TQ // d >= DA_BQ for d in DA_DILATIONS)

    def q_spec(d):
        return pl.BlockSpec((1, d, DA_TQ // d, HEAD_DIM), lambda b, h, t: (b, 0, t, h))

    def kv_spec(d):
        return pl.BlockSpec((1, d, L // d, HEAD_DIM), lambda b, h, t: (b, 0, 0, h))

    n_groups = len(DA_DILATIONS)
    return pl.pallas_call(
        functools.partial(_dilated_kernel, slopes=tuple(slopes)),
        grid=(B, W // HEAD_DIM, L // DA_TQ),
        in_specs=([q_spec(d) for d in DA_DILATIONS] + [kv_spec(d) for d in DA_DILATIONS] * 2),
        out_specs=pl.BlockSpec((1, DA_TQ, HEAD_DIM), lambda b, h, t: (b, t, h)),
        out_shape=jax.ShapeDtypeStruct((B, L, W), BF16),
        scratch_shapes=[pltpu.VMEM((n_groups, DA_TQ, HEAD_DIM), F32),
                        pltpu.VMEM((n_groups, DA_TQ, HEAD_DIM), F32),
                        pltpu.VMEM((n_groups, 3, DA_BQ, DA_BK), F32)],
        compiler_params=_params("arbitrary", "arbitrary", "arbitrary"),
        name="dilated",
    )(*qs, *ks, *vs)


def _mix_kernel(x_ref, ohg_ref, oda_ref, mq_ref, mk_ref, mv_ref, gates_ref,
                whg_ref, wda_ref, wmem_ref, wout_ref, out_ref):
    D = x_ref.shape[-1]

    def body(c, rows):
        mem_parts = []
        ones = jnp.ones((mk_ref.shape[1], HEAD_DIM), BF16)
        for h in range(MEM_HEADS):
            sl = slice(h * HEAD_DIM, (h + 1) * HEAD_DIM)
            s = _dot_nt(mq_ref[rows, sl], mk_ref[0, :, sl])
            p = jnp.exp(s - jnp.max(s, axis=-1, keepdims=True)).astype(BF16)
            ol = _dot(p, jnp.concatenate([mv_ref[0, :, sl], ones], axis=1))
            mem_parts.append((ol[:, :HEAD_DIM] / ol[:, HEAD_DIM:]).astype(BF16))
        o_mem = jnp.concatenate(mem_parts, axis=-1)

        merged = (gates_ref[rows, 0:D].astype(F32) * _dot(ohg_ref[rows, :], whg_ref[...])
                  + gates_ref[rows, D:2 * D].astype(F32) * _dot(oda_ref[rows, :], wda_ref[...])
                  + gates_ref[rows, 2 * D:3 * D].astype(F32) * _dot(o_mem, wmem_ref[...]))
        out_ref[rows, :] = x_ref[rows, :] + _dot(merged.astype(BF16), wout_ref[...])

    _for_row_chunks(x_ref.shape[0], body)


def _mix(x2, o_hg, o_da, mem_q, mem_k, mem_v, gates, w_hg, w_da, w_mem, w_out, L):
    T, D = x2.shape
    tm = WIDE_TOKEN_TILE
    nt = L // tm
    M = mem_k.shape[1]
    const = lambda i: (0, 0)

    def tok(width):
        return pl.BlockSpec((tm, width), lambda i: (i, 0))

    mem_spec = pl.BlockSpec((1, M, mem_k.shape[2]), lambda i: (i // nt, 0, 0))
    return pl.pallas_call(
        _mix_kernel,
        grid=(T // tm,),
        in_specs=[tok(D), tok(o_hg.shape[1]), tok(o_da.shape[1]), tok(mem_q.shape[1]),
                  mem_spec, mem_spec, tok(gates.shape[1]),
                  _resident(w_hg.shape, const), _resident(w_da.shape, const),
                  _resident(w_mem.shape, const), _resident(w_out.shape, const)],
        out_specs=tok(D),
        out_shape=jax.ShapeDtypeStruct((T, D), F32),
        compiler_params=_params("arbitrary"),
        name="mix",
    )(x2, o_hg, o_da, mem_q, mem_k, mem_v, gates, w_hg, w_da, w_mem, w_out)


FFN_TILE = 256


def _ffn_kernel(x_ref, gain_ref, wa_ref, wb_ref, wo_ref, out_ref, h_ref, g_ref):
    d_ff = wa_ref.shape[1]

    def body(c, rows):
        h_ref[...] = _rms_rows(x_ref[rows, :], gain_ref[...]).astype(BF16)
        for j in range(d_ff // FFN_TILE):
            cols = slice(j * FFN_TILE, (j + 1) * FFN_TILE)
            a = _dot(h_ref[...], wa_ref[:, cols])
            b = _dot(h_ref[...], wb_ref[:, cols])
            g_ref[:, cols] = (_silu(a) * b).astype(BF16)
        out_ref[rows, :] = x_ref[rows, :] + _dot(g_ref[...], wo_ref[...])

    _for_row_chunks(x_ref.shape[0], body)


def _ffn(x2, gain, w_in, w_out):
    T, D = x2.shape
    d_ff = w_out.shape[0]
    tm = WIDE_TOKEN_TILE
    return pl.pallas_call(
        _ffn_kernel,
        grid=(T // tm,),
        in_specs=[
            pl.BlockSpec((tm, D), lambda i: (i, 0)),
            pl.BlockSpec((1, D), lambda i: (0, 0)),
            _resident((D, d_ff), lambda i: (0, 0)),
            _resident((D, d_ff), lambda i: (0, 1)),
            _resident((d_ff, D), lambda i: (0, 0)),
        ],
        out_specs=pl.BlockSpec((tm, D), lambda i: (i, 0)),
        out_shape=jax.ShapeDtypeStruct((T, D), F32),
        scratch_shapes=[pltpu.VMEM((ROW_CHUNK, D), BF16), pltpu.VMEM((ROW_CHUNK, d_ff), BF16)],
        compiler_params=_params("arbitrary"),
        name="ffn",
    )(x2, gain, w_in, w_in, w_out)


def kernel(x, mem, norm_mix_gain, norm_mem_gain, w_in, lb_logits_fw, lb_logits_bw, hg_norm_gain,
           da_q_gain, da_k_gain, w_mem_kv, mem_q_gain, mem_k_gain, w_proj_hg, w_proj_da,
           w_proj_mem, w_out, norm_ffn_gain, w_ffn_in, w_ffn_out):
    B, L, D = x.shape
    depth = w_in.shape[0]
    assert depth == 1, "lower-bound tables are evaluated for a single layer"
    T = B * L
    slopes = _alibi_slopes(DA_HEADS)
    hg_cols = 5 * w_proj_hg.shape[1]
    x2 = x.reshape(T, D)
    for l in range(depth):
        mem_k, mem_v = _mem_kv(mem, norm_mem_gain[l][None], w_mem_kv[l].astype(BF16),
                               mem_k_gain[l][None])
        w_in_l = w_in[l].astype(BF16)
        h, hq, lf_fw, k_fw, lf_bw, k_bw, hv, hg = _proj_hgrn(
            x2, norm_mix_gain[l][None], w_in_l, hg_cols,
            lb_logits_fw.astype(F32), lb_logits_bw.astype(F32))
        (q0, q1, q2, k0, k1, k2, v0, v1, v2, mem_q, gates) = _proj_attn(
            h, w_in_l, hg_cols, da_q_gain[l][None], da_k_gain[l][None],
            mem_q_gain[l][None], B, L)

        def seq(t):
            return t.reshape(B, L, t.shape[-1])

        o_hg = _hgrn(seq(hq), seq(lf_fw), seq(k_fw), seq(lf_bw), seq(k_bw), seq(hv), seq(hg),
                     hg_norm_gain[l][None])
        o_da = _dilated((q0, q1, q2), (k0, k1, k2), (v0, v1, v2), slopes)

        x2 = _mix(x2, o_hg.reshape(T, -1), o_da.reshape(T, -1), mem_q, mem_k, mem_v, gates,
                  w_proj_hg[l].astype(BF16), w_proj_da[l].astype(BF16),
                  w_proj_mem[l].astype(BF16), w_out[l].astype(BF16), L)
        x2 = _ffn(x2, norm_ffn_gain[l][None], w_ffn_in[l].astype(BF16), w_ffn_out[l].astype(BF16))
    return x2.reshape(B, L, D)
```

```python
import functools

import jax
import jax.numpy as jnp
from jax import lax
from jax.experimental import pallas as pl
from jax.experimental.pallas import tpu as pltpu

RMS_EPS = 1e-6
NEG_INF = -1e30
HEAD_DIM = 128
HG_CHUNK = 128
HG_UNROLL = 16
DA_CONFIGS = ((128, 1), (512, 4), (2048, 16))
DA_DILATIONS = tuple(d for _, d in DA_CONFIGS)
DA_RADIUS = 64
DA_HEADS_PER_GROUP = 4
DA_HEADS = DA_HEADS_PER_GROUP * len(DA_CONFIGS)
MEM_HEADS = 4
GROUP_WIDTH = DA_HEADS_PER_GROUP * HEAD_DIM
ATTN_SCALE = HEAD_DIM ** -0.5
LOG2_E = 1.4426950408889634

BF16 = jnp.bfloat16
F32 = jnp.float32

VMEM_LIMIT_BYTES = 56 * 1024 * 1024

TOKEN_TILE = 512
WIDE_TOKEN_TILE = 1024
ROW_CHUNK = 256


def _alibi_slopes(n):
    return [float(2.0 ** (-8.0 * (i + 1) / n)) for i in range(n)]


def _sigmoid(x):
    return 1.0 / (1.0 + jnp.exp(-x))


def _silu(x):
    return x * _sigmoid(x)


def _rms_rows(x, gain):
    ms = jnp.mean(x * x, axis=-1, keepdims=True)
    return x * lax.rsqrt(ms + RMS_EPS) * gain


def _head_norm(x, gain, scale):
    gain = gain * scale
    parts = []
    for h in range(x.shape[-1] // HEAD_DIM):
        parts.append(_rms_rows(x[:, h * HEAD_DIM:(h + 1) * HEAD_DIM], gain))
    return jnp.concatenate(parts, axis=-1)


def _dot(a, b):
    return jnp.dot(a, b, preferred_element_type=F32)


def _dot_nt(a, b):
    return lax.dot_general(a, b, (((1,), (1,)), ((), ())), preferred_element_type=F32)


def _dot_tn(a, b):
    return lax.dot_general(a, b, (((0,), (0,)), ((), ())), preferred_element_type=F32)


def _params(*semantics):
    return pltpu.CompilerParams(dimension_semantics=semantics,
                                vmem_limit_bytes=VMEM_LIMIT_BYTES)


def _resident(shape, index_map):
    return pl.BlockSpec(shape, index_map, pipeline_mode=pl.Buffered(1))


def _for_row_chunks(n_rows, body):
    def step(c, carry):
        body(c, pl.ds(pl.multiple_of(c * ROW_CHUNK, ROW_CHUNK), ROW_CHUNK))
        return carry
    lax.fori_loop(0, n_rows // ROW_CHUNK, step, 0, unroll=True)


def _mem_kv_kernel(mem_ref, gain_ref, w_ref, kgain_ref, k_ref, v_ref):
    mem_n = _rms_rows(mem_ref[0], gain_ref[...]).astype(BF16)
    kv = _dot(mem_n, w_ref[...])
    width = k_ref.shape[-1]
    k_ref[0] = _head_norm(kv[:, :width], kgain_ref[...], 1.0).astype(BF16)
    v_ref[0] = kv[:, width:].astype(BF16)


def _mem_kv(mem, gain, w_kv, k_gain):
    B, M, D = mem.shape
    width = w_kv.shape[1] // 2
    return pl.pallas_call(
        _mem_kv_kernel,
        grid=(B,),
        in_specs=[
            pl.BlockSpec((1, M, D), lambda b: (b, 0, 0)),
            pl.BlockSpec((1, D), lambda b: (0, 0)),
            pl.BlockSpec((D, 2 * width), lambda b: (0, 0)),
            pl.BlockSpec((1, HEAD_DIM), lambda b: (0, 0)),
        ],
        out_specs=[
            pl.BlockSpec((1, M, width), lambda b: (b, 0, 0)),
            pl.BlockSpec((1, M, width), lambda b: (b, 0, 0)),
        ],
        out_shape=[jax.ShapeDtypeStruct((B, M, width), BF16)] * 2,
        compiler_params=_params("arbitrary"),
        name="mem_kv",
    )(mem, gain, w_kv, k_gain)


def _lower_bound(logits):
    e = jnp.exp(logits - jnp.max(logits, axis=0, keepdims=True))
    return e[0:1] / jnp.sum(e, axis=0, keepdims=True)


def _proj_hgrn_kernel(x_ref, gain_ref, w_ref, lbfw_ref, lbbw_ref,
                      h_ref, hq_ref, lffw_ref, kfw_ref, lfbw_ref, kbw_ref, hv_ref, hg_ref):
    W = hq_ref.shape[1]

    def section(rows, s):
        return _dot(h_ref[rows, :], w_ref[:, s * W:(s + 1) * W])

    row = lax.broadcasted_iota(jnp.int32, (ROW_CHUNK, ROW_CHUNK), 0)
    col = lax.broadcasted_iota(jnp.int32, (ROW_CHUNK, ROW_CHUNK), 1)
    same_chunk = (row // HG_CHUNK) == (col // HG_CHUNK)
    tri_fw = jnp.where(same_chunk & (row >= col), 1.0, 0.0).astype(BF16)
    tri_bw = jnp.where(same_chunk & (row <= col), 1.0, 0.0).astype(BF16)

    def forget(s, lb_ref, tri, rows, b_ref, k_ref):
        lb = _lower_bound(lb_ref[...])
        f = lb + (1.0 - lb) * _sigmoid(section(rows, s))
        k_ref[rows, :] = (1.0 - f).astype(BF16)
        lf = jnp.log(f)
        lf_hi = lf.astype(BF16)
        lf_lo = (lf - lf_hi.astype(F32)).astype(BF16)
        bb = _dot(tri, jnp.concatenate([lf_hi, lf_lo], axis=1))
        b_ref[rows, :] = bb[:, :W] + bb[:, W:]

    def body(c, rows):
        h_ref[rows, :] = _rms_rows(x_ref[rows, :], gain_ref[...]).astype(BF16)
        hq_ref[rows, :] = (_silu(section(rows, 0)) * ATTN_SCALE).astype(BF16)
        forget(1, lbfw_ref, tri_fw, rows, lffw_ref, kfw_ref)
        forget(2, lbbw_ref, tri_bw, rows, lfbw_ref, kbw_ref)
        hv_ref[rows, :] = section(rows, 3).astype(BF16)
        hg_ref[rows, :] = _silu(section(rows, 4)).astype(BF16)

    _for_row_chunks(x_ref.shape[0], body)


def _proj_hgrn(x2, gain, w_in, hg_cols, lb_fw, lb_bw):
    T, D = x2.shape
    W = hg_cols // 5
    tm = TOKEN_TILE
    const = lambda i: (0, 0)
    tok = lambda width: pl.BlockSpec((tm, width), lambda i: (i, 0))
    outs = [(D, BF16), (W, BF16), (W, F32), (W, BF16), (W, F32), (W, BF16), (W, BF16), (W, BF16)]
    return pl.pallas_call(
        _proj_hgrn_kernel,
        grid=(T // tm,),
        in_specs=[tok(D), pl.BlockSpec((1, D), const), _resident((D, hg_cols), const),
                  pl.BlockSpec(lb_fw.shape, const), pl.BlockSpec(lb_bw.shape, const)],
        out_specs=[tok(w) for w, _ in outs],
        out_shape=[jax.ShapeDtypeStruct((T, w), dt) for w, dt in outs],
        compiler_params=_params("arbitrary"),
        name="proj_hgrn",
    )(x2, gain, w_in, lb_fw, lb_bw)


def _proj_attn_kernel(h_ref, *refs):
    n_windows = len(refs) - 15
    w_refs = refs[:n_windows]
    (qg_ref, kg_ref, mg_ref, q0_ref, q1_ref, q2_ref, k0_ref, k1_ref, k2_ref,
     v0_ref, v1_ref, v2_ref, mq_ref, gates_ref, scr_ref) = refs[n_windows:]
    window = w_refs[0].shape[1]
    nh = GROUP_WIDTH // HEAD_DIM
    gates_col0 = n_windows * window - gates_ref.shape[1]
    D = gates_ref.shape[1] // 3

    def q_norm(t):
        return _rms_rows(t, qg_ref[...] * (ATTN_SCALE * LOG2_E))

    def k_norm(t):
        return _rms_rows(t, kg_ref[...])

    def body(c, rows):
        def project(col0, width):
            w, off = divmod(col0, window)
            assert off + width <= window
            return _dot(h_ref[rows, :], w_refs[w][:, off:off + width])

        def class_major(tile, out_ref, fn):
            d = out_ref.shape[1]
            n = ROW_CHUNK // d
            acc = project(tile * GROUP_WIDTH, GROUP_WIDTH)
            for hh in range(nh):
                cols = slice(hh * HEAD_DIM, (hh + 1) * HEAD_DIM)
                vals = fn(acc[:, cols])
                if d == 1:
                    out_ref[0, 0, rows, cols] = vals.astype(BF16)
                    continue
                scr_ref[tile, hh] = vals
                dst = pl.ds(pl.multiple_of(c * n, n), n)
                for r in range(d):
                    out_ref[0, r, dst, cols] = scr_ref[tile, hh, pl.ds(r, n, stride=d), :].astype(BF16)

        for g, (q_ref, k_ref, v_ref) in enumerate(((q0_ref, k0_ref, v0_ref), (q1_ref, k1_ref, v1_ref),
                                                   (q2_ref, k2_ref, v2_ref))):
            class_major(g, q_ref, q_norm)
            class_major(3 + g, k_ref, k_norm)
            class_major(6 + g, v_ref, lambda t: t)
        mq = project(9 * GROUP_WIDTH, GROUP_WIDTH)
        mq_ref[rows, :] = _head_norm(mq, mg_ref[...], ATTN_SCALE * LOG2_E).astype(BF16)
        for s in range(3):
            gate = project(gates_col0 + s * D, D)
            gates_ref[rows, s * D:(s + 1) * D] = _sigmoid(gate).astype(BF16)

    _for_row_chunks(h_ref.shape[0], body)


def _proj_attn(h, w_in, col0, q_gain, k_gain, mq_gain, B, L):
    T, D = h.shape
    tm = TOKEN_TILE
    nt = L // tm
    n_groups = len(DA_DILATIONS)
    gates_width = w_in.shape[1] - col0 - (3 * n_groups + 1) * GROUP_WIDTH
    window = D
    assert col0 % window == 0 and w_in.shape[1] % window == 0 and gates_width == 3 * window
    windows = [_resident((D, window), lambda i, k=k: (0, k))
               for k in range(col0 // window, w_in.shape[1] // window)]
    const = lambda i: (0, 0)

    def cls(d):
        return (pl.BlockSpec((1, d, tm // d, GROUP_WIDTH), lambda i: (i // nt, 0, i % nt, 0)),
                jax.ShapeDtypeStruct((B, d, L // d, GROUP_WIDTH), BF16))

    outs = [cls(d) for _ in range(3) for d in DA_DILATIONS]
    for width in (GROUP_WIDTH, gates_width):
        outs.append((pl.BlockSpec((tm, width), lambda i: (i, 0)),
                     jax.ShapeDtypeStruct((T, width), BF16)))
    gain_spec = pl.BlockSpec((1, HEAD_DIM), const)
    return pl.pallas_call(
        _proj_attn_kernel,
        grid=(T // tm,),
        in_specs=[pl.BlockSpec((tm, D), lambda i: (i, 0))] + windows
                 + [gain_spec, gain_spec, gain_spec],
        out_specs=[o[0] for o in outs],
        out_shape=[o[1] for o in outs],
        scratch_shapes=[pltpu.VMEM((3 * n_groups, GROUP_WIDTH // HEAD_DIM, ROW_CHUNK, HEAD_DIM), F32)],
        compiler_params=_params("arbitrary"),
        name="proj_attn",
    )(h, *([w_in] * len(windows)), q_gain, k_gain, mq_gain)


HG_PAIR = 2 * HEAD_DIM


def _head_diag(m):
    return jnp.concatenate([m[:HEAD_DIM, :HEAD_DIM], m[HEAD_DIM:, HEAD_DIM:]], axis=1)


def _hgrn_kernel(q_ref, bfw_ref, kfw_ref, bbw_ref, kbw_ref, v_ref, g_ref, gain_ref,
                 o_ref, acc_ref, sfw_ref, sbw_ref):
    L = q_ref.shape[1]
    C = HG_CHUNK
    n_chunks = L // C
    row = lax.broadcasted_iota(jnp.int32, (C, HG_PAIR), 0)
    col = lax.broadcasted_iota(jnp.int32, (C, HG_PAIR), 1)
    first_head = col < HEAD_DIM
    src = jnp.where(first_head, col, col - HEAD_DIM)
    lower = row >= src
    upper = row <= src

    sfw_ref[...] = jnp.zeros_like(sfw_ref)
    sbw_ref[...] = jnp.zeros_like(sbw_ref)

    def emit(rows, o, second):
        if not second:
            acc_ref[rows, :] = o
            return
        o = o + acc_ref[rows, :]
        o_ref[0, rows, :] = (_head_norm(o, gain_ref[...], 1.0)
                             * g_ref[0, rows, :].astype(F32)).astype(BF16)

    def chunk(c, b_ref, k_ref, s, second, causal, mid_row, end_row):
        rows = pl.ds(pl.multiple_of(c * C, C), C)
        b = b_ref[0, rows, :]
        b_mid = b[mid_row:mid_row + 1, :]
        b_end = b[end_row:end_row + 1, :]
        q_in = q_ref[0, rows, :].astype(F32) * jnp.exp(b - b_mid)
        k_in = k_ref[0, rows, :].astype(F32) * jnp.exp(b_mid - b)
        q_st = (q_in * jnp.exp(b_mid)).astype(BF16)
        k_st = (k_in * jnp.exp(b_end - b_mid)).astype(BF16)
        v = v_ref[0, rows, :]
        k_bf = k_in.astype(BF16)
        zero = jnp.zeros_like(k_bf)
        k_diag = jnp.concatenate([jnp.where(first_head, k_bf, zero),
                                  jnp.where(first_head, zero, k_bf)], axis=0)
        a = _dot_nt(q_in.astype(BF16), k_diag)
        a = jnp.where(causal, a, 0.0).astype(BF16)
        lhs = jnp.concatenate(
            [jnp.concatenate([q_st[:, :HEAD_DIM], a[:, :HEAD_DIM]], axis=1),
             jnp.concatenate([q_st[:, HEAD_DIM:], a[:, HEAD_DIM:]], axis=1)], axis=0)
        rhs = jnp.concatenate([s.astype(BF16), v], axis=0)
        emit(rows, _head_diag(_dot(lhs, rhs)), second)
        d = jnp.exp(b_end)
        d_col = jnp.concatenate(
            [jnp.broadcast_to(d[:, :HEAD_DIM], (HEAD_DIM, HEAD_DIM)).T,
             jnp.broadcast_to(d[:, HEAD_DIM:], (HEAD_DIM, HEAD_DIM)).T], axis=1)
        return s * d_col + _head_diag(_dot_tn(k_st, v))

    def body(n, carry, second):
        s_fw = sfw_ref[...]
        s_bw = sbw_ref[...]
        for u in range(HG_UNROLL):
            c = n * HG_UNROLL + u
            s_fw = chunk(c, bfw_ref, kfw_ref, s_fw, second, lower, C // 2 - 1, C - 1)
            s_bw = chunk(n_chunks - 1 - c, bbw_ref, kbw_ref, s_bw, second, upper, C // 2, 0)
        sfw_ref[...] = s_fw
        sbw_ref[...] = s_bw
        return carry

    trips = n_chunks // HG_UNROLL
    lax.fori_loop(0, trips // 2, functools.partial(body, second=False), 0)
    lax.fori_loop(trips // 2, trips, functools.partial(body, second=True), 0)


def _hgrn(hq, b_fw, k_fw, b_bw, k_bw, hv, hg, gain):
    B, L, W = hq.shape
    spec = pl.BlockSpec((1, L, HG_PAIR), lambda b, h: (b, 0, h))
    return pl.pallas_call(
        _hgrn_kernel,
        grid=(B, W // HG_PAIR),
        in_specs=[spec] * 7 + [pl.BlockSpec((1, HEAD_DIM), lambda b, h: (0, 0))],
        out_specs=spec,
        out_shape=jax.ShapeDtypeStruct((B, L, W), BF16),
        scratch_shapes=[pltpu.VMEM((L, HG_PAIR), F32),
                        pltpu.VMEM((HEAD_DIM, HG_PAIR), F32), pltpu.VMEM((HEAD_DIM, HG_PAIR), F32)],
        compiler_params=_params("arbitrary", "arbitrary"),
        name="hgrn2",
    )(hq, b_fw, k_fw, b_bw, k_bw, hv, hg, gain)


DA_TQ = 4096
DA_BQ = 128
DA_BK = DA_BQ + 2 * DA_RADIUS
DA_BLOCKS = DA_TQ // DA_BQ
DA_MERGE_ROWS = 256


def _dilated_kernel(q0_ref, q1_ref, q2_ref, k0_ref, k1_ref, k2_ref, v0_ref, v1_ref, v2_ref,
                    o_ref, og_ref, lse_ref, bias_ref, *, slopes):
    h = pl.program_id(1)
    t = pl.program_id(2)
    bq, bk = DA_BQ, DA_BK
    groups = tuple(zip(DA_DILATIONS, (q0_ref, q1_ref, q2_ref), (k0_ref, k1_ref, k2_ref),
                       (v0_ref, v1_ref, v2_ref)))

    @pl.when(t == 0)
    def _():
        row = lax.broadcasted_iota(jnp.int32, (bq, bk), 0)
        col = lax.broadcasted_iota(jnp.int32, (bq, bk), 1)
        for g, d in enumerate(DA_DILATIONS):
            slope = jnp.float32(slopes[g * DA_HEADS_PER_GROUP])
            for i in range(1, DA_HEADS_PER_GROUP):
                slope = jnp.where(h == i, jnp.float32(slopes[g * DA_HEADS_PER_GROUP + i]), slope)
            slope = slope * (float(d) * LOG2_E)
            for variant, offset in enumerate((-DA_RADIUS, 0, -2 * DA_RADIUS)):
                dist = jnp.abs(col - row + offset)
                bias_ref[g, variant] = jnp.where(dist <= DA_RADIUS, -slope * dist.astype(F32),
                                                 NEG_INF)

    ones = jnp.ones((bk, HEAD_DIM), BF16)

    def block(idx, g, d, q_ref, k_ref, v_ref):
        Ld = k_ref.shape[2]
        per_class = DA_TQ // d // bq
        r = 0 if d == 1 else idx // per_class
        i = idx if d == 1 else idx % per_class
        n0 = t * (DA_TQ // d) + i * bq
        k0 = pl.multiple_of(jnp.clip(n0 - DA_RADIUS, 0, Ld - bk), DA_RADIUS)
        variant = jnp.where(n0 == 0, 1, jnp.where(n0 == Ld - bq, 2, 0))
        q = q_ref[0, r, pl.ds(pl.multiple_of(i * bq, bq), bq), :]
        k = k_ref[0, r, pl.ds(k0, bk), :]
        v = v_ref[0, r, pl.ds(k0, bk), :]
        bias = bias_ref[g, variant]
        s = jnp.where(bias == NEG_INF, NEG_INF, _dot_nt(q, k) + bias)
        m = jnp.max(s, axis=-1, keepdims=True)
        p = jnp.exp2(s - m).astype(BF16)
        ol = _dot(p, jnp.concatenate([v, ones], axis=1))
        l = ol[:, HEAD_DIM:]
        if d == 1:
            dst = pl.ds(pl.multiple_of(i * bq, bq), bq)
        else:
            dst = pl.ds(i * (bq * d) + r, bq, stride=d)
        og_ref[g, dst, :] = ol[:, :HEAD_DIM] / l
        lse_ref[g, dst, :] = m + jnp.log2(l)

    def body(idx, carry):
        for g, (d, q_ref, k_ref, v_ref) in enumerate(groups):
            block(idx, g, d, q_ref, k_ref, v_ref)
        return carry

    lax.fori_loop(0, DA_BLOCKS, body, 0, unroll=True)

    for c in range(DA_TQ // DA_MERGE_ROWS):
        rows = slice(c * DA_MERGE_ROWS, (c + 1) * DA_MERGE_ROWS)
        ls = [lse_ref[g, rows, :] for g in range(len(groups))]
        m = jnp.maximum(jnp.maximum(ls[0], ls[1]), ls[2])
        es = [jnp.exp2(x - m) for x in ls]
        acc = es[0] * og_ref[0, rows, :]
        for g in range(1, len(groups)):
            acc = acc + es[g] * og_ref[g, rows, :]
        o_ref[0, rows, :] = (acc / (es[0] + es[1] + es[2])).astype(BF16)


def _dilated(qs, ks, vs, slopes):
    B, _, L, W = qs[0].shape
    assert L % DA_TQ == 0 and all(L // d >= DA_BK and DA_TQ // d >= DA_BQ for d in DA_DILATIONS)

    def q_spec(d):
        return pl.BlockSpec((1, d, DA_TQ // d, HEAD_DIM), lambda b, h, t: (b, 0, t, h))

    def kv_spec(d):
        return pl.BlockSpec((1, d, L // d, HEAD_DIM), lambda b, h, t: (b, 0, 0, h))

    n_groups = len(DA_DILATIONS)
    return pl.pallas_call(
        functools.partial(_dilated_kernel, slopes=tuple(slopes)),
        grid=(B, W // HEAD_DIM, L // DA_TQ),
        in_specs=([q_spec(d) for d in DA_DILATIONS] + [kv_spec(d) for d in DA_DILATIONS] * 2),
        out_specs=pl.BlockSpec((1, DA_TQ, HEAD_DIM), lambda b, h, t: (b, t, h)),
        out_shape=jax.ShapeDtypeStruct((B, L, W), BF16),
        scratch_shapes=[pltpu.VMEM((n_groups, DA_TQ, HEAD_DIM), F32),
                        pltpu.VMEM((n_groups, DA_TQ, HEAD_DIM), F32),
                        pltpu.VMEM((n_groups, 3, DA_BQ, DA_BK), F32)],
        compiler_params=_params("arbitrary", "arbitrary", "arbitrary"),
        name="dilated",
    )(*qs, *ks, *vs)


def _mix_kernel(x_ref, ohg_ref, oda_ref, mq_ref, mk_ref, mv_ref, gates_ref,
                whg_ref, wda_ref, wmem_ref, wout_ref, out_ref):
    D = x_ref.shape[-1]

    def body(c, rows):
        mem_parts = []
        ones = jnp.ones((mk_ref.shape[1], HEAD_DIM), BF16)
        for h in range(MEM_HEADS):
            sl = slice(h * HEAD_DIM, (h + 1) * HEAD_DIM)
            s = _dot_nt(mq_ref[rows, sl], mk_ref[0, :, sl])
            p = jnp.exp2(s - jnp.max(s, axis=-1, keepdims=True)).astype(BF16)
            ol = _dot(p, jnp.concatenate([mv_ref[0, :, sl], ones], axis=1))
            mem_parts.append((ol[:, :HEAD_DIM] / ol[:, HEAD_DIM:]).astype(BF16))
        o_mem = jnp.concatenate(mem_parts, axis=-1)

        merged = (gates_ref[rows, 0:D].astype(F32) * _dot(ohg_ref[rows, :], whg_ref[...])
                  + gates_ref[rows, D:2 * D].astype(F32) * _dot(oda_ref[rows, :], wda_ref[...])
                  + gates_ref[rows, 2 * D:3 * D].astype(F32) * _dot(o_mem, wmem_ref[...]))
        out_ref[rows, :] = x_ref[rows, :] + _dot(merged.astype(BF16), wout_ref[...])

    _for_row_chunks(x_ref.shape[0], body)


def _mix(x2, o_hg, o_da, mem_q, mem_k, mem_v, gates, w_hg, w_da, w_mem, w_out, L):
    T, D = x2.shape
    tm = WIDE_TOKEN_TILE
    nt = L // tm
    M = mem_k.shape[1]
    const = lambda i: (0, 0)

    def tok(width):
        return pl.BlockSpec((tm, width), lambda i: (i, 0))

    mem_spec = pl.BlockSpec((1, M, mem_k.shape[2]), lambda i: (i // nt, 0, 0))
    return pl.pallas_call(
        _mix_kernel,
        grid=(T // tm,),
        in_specs=[tok(D), tok(o_hg.shape[1]), tok(o_da.shape[1]), tok(mem_q.shape[1]),
                  mem_spec, mem_spec, tok(gates.shape[1]),
                  _resident(w_hg.shape, const), _resident(w_da.shape, const),
                  _resident(w_mem.shape, const), _resident(w_out.shape, const)],
        out_specs=tok(D),
        out_shape=jax.ShapeDtypeStruct((T, D), F32),
        compiler_params=_params("arbitrary"),
        name="mix",
    )(x2, o_hg, o_da, mem_q, mem_k, mem_v, gates, w_hg, w_da, w_mem, w_out)


FFN_TILE = 256


def _ffn_kernel(x_ref, gain_ref, wa_ref, wb_ref, wo_ref, out_ref, h_ref, g_ref):
    d_ff = wa_ref.shape[1]

    def body(c, rows):
        h_ref[...] = _rms_rows(x_ref[rows, :], gain_ref[...]).astype(BF16)
        for j in range(d_ff // FFN_TILE):
            cols = slice(j * FFN_TILE, (j + 1) * FFN_TILE)
            a = _dot(h_ref[...], wa_ref[:, cols])
            b = _dot(h_ref[...], wb_ref[:, cols])
            g_ref[:, cols] = (_silu(a) * b).astype(BF16)
        out_ref[rows, :] = x_ref[rows, :] + _dot(g_ref[...], wo_ref[...])

    _for_row_chunks(x_ref.shape[0], body)


def _ffn(x2, gain, w_in, w_out):
    T, D = x2.shape
    d_ff = w_out.shape[0]
    tm = WIDE_TOKEN_TILE
    return pl.pallas_call(
        _ffn_kernel,
        grid=(T // tm,),
        in_specs=[
            pl.BlockSpec((tm, D), lambda i: (i, 0)),
            pl.BlockSpec((1, D), lambda i: (0, 0)),
            _resident((D, d_ff), lambda i: (0, 0)),
            _resident((D, d_ff), lambda i: (0, 1)),
            _resident((d_ff, D), lambda i: (0, 0)),
        ],
        out_specs=pl.BlockSpec((tm, D), lambda i: (i, 0)),
        out_shape=jax.ShapeDtypeStruct((T, D), F32),
        scratch_shapes=[pltpu.VMEM((ROW_CHUNK, D), BF16), pltpu.VMEM((ROW_CHUNK, d_ff), BF16)],
        compiler_params=_params("arbitrary"),
        name="ffn",
    )(x2, gain, w_in, w_in, w_out)


def kernel(x, mem, norm_mix_gain, norm_mem_gain, w_in, lb_logits_fw, lb_logits_bw, hg_norm_gain,
           da_q_gain, da_k_gain, w_mem_kv, mem_q_gain, mem_k_gain, w_proj_hg, w_proj_da,
           w_proj_mem, w_out, norm_ffn_gain, w_ffn_in, w_ffn_out):
    B, L, D = x.shape
    depth = w_in.shape[0]
    assert depth == 1, "lower-bound tables are evaluated for a single layer"
    T = B * L
    slopes = _alibi_slopes(DA_HEADS)
    hg_cols = 5 * w_proj_hg.shape[1]
    x2 = x.reshape(T, D)
    for l in range(depth):
        mem_k, mem_v = _mem_kv(mem, norm_mem_gain[l][None], w_mem_kv[l].astype(BF16),
                               mem_k_gain[l][None])
        w_in_l = w_in[l].astype(BF16)
        h, hq, lf_fw, k_fw, lf_bw, k_bw, hv, hg = _proj_hgrn(
            x2, norm_mix_gain[l][None], w_in_l, hg_cols,
            lb_logits_fw.astype(F32), lb_logits_bw.astype(F32))
        (q0, q1, q2, k0, k1, k2, v0, v1, v2, mem_q, gates) = _proj_attn(
            h, w_in_l, hg_cols, da_q_gain[l][None], da_k_gain[l][None],
            mem_q_gain[l][None], B, L)

        def seq(t):
            return t.reshape(B, L, t.shape[-1])

        o_hg = _hgrn(seq(hq), seq(lf_fw), seq(k_fw), seq(lf_bw), seq(k_bw), seq(hv), seq(hg),
                     hg_norm_gain[l][None])
        o_da = _dilated((q0, q1, q2), (k0, k1, k2), (v0, v1, v2), slopes)

        x2 = _mix(x2, o_hg.reshape(T, -1), o_da.reshape(T, -1), mem_q, mem_k, mem_v, gates,
                  w_proj_hg[l].astype(BF16), w_proj_da[l].astype(BF16),
                  w_proj_mem[l].astype(BF16), w_out[l].astype(BF16), L)
        x2 = _ffn(x2, norm_ffn_gain[l][None], w_ffn_in[l].astype(BF16), w_ffn_out[l].astype(BF16))
    return x2.reshape(B, L, D)
```
